```python
import jax, jax.numpy as jnp
from jax import lax
import numpy as np

D_MODEL = 2048
BATCH = 1
SEQ = 8192
DEPTH = 1
DEC_BATCH = 32
DEC_SEQ = 8
PAST_LEN = 16384
PAGE_SIZE = 128

N_META = 16
N_HEADS = 16
N_KV_HEADS = 4
HEAD_DIM = 64
GROUP = N_HEADS // N_KV_HEADS
D_ATTN = N_HEADS * HEAD_DIM
D_KV = N_KV_HEADS * HEAD_DIM
ATTN_SCALE = HEAD_DIM ** -0.5
N_IDX_HEADS = 16
IDX_DIM = 64
INDEX_W_SCALE = (N_IDX_HEADS ** -0.5) * (IDX_DIM ** -0.5)
TOPK_MAX = 256
POOL_WINDOWS = (2, 4, 8, 16)
N_POOL_GROUPS = 4
D_POOL = D_MODEL // 2
POOL_GROUP = D_POOL // N_POOL_GROUPS
POOL_OUT_GROUP = D_MODEL // N_POOL_GROUPS
POOL_BUF = max(POOL_WINDOWS) - 1
D_FF = 4 * D_MODEL
ROPE_THETA = 10000.0
EPS = 1e-6
Q_BLOCK = 128
SPLITS = (D_POOL, D_ATTN, D_KV, D_KV, N_IDX_HEADS * IDX_DIM, IDX_DIM, N_IDX_HEADS, D_MODEL, D_MODEL)
D_IN = sum(SPLITS)
F32 = jnp.float32

kernel_name = 'hybrid_pool_dsa_gated_decode_step'


def rms_norm(x, g):
    x32 = x.astype(F32)
    y = x32 * lax.rsqrt(jnp.mean(x32 * x32, axis=-1, keepdims=True) + EPS) * g.astype(F32)
    return y.astype(x.dtype)


def rope(x, pos):
    d = x.shape[-1]
    half = d // 2
    inv_freq = ROPE_THETA ** (-(jnp.arange(half, dtype=F32) * 2.0 / d))
    ang = pos.astype(F32)[..., None] * inv_freq
    cos = jnp.cos(ang)[..., None, :]
    sin = jnp.sin(ang)[..., None, :]
    x1 = x[..., :half].astype(F32)
    x2 = x[..., half:].astype(F32)
    return jnp.concatenate([x1 * cos - x2 * sin, x2 * cos + x1 * sin], axis=-1).astype(x.dtype)


def split_proj(h, w_in, idx_k_norm, pos):
    B, T, _ = h.shape
    points = [int(p) for p in np.cumsum(SPLITS)[:-1]]
    u, q, k, v, qi, ki, wi, gp, ga = jnp.split(h @ w_in, points, axis=-1)
    q = rope(q.reshape(B, T, N_HEADS, HEAD_DIM), pos)
    k = rope(k.reshape(B, T, N_KV_HEADS, HEAD_DIM), pos)
    v = v.reshape(B, T, N_KV_HEADS, HEAD_DIM)
    qi = rope(qi.reshape(B, T, N_IDX_HEADS, IDX_DIM), pos)
    ki = rope(rms_norm(ki, idx_k_norm)[:, :, None, :], pos)[:, :, 0, :]
    wi = wi.astype(F32) * INDEX_W_SCALE
    return u, q, k, v, qi, ki, wi, gp, ga


def pool_mix(u_ext, pos, w_pool, pool_scale):
    B, L, _ = u_ext.shape
    T = L - POOL_BUF
    cs = jnp.cumsum(u_ext.astype(F32), axis=1)
    cs = jnp.concatenate([jnp.zeros((B, 1, D_POOL), F32), cs], axis=1)
    end = cs[:, POOL_BUF + 1:]
    cur = u_ext[:, POOL_BUF:].astype(F32)
    outs = []
    for g, w in enumerate(POOL_WINDOWS):
        lo, hi = g * POOL_GROUP, (g + 1) * POOL_GROUP
        start = cs[:, POOL_BUF + 1 - w: POOL_BUF + 1 - w + T, lo:hi]
        cnt = jnp.minimum(pos + 1, w).astype(F32)[None, :, None]
        outs.append((end[..., lo:hi] - start) / cnt - cur[..., lo:hi])
    pooled = jnp.stack(outs, axis=2)
    y = jnp.einsum('btgc,gcd->btgd', pooled, w_pool.astype(F32)).reshape(B, T, D_MODEL)
    return (y * pool_scale.astype(F32)).astype(u_ext.dtype)


def index_scores(qi, ki, wi):
    s = jnp.einsum('bqhd,bld->bqhl', qi.astype(F32), ki.astype(F32))
    return jnp.einsum('bqhl,bqh->bql', jax.nn.relu(s), wi)


def sparse_attend(q, ks, vs, valid):
    B, Q = q.shape[:2]
    qg = q.astype(F32).reshape(B, Q, N_KV_HEADS, GROUP, HEAD_DIM)
    s = jnp.einsum('bqngd,bqknd->bqngk', qg, ks.astype(F32)) * ATTN_SCALE
    s = jnp.where(valid[:, :, None, None, :], s, -jnp.inf)
    p = jax.nn.softmax(s, axis=-1)
    o = jnp.einsum('bqngk,bqknd->bqngd', p, vs.astype(F32))
    return o.reshape(B, Q, D_ATTN).astype(q.dtype)


gather_rows = jax.vmap(lambda rows, idx: rows[idx])


def dsa_prompt(q, k, v, qi, ki, wi):
    B, T = q.shape[:2]
    topk = min(TOPK_MAX, SEQ // 4)
    n_blk = -(-T // Q_BLOCK)
    pad = n_blk * Q_BLOCK - T

    def blocks(a):
        a = jnp.pad(a, [(0, 0), (0, pad)] + [(0, 0)] * (a.ndim - 2))
        return jnp.moveaxis(a.reshape((B, n_blk, Q_BLOCK) + a.shape[2:]), 1, 0)

    key_pos = jnp.arange(T)

    def one_block(args):
        i, q_b, qi_b, wi_b = args
        q_pos = i * Q_BLOCK + jnp.arange(Q_BLOCK)
        score = index_scores(qi_b, ki, wi_b)
        score = jnp.where(key_pos[None, None, :] <= q_pos[None, :, None], score, -jnp.inf)
        _, sel = lax.top_k(score, topk)
        valid = sel <= q_pos[None, :, None]
        return sparse_attend(q_b, gather_rows(k, sel), gather_rows(v, sel), valid)

    out = lax.map(one_block, (jnp.arange(n_blk), blocks(q), blocks(qi), blocks(wi)))
    out = jnp.moveaxis(out, 0, 1).reshape(B, n_blk * Q_BLOCK, D_ATTN)
    return out[:, :T]


def dsa_sample(q, k_new, v_new, qi, ki_new, wi, ck, cv, ckidx, page_table):
    Bd, S = q.shape[:2]
    L = PAST_LEN + S
    topk = min(TOPK_MAX, L // 4)
    ki_past = ckidx[page_table].reshape(Bd, PAST_LEN, IDX_DIM)
    ki_all = jnp.concatenate([ki_past.astype(ki_new.dtype), ki_new], axis=1)
    q_pos = PAST_LEN + jnp.arange(S)
    score = index_scores(qi, ki_all, wi)
    score = jnp.where(jnp.arange(L)[None, None, :] <= q_pos[None, :, None], score, -jnp.inf)
    _, sel = lax.top_k(score, topk)
    valid = sel <= q_pos[None, :, None]
    in_past = sel < PAST_LEN
    p_idx = jnp.minimum(sel, PAST_LEN - 1)
    phys = page_table[jnp.arange(Bd)[:, None, None], p_idx // PAGE_SIZE] * PAGE_SIZE + p_idx % PAGE_SIZE
    n_idx = jnp.clip(sel - PAST_LEN, 0, S - 1)

    def pick(pool, new):
        past = pool.reshape((-1,) + pool.shape[2:])[phys]
        return jnp.where(in_past[..., None, None], past.astype(new.dtype), gather_rows(new, n_idx))

    return sparse_attend(q, pick(ck, k_new), pick(cv, v_new), valid)


def merge_and_ffn(x, pool_out, attn, gp, ga, w_attn_o, w_out, g_post_mix, g_pre_ffn, w_up, w_down, g_post_ffn):
    attn_out = attn @ w_attn_o
    m = (jax.nn.sigmoid(gp.astype(F32)) * pool_out.astype(F32)
         + jax.nn.sigmoid(ga.astype(F32)) * attn_out.astype(F32))
    mix = m.astype(x.dtype) @ w_out
    x = x + rms_norm(mix, g_post_mix)
    f = jnp.square(jax.nn.relu(rms_norm(x, g_pre_ffn) @ w_up)) @ w_down
    return x + rms_norm(f, g_post_ffn)


def setup_inputs(seed: int = 0) -> dict:
    key = jax.random.key(seed)
    ks = jax.random.split(key, 20)
    n_pages = PAST_LEN // PAGE_SIZE
    n_used = DEC_BATCH * n_pages
    n_pool = n_used + max(1, n_used // 4)

    def nrm(k, shape, scale=1.0):
        return jax.random.normal(k, shape, F32) * scale

    def gain(k, shape):
        return 1.0 + 0.05 * jax.random.normal(k, shape, F32)

    page_table = jax.random.permutation(ks[0], n_pool)[:n_used].reshape(DEC_BATCH, n_pages).astype(jnp.int32)
    return {
        'x_prompt': nrm(ks[1], (BATCH, SEQ, D_MODEL)),
        'x_sample': nrm(ks[2], (DEC_BATCH, DEC_SEQ, D_MODEL)),
        'cache_k': nrm(ks[3], (DEPTH, n_pool, PAGE_SIZE, N_KV_HEADS, HEAD_DIM)),
        'cache_v': nrm(ks[4], (DEPTH, n_pool, PAGE_SIZE, N_KV_HEADS, HEAD_DIM)),
        'cache_kidx': nrm(ks[5], (DEPTH, n_pool, PAGE_SIZE, IDX_DIM)),
        'state_pool': nrm(ks[6], (DEPTH, DEC_BATCH, POOL_BUF, D_POOL)),
        'page_table': page_table,
        'meta_tokens': nrm(ks[7], (N_META, D_MODEL)),
        'norm_mix_pre': gain(ks[8], (DEPTH, D_MODEL)),
        'w_in': nrm(ks[9], (DEPTH, D_MODEL, D_IN), D_MODEL ** -0.5),
        'idx_k_norm': gain(ks[10], (DEPTH, IDX_DIM)),
        'w_pool': nrm(ks[11], (DEPTH, N_POOL_GROUPS, POOL_GROUP, POOL_OUT_GROUP), POOL_GROUP ** -0.5),
        'pool_scale': gain(ks[12], (DEPTH, D_MODEL)),
        'w_attn_o': nrm(ks[13], (DEPTH, D_ATTN, D_MODEL), D_ATTN ** -0.5),
        'w_out': nrm(ks[14], (DEPTH, D_MODEL, D_MODEL), D_MODEL ** -0.5),
        'norm_mix_post': gain(ks[15], (DEPTH, D_MODEL)),
        'norm_ffn_pre': gain(ks[16], (DEPTH, D_MODEL)),
        'w_up': nrm(ks[17], (DEPTH, D_MODEL, D_FF), D_MODEL ** -0.5),
        'w_down': nrm(ks[18], (DEPTH, D_FF, D_MODEL), D_FF ** -0.5),
        'norm_ffn_post': gain(ks[19], (DEPTH, D_MODEL)),
    }


def reference(x_prompt, x_sample, cache_k, cache_v, cache_kidx, state_pool, page_table,
              meta_tokens, norm_mix_pre, w_in, idx_k_norm, w_pool, pool_scale, w_attn_o,
              w_out, norm_mix_post, norm_ffn_pre, w_up, w_down, norm_ffn_post):
    B = x_prompt.shape[0]
    meta = jnp.broadcast_to(meta_tokens.astype(x_prompt.dtype)[None], (B, N_META, D_MODEL))
    xp = jnp.concatenate([meta, x_prompt], axis=1)
    xs = x_sample
    T = xp.shape[1]
    S = xs.shape[1]
    pos_p = jnp.arange(T)[None]
    pos_s = (PAST_LEN + jnp.arange(S))[None]
    k_p_l = []
    v_p_l = []
    ki_p_l = []
    pool_p_l = []
    k_s_l = []
    v_s_l = []
    ki_s_l = []
    pool_s_l = []
    for l in range(DEPTH):
        lw = (w_attn_o[l], w_out[l], norm_mix_post[l], norm_ffn_pre[l], w_up[l], w_down[l], norm_ffn_post[l])
        u, q, k, v, qi, ki, wi, gp, ga = split_proj(rms_norm(xp, norm_mix_pre[l]), w_in[l], idx_k_norm[l], pos_p)
        u_ext = jnp.concatenate([jnp.zeros((B, POOL_BUF, D_POOL), u.dtype), u], axis=1)
        pool_out = pool_mix(u_ext, pos_p[0], w_pool[l], pool_scale[l])
        attn = dsa_prompt(q, k, v, qi, ki, wi)
        xp = merge_and_ffn(xp, pool_out, attn, gp, ga, *lw)
        k_p_l.append(k)
        v_p_l.append(v)
        ki_p_l.append(ki)
        pool_p_l.append(u_ext[:, -POOL_BUF:])
        u, q, k, v, qi, ki, wi, gp, ga = split_proj(rms_norm(xs, norm_mix_pre[l]), w_in[l], idx_k_norm[l], pos_s)
        u_ext = jnp.concatenate([state_pool[l].astype(u.dtype), u], axis=1)
        pool_out = pool_mix(u_ext, pos_s[0], w_pool[l], pool_scale[l])
        attn = dsa_sample(q, k, v, qi, ki, wi, cache_k[l], cache_v[l], cache_kidx[l], page_table)
        xs = merge_and_ffn(xs, pool_out, attn, gp, ga, *lw)
        k_s_l.append(k)
        v_s_l.append(v)
        ki_s_l.append(ki)
        pool_s_l.append(u_ext[:, -POOL_BUF:])
    y_prompt = xp[:, N_META:]
    y_sample = xs
    return (y_prompt, y_sample,
            jnp.stack(k_p_l), jnp.stack(v_p_l), jnp.stack(ki_p_l), jnp.stack(pool_p_l),
            jnp.stack(k_s_l), jnp.stack(v_s_l), jnp.stack(ki_s_l), jnp.stack(pool_s_l))
```

```python
import functools

import jax
import jax.numpy as jnp
import numpy as np
from jax import lax
from jax.experimental import pallas as pl
from jax.experimental.pallas import tpu as pltpu

F32 = jnp.float32
BF16 = jnp.bfloat16
I32 = jnp.int32

D_MODEL = 2048
SEQ = 8192
DEC_BATCH = 32
DEC_SEQ = 8
PAST_LEN = 16384
PAGE_SIZE = 128
N_PAGES = PAST_LEN // PAGE_SIZE
N_META = 16
N_HEADS = 16
N_KV_HEADS = 4
HEAD_DIM = 64
GROUP = N_HEADS // N_KV_HEADS
D_ATTN = N_HEADS * HEAD_DIM
D_KV = N_KV_HEADS * HEAD_DIM
ATTN_SCALE = HEAD_DIM ** -0.5
N_IDX_HEADS = 16
IDX_DIM = 64
INDEX_W_SCALE = (N_IDX_HEADS ** -0.5) * (IDX_DIM ** -0.5)
TOPK = 256
POOL_WINDOWS = (2, 4, 8, 16)
N_POOL_GROUPS = 4
D_POOL = D_MODEL // 2
POOL_GROUP = D_POOL // N_POOL_GROUPS
POOL_OUT_GROUP = D_MODEL // N_POOL_GROUPS
POOL_BUF = max(POOL_WINDOWS) - 1
D_FF = 4 * D_MODEL
ROPE_THETA = 10000.0
EPS = 1e-6

LANES = 128
T = SEQ + N_META
QB = 128
N_QBLK = -(-T // QB)
ROW_S = N_QBLK * QB
NS = DEC_BATCH * DEC_SEQ
R = 8704
VMEM_LIMIT = 56 * 1024 * 1024

NEG = -1e30
KEY_MASKED = -2 ** 31


def _cparams(sem):
    return pltpu.CompilerParams(dimension_semantics=sem, vmem_limit_bytes=VMEM_LIMIT)


def _rms(x, g):
    return x * lax.rsqrt(jnp.mean(x * x, axis=-1, keepdims=True) + EPS) * g


def _swap_halves(x):
    lane = lax.broadcasted_iota(I32, x.shape, 1)
    return jnp.where(lane % HEAD_DIM < HEAD_DIM // 2,
                     pltpu.roll(x, LANES - HEAD_DIM // 2, 1),
                     pltpu.roll(x, HEAD_DIM // 2, 1))


def _rope_cols(x, cos, sin):
    outs = []
    for c in range(x.shape[1] // LANES):
        xc = x[:, c * LANES:(c + 1) * LANES]
        outs.append(xc * cos + _swap_halves(xc) * sin)
    return outs


PROJ_TM = 544
N_ROPE_A = D_ATTN + D_KV
PROJ_TN_A = N_ROPE_A
W_A_COLS = 2 * PROJ_TN_A
W_B_COLS = D_KV + D_POOL
W_C_COLS = 2 * D_MODEL
PROJ_TN_C = 1024


def _norm_to_scratch(x_ref, g_ref, xn_ref):
    @pl.when(pl.program_id(1) == 0)
    def _():
        xn_ref[...] = _rms(x_ref[...], g_ref[...]).astype(BF16)


def _proj_a_kernel(x_ref, g_ref, w_ref, cos_ref, sin_ref, gk_ref,
                   q_ref, k_ref, kb_ref, qi_ref, ki_ref, kib_ref, wi_ref, xn_ref):
    _norm_to_scratch(x_ref, g_ref, xn_ref)
    j = pl.program_id(1)
    p = jnp.dot(xn_ref[...], w_ref[...], preferred_element_type=F32)
    cos = cos_ref[...]
    sin = sin_ref[...]

    @pl.when(j == 0)
    def _():
        cols = _rope_cols(p, cos, sin)
        nq = D_ATTN // LANES
        for c in range(nq):
            q_ref[:, c * LANES:(c + 1) * LANES] = cols[c]
        for c in range(D_KV // LANES):
            k_ref[:, c * LANES:(c + 1) * LANES] = cols[nq + c]
            kb_ref[:, c * LANES:(c + 1) * LANES] = cols[nq + c].astype(BF16)

    @pl.when(j == 1)
    def _():
        nqi = N_IDX_HEADS * IDX_DIM // LANES
        cols = _rope_cols(p[:, :nqi * LANES], cos, sin)
        for c in range(nqi):
            qi_ref[:, c * LANES:(c + 1) * LANES] = cols[c]
        slab = p[:, nqi * LANES:(nqi + 1) * LANES]
        lane = lax.broadcasted_iota(I32, slab.shape, 1)
        is_ki = lane < IDX_DIM
        ms = jnp.sum(jnp.where(is_ki, slab * slab, 0.0), axis=-1, keepdims=True) / IDX_DIM
        kin = slab * lax.rsqrt(ms + EPS) * gk_ref[...]
        kin = kin * cos + _swap_halves(kin) * sin
        ki2 = jnp.where(is_ki, kin, pltpu.roll(kin, IDX_DIM, 1))
        ki_ref[...] = ki2
        kib_ref[...] = ki2.astype(BF16)
        wi_ref[...] = pltpu.roll(slab, LANES - IDX_DIM, 1) * INDEX_W_SCALE


def _proj_a(x_all, g, w_a, cos, sin, gk):
    n_i = R // PROJ_TM
    row = lambda i, j: (i, 0)
    outs = (
        jax.ShapeDtypeStruct((R, D_ATTN), F32),
        jax.ShapeDtypeStruct((R, D_KV), F32),
        jax.ShapeDtypeStruct((R, D_KV), BF16),
        jax.ShapeDtypeStruct((R, N_IDX_HEADS * IDX_DIM), F32),
        jax.ShapeDtypeStruct((R, LANES), F32),
        jax.ShapeDtypeStruct((R, LANES), BF16),
        jax.ShapeDtypeStruct((R, LANES), F32),
    )
    return pl.pallas_call(
        _proj_a_kernel,
        grid=(n_i, 2),
        in_specs=[
            pl.BlockSpec((PROJ_TM, D_MODEL), row),
            pl.BlockSpec((1, D_MODEL), lambda i, j: (0, 0)),
            pl.BlockSpec((D_MODEL, PROJ_TN_A), lambda i, j: (0, j)),
            pl.BlockSpec((PROJ_TM, LANES), row),
            pl.BlockSpec((PROJ_TM, LANES), row),
            pl.BlockSpec((1, LANES), lambda i, j: (0, 0)),
        ],
        out_specs=[pl.BlockSpec((PROJ_TM, o.shape[1]), row) for o in outs],
        out_shape=outs,
        scratch_shapes=[pltpu.VMEM((PROJ_TM, D_MODEL), BF16)],
        compiler_params=_cparams(("arbitrary", "arbitrary")),
        name="proj_rope",
    )(x_all, g, w_a, cos, sin, gk)


def _proj_b_kernel(x_ref, g_ref, w_ref, v_ref, vb_ref, u_ref, xn_ref):
    _norm_to_scratch(x_ref, g_ref, xn_ref)
    p = jnp.dot(xn_ref[...], w_ref[...], preferred_element_type=F32)
    v_ref[...] = p[:, :D_KV]
    vb_ref[...] = p[:, :D_KV].astype(BF16)
    u_ref[...] = p[:, D_KV:]


def _proj_b(x_all, g, w_b):
    n_i = R // PROJ_TM
    row = lambda i, j: (i, 0)
    outs = (
        jax.ShapeDtypeStruct((R, D_KV), F32),
        jax.ShapeDtypeStruct((R, D_KV), BF16),
        jax.ShapeDtypeStruct((R, D_POOL), F32),
    )
    return pl.pallas_call(
        _proj_b_kernel,
        grid=(n_i, 1),
        in_specs=[
            pl.BlockSpec((PROJ_TM, D_MODEL), row),
            pl.BlockSpec((1, D_MODEL), lambda i, j: (0, 0)),
            pl.BlockSpec((D_MODEL, W_B_COLS), lambda i, j: (0, 0)),
        ],
        out_specs=[pl.BlockSpec((PROJ_TM, o.shape[1]), row) for o in outs],
        out_shape=outs,
        scratch_shapes=[pltpu.VMEM((PROJ_TM, D_MODEL), BF16)],
        compiler_params=_cparams(("arbitrary", "arbitrary")),
        name="proj_vu",
    )(x_all, g, w_b)


def _proj_c_kernel(x_ref, g_ref, w_ref, o_ref, xn_ref):
    _norm_to_scratch(x_ref, g_ref, xn_ref)
    o_ref[...] = jnp.dot(xn_ref[...], w_ref[...], preferred_element_type=F32)


def _proj_c(x_all, g, w_c):
    n_i = R // PROJ_TM
    return pl.pallas_call(
        _proj_c_kernel,
        grid=(n_i, W_C_COLS // PROJ_TN_C),
        in_specs=[
            pl.BlockSpec((PROJ_TM, D_MODEL), lambda i, j: (i, 0)),
            pl.BlockSpec((1, D_MODEL), lambda i, j: (0, 0)),
            pl.BlockSpec((D_MODEL, PROJ_TN_C), lambda i, j: (0, j)),
        ],
        out_specs=pl.BlockSpec((PROJ_TM, PROJ_TN_C), lambda i, j: (i, j)),
        out_shape=jax.ShapeDtypeStruct((R, W_C_COLS), F32),
        scratch_shapes=[pltpu.VMEM((PROJ_TM, D_MODEL), BF16)],
        compiler_params=_cparams(("arbitrary", "arbitrary")),
        name="proj_gates",
    )(x_all, g, w_c)


POOL_TM = 512
HALO = 16


def _window_mean_minus_cur(ext_ref, rows, inv_cnt):
    outs = []
    for g, w in enumerate(POOL_WINDOWS):
        cols = slice(g * POOL_GROUP, (g + 1) * POOL_GROUP)
        cur = ext_ref[HALO:HALO + rows, cols]
        acc = cur
        for d in range(1, w):
            acc = acc + ext_ref[HALO - d:HALO - d + rows, cols]
        outs.append(acc * inv_cnt[g] - cur)
    return outs


def _pool_prompt_kernel(u_ref, halo_ref, o_ref, ext_ref):
    i = pl.program_id(0)
    ext_ref[HALO:, :] = u_ref[...]
    ext_ref[:HALO, :] = jnp.where(i == 0, 0.0, halo_ref[...])
    pos = i * POOL_TM + lax.broadcasted_iota(I32, (POOL_TM, 1), 0)
    inv_cnt = [1.0 / jnp.minimum(pos + 1, w).astype(F32) for w in POOL_WINDOWS]
    outs = _window_mean_minus_cur(ext_ref, POOL_TM, inv_cnt)
    for g in range(N_POOL_GROUPS):
        o_ref[:, g * POOL_GROUP:(g + 1) * POOL_GROUP] = outs[g]


def _pool_prompt(u):
    per = POOL_TM // HALO
    return pl.pallas_call(
        _pool_prompt_kernel,
        grid=(R // POOL_TM,),
        in_specs=[
            pl.BlockSpec((POOL_TM, D_POOL), lambda i: (i, 0)),
            pl.BlockSpec((HALO, D_POOL), lambda i: (jnp.maximum(i * per - 1, 0), 0)),
        ],
        out_specs=pl.BlockSpec((POOL_TM, D_POOL), lambda i: (i, 0)),
        out_shape=jax.ShapeDtypeStruct((R, D_POOL), F32),
        scratch_shapes=[pltpu.VMEM((POOL_TM + HALO, D_POOL), F32)],
        compiler_params=_cparams(("arbitrary",)),
        name="pool_prompt",
    )(u, u)


def _pool_sample_kernel(u_ref, st_ref, o_ref, ext_ref):
    for b in range(DEC_BATCH):
        ext_ref[:HALO, :] = st_ref[b]
        ext_ref[HALO:, :] = u_ref[b]
        inv_cnt = [1.0 / w for w in POOL_WINDOWS]
        outs = _window_mean_minus_cur(ext_ref, DEC_SEQ, inv_cnt)
        for g in range(N_POOL_GROUPS):
            o_ref[b, :, g * POOL_GROUP:(g + 1) * POOL_GROUP] = outs[g]


def _pool_sample(u_s, state16):
    return pl.pallas_call(
        _pool_sample_kernel,
        out_shape=jax.ShapeDtypeStruct((DEC_BATCH, DEC_SEQ, D_POOL), F32),
        scratch_shapes=[pltpu.VMEM((HALO + DEC_SEQ, D_POOL), F32)],
        compiler_params=pltpu.CompilerParams(vmem_limit_bytes=VMEM_LIMIT),
        name="pool_sample",
    )(u_s, state16)


MERGE_TM = 256


def _merge_kernel(x_ref, pooled_ref, attn_ref, gate_ref, wp_ref, ps_ref, wa_ref, wo_ref, gn_ref, o_ref):
    pooled = pooled_ref[...].astype(BF16)
    pool_out = jnp.concatenate(
        [jnp.dot(pooled[:, g * POOL_GROUP:(g + 1) * POOL_GROUP], wp_ref[g], preferred_element_type=F32)
         for g in range(N_POOL_GROUPS)], axis=1) * ps_ref[...]
    attn_out = jnp.dot(attn_ref[...], wa_ref[...], preferred_element_type=F32)
    gate = gate_ref[...]
    m = (jax.nn.sigmoid(gate[:, :D_MODEL]) * pool_out
         + jax.nn.sigmoid(gate[:, D_MODEL:]) * attn_out)
    mix = jnp.dot(m.astype(BF16), wo_ref[...], preferred_element_type=F32)
    o_ref[...] = x_ref[...] + _rms(mix, gn_ref[...])


def _merge(x_all, pooled, attn, gates, w_pool, pool_scale, w_attn_o, w_out, g_post):
    row = lambda i: (i, 0)
    const2 = lambda i: (0, 0)
    return pl.pallas_call(
        _merge_kernel,
        grid=(R // MERGE_TM,),
        in_specs=[
            pl.BlockSpec((MERGE_TM, D_MODEL), row),
            pl.BlockSpec((MERGE_TM, D_POOL), row),
            pl.BlockSpec((MERGE_TM, D_ATTN), row),
            pl.BlockSpec((MERGE_TM, 2 * D_MODEL), row),
            pl.BlockSpec((N_POOL_GROUPS, POOL_GROUP, POOL_OUT_GROUP), lambda i: (0, 0, 0)),
            pl.BlockSpec((1, D_MODEL), const2),
            pl.BlockSpec((D_ATTN, D_MODEL), const2),
            pl.BlockSpec((D_MODEL, D_MODEL), const2),
            pl.BlockSpec((1, D_MODEL), const2),
        ],
        out_specs=pl.BlockSpec((MERGE_TM, D_MODEL), row),
        out_shape=jax.ShapeDtypeStruct((R, D_MODEL), F32),
        compiler_params=_cparams(("arbitrary",)),
        name="merge",
    )(x_all, pooled, attn, gates, w_pool, pool_scale, w_attn_o, w_out, g_post)


FFN_TM = 544
FFN_TF = 512


def _ffn_kernel(x_ref, gpre_ref, wu_ref, wd_ref, gpost_ref, o_ref, h_ref, acc_ref):
    j = pl.program_id(1)

    @pl.when(j == 0)
    def _():
        h_ref[...] = _rms(x_ref[...], gpre_ref[...]).astype(BF16)
        acc_ref[...] = jnp.zeros_like(acc_ref)

    a = jnp.maximum(jnp.dot(h_ref[...], wu_ref[...], preferred_element_type=F32), 0.0)
    acc_ref[...] += jnp.dot((a * a).astype(BF16), wd_ref[...], preferred_element_type=F32)

    @pl.when(j == pl.num_programs(1) - 1)
    def _():
        o_ref[...] = x_ref[...] + _rms(acc_ref[...], gpost_ref[...])


def _ffn(x1, g_pre, w_up, w_down, g_post):
    return pl.pallas_call(
        _ffn_kernel,
        grid=(R // FFN_TM, D_FF // FFN_TF),
        in_specs=[
            pl.BlockSpec((FFN_TM, D_MODEL), lambda i, j: (i, 0)),
            pl.BlockSpec((1, D_MODEL), lambda i, j: (0, 0)),
            pl.BlockSpec((D_MODEL, FFN_TF), lambda i, j: (0, j)),
            pl.BlockSpec((FFN_TF, D_MODEL), lambda i, j: (j, 0)),
            pl.BlockSpec((1, D_MODEL), lambda i, j: (0, 0)),
        ],
        out_specs=pl.BlockSpec((FFN_TM, D_MODEL), lambda i, j: (i, 0)),
        out_shape=jax.ShapeDtypeStruct((R, D_MODEL), F32),
        scratch_shapes=[pltpu.VMEM((FFN_TM, D_MODEL), BF16), pltpu.VMEM((FFN_TM, D_MODEL), F32)],
        compiler_params=_cparams(("arbitrary", "arbitrary")),
        name="ffn",
    )(x1, g_pre, w_up, w_down, g_post)


INT_MAX = 2 ** 31 - 1
MAX_BISECT = 34


def _score_key(s):
    bits = pltpu.bitcast(s, I32)
    return jnp.where(bits < 0, bits ^ INT_MAX, bits)


def _floor_avg(lo, hi):
    return (lo >> 1) + (hi >> 1) + (lo & hi & 1)


def _bisect_threshold(count_ge, active0, rows):
    lo0 = jnp.full((rows, 1), KEY_MASKED + 1, I32)
    hi0 = jnp.full((rows, 1), INT_MAX, I32)

    def cond(st):
        it, _, _, active = st
        return jnp.logical_and(it < MAX_BISECT, jnp.max(active) > 0)

    def body(st):
        it, lo, hi, active = st
        mid = _floor_avg(lo, hi)
        cnt = count_ge(mid)
        on = active > 0
        ge = cnt >= TOPK
        lo = jnp.where(jnp.logical_and(on, ge), mid, lo)
        hi = jnp.where(jnp.logical_and(on, jnp.logical_not(ge)), mid, hi)
        still = jnp.logical_and(cnt != TOPK, hi - lo > 1)
        return it + 1, lo, hi, jnp.where(jnp.logical_and(on, still), 1, 0).astype(I32)

    _, lo, _, _ = lax.while_loop(cond, body, (jnp.int32(0), lo0, hi0, active0.astype(I32)))
    return lo


def _head_slab(x_ref, h):
    j = h // 2
    return x_ref[:, j * LANES:(j + 1) * LANES]


def _place_half(slab, src_half, dst_half):
    lane = lax.broadcasted_iota(I32, slab.shape, 1)
    x = slab if src_half == dst_half else pltpu.roll(slab, HEAD_DIM, 1)
    return jnp.where((lane >= HEAD_DIM) == (dst_half == 1), x, 0.0)


def _gather_heads(o, rows):
    outs = []
    for j in range(N_HEADS // 2):
        n = (2 * j) // GROUP
        tc, th = n // 2, n % 2
        a0 = o[(2 * j) * rows:(2 * j + 1) * rows, tc * LANES:(tc + 1) * LANES]
        a1 = o[(2 * j + 1) * rows:(2 * j + 2) * rows, tc * LANES:(tc + 1) * LANES]
        x0 = a0 if th == 0 else pltpu.roll(a0, HEAD_DIM, 1)
        x1 = a1 if th == 1 else pltpu.roll(a1, HEAD_DIM, 1)
        lane = lax.broadcasted_iota(I32, x0.shape, 1)
        outs.append(jnp.where(lane < HEAD_DIM, x0, x1))
    return outs


def _build_qbd(q_ref, qbd_ref, rows):
    for h in range(N_HEADS):
        n = h // GROUP
        tc, th = n // 2, n % 2
        placed = _place_half(_head_slab(q_ref, h) * ATTN_SCALE, h % 2, th).astype(qbd_ref.dtype)
        for c in range(D_KV // LANES):
            qbd_ref[h * rows:(h + 1) * rows, c * LANES:(c + 1) * LANES] = (
                placed if c == tc else jnp.zeros_like(placed))


def _softmax_step(a3, m_ref, l_ref, acc_ref, vc, rows):
    n = N_HEADS * rows
    ch = a3.shape[2]
    m_prev = m_ref[...].reshape(N_HEADS, rows, 1)
    m_new = jnp.maximum(m_prev, jnp.max(a3, axis=2, keepdims=True))
    p = jnp.exp(a3 - m_new)
    alpha = jnp.exp(m_prev - m_new)
    l_ref[...] = (alpha * l_ref[...].reshape(N_HEADS, rows, 1)
                  + jnp.sum(p, axis=2, keepdims=True)).reshape(n, 1)
    m_ref[...] = m_new.reshape(n, 1)
    pv = jnp.dot(p.reshape(n, ch).astype(BF16), vc, preferred_element_type=F32)
    acc_ref[...] = alpha.reshape(n, 1) * acc_ref[...] + pv


ATT_CH = 256
N_ATT_CH = R // ATT_CH
NT_DIMS = (((1,), (1,)), ((), ()))


def _attn_prompt_kernel(qi_ref, wi_ref, q_ref, kib_ref, kb_ref, vb_ref, o_ref,
                        s_ref, qim_ref, wb_ref, qbd_ref, m_ref, l_ref, acc_ref):
    i = pl.program_id(0)

    @pl.when(i >= N_QBLK)
    def _():
        o_ref[...] = jnp.zeros_like(o_ref)

    @pl.when(i < N_QBLK)
    def _():
        n_ch = (i * QB) // ATT_CH + 1
        qrow = i * QB + lax.broadcasted_iota(I32, (QB, 1), 0)

        wi = wi_ref[...]
        for h in range(N_IDX_HEADS):
            qim_ref[h * QB:(h + 1) * QB, :] = _place_half(_head_slab(qi_ref, h), h % 2, h % 2).astype(BF16)
            wb_ref[h] = jnp.broadcast_to(wi[:, h:h + 1], (QB, LANES))
        _build_qbd(q_ref, qbd_ref, QB)

        def score_chunk(c, carry):
            start = pl.multiple_of(c * ATT_CH, ATT_CH)
            kc = kib_ref[pl.ds(start, ATT_CH), :]
            acc = [jnp.zeros((QB, LANES), F32) for _ in range(ATT_CH // LANES)]
            hpd = 4
            for hg in range(N_IDX_HEADS // hpd):
                d = lax.dot_general(qim_ref[hg * hpd * QB:(hg + 1) * hpd * QB, :], kc, NT_DIMS,
                                    preferred_element_type=F32)
                for hh in range(hpd):
                    w = wb_ref[hg * hpd + hh]
                    for t in range(ATT_CH // LANES):
                        acc[t] = acc[t] + w * jnp.maximum(
                            d[hh * QB:(hh + 1) * QB, t * LANES:(t + 1) * LANES], 0.0)
            for t in range(ATT_CH // LANES):
                kpos = start + t * LANES + lax.broadcasted_iota(I32, (QB, LANES), 1)
                s_ref[c, :, t * LANES:(t + 1) * LANES] = jnp.where(
                    kpos <= qrow, _score_key(acc[t]), KEY_MASKED)
            return carry

        lax.fori_loop(0, n_ch, score_chunk, 0)

        def count_ge(mid):
            mid_b = jnp.broadcast_to(mid, (QB, LANES))

            def body(c, cnt):
                keys = s_ref[c]
                for t in range(ATT_CH // LANES):
                    cnt = cnt + jnp.where(keys[:, t * LANES:(t + 1) * LANES] >= mid_b, 1.0, 0.0)
                return cnt

            cnt = lax.fori_loop(0, n_ch, body, jnp.zeros((QB, LANES), F32))
            return jnp.sum(cnt, axis=1, keepdims=True).astype(I32)

        lo = _bisect_threshold(count_ge, qrow + 1 > TOPK, QB)
        lo_b = jnp.broadcast_to(lo, (QB, ATT_CH))

        m_ref[...] = jnp.full(m_ref.shape, NEG, F32)
        l_ref[...] = jnp.zeros_like(l_ref)
        acc_ref[...] = jnp.zeros_like(acc_ref)

        def attn_chunk(c, carry):
            start = pl.multiple_of(c * ATT_CH, ATT_CH)
            bias = jnp.where(s_ref[c] >= lo_b, 0.0, NEG)
            a = lax.dot_general(qbd_ref[...], kb_ref[pl.ds(start, ATT_CH), :], NT_DIMS,
                                preferred_element_type=F32)
            a3 = a.reshape(N_HEADS, QB, ATT_CH) + bias[None]
            _softmax_step(a3, m_ref, l_ref, acc_ref, vb_ref[pl.ds(start, ATT_CH), :], QB)
            return carry

        lax.fori_loop(0, n_ch, attn_chunk, 0)

        o = acc_ref[...] / l_ref[...]
        for j, slab in enumerate(_gather_heads(o, QB)):
            o_ref[:, j * LANES:(j + 1) * LANES] = slab.astype(BF16)


def _attn_prompt(qi, wi, q, kib, kb, vb):
    row = lambda i: (i, 0)
    full = lambda i: (0, 0)
    nrow = N_HEADS * QB
    return pl.pallas_call(
        _attn_prompt_kernel,
        grid=(R // QB,),
        in_specs=[
            pl.BlockSpec((QB, N_IDX_HEADS * IDX_DIM), row),
            pl.BlockSpec((QB, LANES), row),
            pl.BlockSpec((QB, D_ATTN), row),
            pl.BlockSpec((R, LANES), full),
            pl.BlockSpec((R, D_KV), full),
            pl.BlockSpec((R, D_KV), full),
        ],
        out_specs=pl.BlockSpec((QB, D_ATTN), row),
        out_shape=jax.ShapeDtypeStruct((R, D_ATTN), BF16),
        scratch_shapes=[
            pltpu.VMEM((N_ATT_CH, QB, ATT_CH), I32),
            pltpu.VMEM((nrow, LANES), BF16),
            pltpu.VMEM((N_IDX_HEADS, QB, LANES), F32),
            pltpu.VMEM((nrow, D_KV), BF16),
            pltpu.VMEM((nrow, 1), F32),
            pltpu.VMEM((nrow, 1), F32),
            pltpu.VMEM((nrow, D_KV), F32),
        ],
        compiler_params=_cparams(("arbitrary",)),
        name="attn_prompt",
    )(qi, wi, q, kib, kb, vb)


S_CH = 2048
S_NCH = PAST_LEN // S_CH
S_PPC = S_CH // PAGE_SIZE
S_ROWS = N_HEADS * DEC_SEQ


def _kidx_page_copy(pt_ref, kidx_hbm, kbuf, sem, b, p, slot):
    return pltpu.make_async_copy(
        kidx_hbm.at[pt_ref[b, p]],
        kbuf.at[slot, pl.ds(pl.multiple_of(p * PAGE_SIZE, PAGE_SIZE), PAGE_SIZE)],
        sem.at[slot])


def _sample_score_kernel(pt_ref, qi_ref, wi_ref, kin_ref, kidx_hbm, bp_ref, bn_ref,
                         kbuf, sem, s_ref, sn_ref, knp_ref):
    b = pl.program_id(0)
    nb = pl.num_programs(0)
    slot = b % 2

    def start_batch(bb, sl):
        def body(p, c):
            _kidx_page_copy(pt_ref, kidx_hbm, kbuf, sem, bb, p, sl).start()
            return c
        lax.fori_loop(0, N_PAGES, body, 0)

    @pl.when(b == 0)
    def _():
        start_batch(0, 0)

    @pl.when(b + 1 < nb)
    def _():
        start_batch(b + 1, 1 - slot)

    def wait_body(p, c):
        _kidx_page_copy(pt_ref, kidx_hbm, kbuf, sem, b, p, slot).wait()
        return c
    lax.fori_loop(0, N_PAGES, wait_body, 0)

    qi = qi_ref[0]
    qis = jnp.concatenate(
        [qi[:, h * IDX_DIM:(h + 1) * IDX_DIM] for h in range(N_IDX_HEADS)], axis=0).astype(BF16)
    wi = wi_ref[0]
    wb = [jnp.broadcast_to(wi[:, h:h + 1], (DEC_SEQ, LANES)) for h in range(N_IDX_HEADS)]

    def head_sum(d, width):
        outs = []
        for t in range(width // LANES):
            acc = jnp.zeros((DEC_SEQ, LANES), F32)
            for h in range(N_IDX_HEADS):
                acc = acc + wb[h] * jnp.maximum(
                    d[h * DEC_SEQ:(h + 1) * DEC_SEQ, t * LANES:(t + 1) * LANES], 0.0)
            outs.append(acc)
        return outs

    def score_chunk(c, carry):
        kc = kbuf[slot, pl.ds(pl.multiple_of(c * S_CH, S_CH), S_CH), :].astype(BF16)
        d = lax.dot_general(qis, kc, NT_DIMS, preferred_element_type=F32)
        for t, acc in enumerate(head_sum(d, S_CH)):
            s_ref[c, :, t * LANES:(t + 1) * LANES] = _score_key(acc)
        return carry
    lax.fori_loop(0, S_NCH, score_chunk, 0)

    knp_ref[...] = jnp.zeros_like(knp_ref)
    knp_ref[:DEC_SEQ, :] = kin_ref[0][:, :IDX_DIM]
    d = lax.dot_general(qis, knp_ref[...].astype(BF16), NT_DIMS, preferred_element_type=F32)
    kpos = lax.broadcasted_iota(I32, (DEC_SEQ, LANES), 1)
    srow = lax.broadcasted_iota(I32, (DEC_SEQ, LANES), 0)
    sn_ref[...] = jnp.where(kpos <= srow, _score_key(head_sum(d, LANES)[0]), KEY_MASKED)

    def count_ge(mid):
        past = jnp.sum(jnp.where(s_ref[...] >= mid[None], 1.0, 0.0), axis=0)
        cnt = jnp.sum(past, axis=1, keepdims=True)
        cnt = cnt + jnp.sum(jnp.where(sn_ref[...] >= mid, 1.0, 0.0), axis=1, keepdims=True)
        return cnt.astype(I32)

    lo = _bisect_threshold(count_ge, jnp.ones((DEC_SEQ, 1), I32) > 0, DEC_SEQ)
    bp_ref[0] = jnp.where(s_ref[...] >= lo[None], 0.0, NEG)
    bn_ref[0] = jnp.where(sn_ref[...] >= lo, 0.0, NEG)


def _sample_scores(page_table, qi_s, wi_s, ki_s, cache_kidx):
    blk = lambda b, pt: (b, 0, 0)
    grid_spec = pltpu.PrefetchScalarGridSpec(
        num_scalar_prefetch=1,
        grid=(DEC_BATCH,),
        in_specs=[
            pl.BlockSpec((1, DEC_SEQ, N_IDX_HEADS * IDX_DIM), blk),
            pl.BlockSpec((1, DEC_SEQ, LANES), blk),
            pl.BlockSpec((1, DEC_SEQ, LANES), blk),
            pl.BlockSpec(memory_space=pl.ANY),
        ],
        out_specs=[
            pl.BlockSpec((1, S_NCH, DEC_SEQ, S_CH), lambda b, pt: (b, 0, 0, 0)),
            pl.BlockSpec((1, DEC_SEQ, LANES), blk),
        ],
        scratch_shapes=[
            pltpu.VMEM((2, PAST_LEN, IDX_DIM), F32),
            pltpu.SemaphoreType.DMA((2,)),
            pltpu.VMEM((S_NCH, DEC_SEQ, S_CH), I32),
            pltpu.VMEM((DEC_SEQ, LANES), I32),
            pltpu.VMEM((LANES, IDX_DIM), F32),
        ],
    )
    return pl.pallas_call(
        _sample_score_kernel,
        grid_spec=grid_spec,
        out_shape=(
            jax.ShapeDtypeStruct((DEC_BATCH, S_NCH, DEC_SEQ, S_CH), F32),
            jax.ShapeDtypeStruct((DEC_BATCH, DEC_SEQ, LANES), F32),
        ),
        compiler_params=_cparams(("arbitrary",)),
        name="sample_scores",
    )(page_table, qi_s, wi_s, ki_s, cache_kidx)


def _kv_page_copies(pt_ref, ck_hbm, cv_hbm, kbuf, vbuf, sem, b, c, p, slot):
    page = pt_ref[b, c * S_PPC + p]
    dst = pl.ds(pl.multiple_of(p * PAGE_SIZE, PAGE_SIZE), PAGE_SIZE)
    return (pltpu.make_async_copy(ck_hbm.at[page], kbuf.at[slot, dst], sem.at[0, slot]),
            pltpu.make_async_copy(cv_hbm.at[page], vbuf.at[slot, dst], sem.at[1, slot]))


def _sample_attn_kernel(pt_ref, q_ref, bp_ref, bn_ref, kn_ref, vn_ref, ck_hbm, cv_hbm, o_ref,
                        kbuf, vbuf, sem, qbd_ref, m_ref, l_ref, acc_ref, knp_ref, vnp_ref):
    b = pl.program_id(0)
    c = pl.program_id(1)
    step = b * S_NCH + c
    n_steps = pl.num_programs(0) * S_NCH
    slot = step % 2

    def start_chunk(st, sl):
        bb = st // S_NCH
        cc = st % S_NCH

        def body(p, carry):
            for cp in _kv_page_copies(pt_ref, ck_hbm, cv_hbm, kbuf, vbuf, sem, bb, cc, p, sl):
                cp.start()
            return carry
        lax.fori_loop(0, S_PPC, body, 0)

    @pl.when(step == 0)
    def _():
        start_chunk(0, 0)

    @pl.when(step + 1 < n_steps)
    def _():
        start_chunk(step + 1, 1 - slot)

    def wait_body(p, carry):
        for cp in _kv_page_copies(pt_ref, ck_hbm, cv_hbm, kbuf, vbuf, sem, b, c, p, slot):
            cp.wait()
        return carry
    lax.fori_loop(0, S_PPC, wait_body, 0)

    @pl.when(c == 0)
    def _():
        _build_qbd(q_ref.at[0], qbd_ref, DEC_SEQ)
        m_ref[...] = jnp.full(m_ref.shape, NEG, F32)
        l_ref[...] = jnp.zeros_like(l_ref)
        acc_ref[...] = jnp.zeros_like(acc_ref)

    def update(kc, vc, bias):
        a = lax.dot_general(qbd_ref[...].astype(BF16), kc, NT_DIMS, preferred_element_type=F32)
        a3 = a.reshape(N_HEADS, DEC_SEQ, a.shape[1]) + bias[None]
        _softmax_step(a3, m_ref, l_ref, acc_ref, vc, DEC_SEQ)

    update(kbuf[slot].astype(BF16), vbuf[slot].astype(BF16), bp_ref[0, 0])

    @pl.when(c == S_NCH - 1)
    def _():
        knp_ref[...] = jnp.zeros_like(knp_ref)
        vnp_ref[...] = jnp.zeros_like(vnp_ref)
        knp_ref[:DEC_SEQ, :] = kn_ref[0]
        vnp_ref[:DEC_SEQ, :] = vn_ref[0]
        update(knp_ref[...].astype(BF16), vnp_ref[...].astype(BF16), bn_ref[0])
        o = acc_ref[...] / l_ref[...]
        for j, slab in enumerate(_gather_heads(o, DEC_SEQ)):
            o_ref[0, :, j * LANES:(j + 1) * LANES] = slab


def _sample_attn(page_table, q_s, bias_past, bias_new, k_s, v_s, cache_k, cache_v):
    blk = lambda b, c, pt: (b, 0, 0)
    grid_spec = pltpu.PrefetchScalarGridSpec(
        num_scalar_prefetch=1,
        grid=(DEC_BATCH, S_NCH),
        in_specs=[
            pl.BlockSpec((1, DEC_SEQ, D_ATTN), blk),
            pl.BlockSpec((1, 1, DEC_SEQ, S_CH), lambda b, c, pt: (b, c, 0, 0)),
            pl.BlockSpec((1, DEC_SEQ, LANES), blk),
            pl.BlockSpec((1, DEC_SEQ, D_KV), blk),
            pl.BlockSpec((1, DEC_SEQ, D_KV), blk),
            pl.BlockSpec(memory_space=pl.ANY),
            pl.BlockSpec(memory_space=pl.ANY),
        ],
        out_specs=pl.BlockSpec((1, DEC_SEQ, D_ATTN), blk),
        scratch_shapes=[
            pltpu.VMEM((2, S_CH, D_KV), F32),
            pltpu.VMEM((2, S_CH, D_KV), F32),
            pltpu.SemaphoreType.DMA((2, 2)),
            pltpu.VMEM((S_ROWS, D_KV), F32),
            pltpu.VMEM((S_ROWS, 1), F32),
            pltpu.VMEM((S_ROWS, 1), F32),
            pltpu.VMEM((S_ROWS, D_KV), F32),
            pltpu.VMEM((LANES, D_KV), F32),
            pltpu.VMEM((LANES, D_KV), F32),
        ],
    )
    return pl.pallas_call(
        _sample_attn_kernel,
        grid_spec=grid_spec,
        out_shape=jax.ShapeDtypeStruct((DEC_BATCH, DEC_SEQ, D_ATTN), F32),
        compiler_params=_cparams(("arbitrary", "arbitrary")),
        name="sample_attn",
    )(page_table, q_s, bias_past, bias_new, k_s, v_s, cache_k, cache_v)


def _rope_tables():
    pos = np.zeros((R,), np.float32)
    pos[:T] = np.arange(T)
    pos[ROW_S:ROW_S + NS] = np.tile(PAST_LEN + np.arange(DEC_SEQ), DEC_BATCH)
    half = HEAD_DIM // 2
    inv_freq = ROPE_THETA ** (-(jnp.arange(half, dtype=F32) * 2.0 / HEAD_DIM))
    ang = jnp.asarray(pos)[:, None] * inv_freq
    cos = jnp.tile(jnp.cos(ang), (1, LANES // half))
    sin = jnp.sin(ang)
    sin = jnp.tile(jnp.concatenate([-sin, sin], axis=1), (1, LANES // HEAD_DIM))
    return cos, sin


def kernel(x_prompt, x_sample, cache_k, cache_v, cache_kidx, state_pool, page_table, meta_tokens,
           norm_mix_pre, w_in, idx_k_norm, w_pool, pool_scale, w_attn_o, w_out, norm_mix_post,
           norm_ffn_pre, w_up, w_down, norm_ffn_post):
    x_all = jnp.concatenate([
        meta_tokens, x_prompt[0], jnp.zeros((ROW_S - T, D_MODEL), F32),
        x_sample.reshape(NS, D_MODEL), jnp.zeros((R - ROW_S - NS, D_MODEL), F32)], axis=0)

    w = w_in[0]
    o_u, o_q, o_k, o_v, o_qi, o_ki, o_wi, o_gp = np.cumsum((0, D_POOL, D_ATTN, D_KV, D_KV,
                                                            N_IDX_HEADS * IDX_DIM, IDX_DIM, N_IDX_HEADS))
    n_kiwi = IDX_DIM + N_IDX_HEADS
    w_a = jnp.concatenate([
        w[:, o_q:o_v], w[:, o_qi:o_gp],
        jnp.zeros((D_MODEL, W_A_COLS - D_ATTN - D_KV - N_IDX_HEADS * IDX_DIM - n_kiwi), F32)],
        axis=1).astype(BF16)
    w_b = jnp.concatenate([w[:, o_v:o_qi], w[:, o_u:o_q]], axis=1).astype(BF16)
    w_c = w[:, o_gp:].astype(BF16)

    g_pre = norm_mix_pre[0][None]
    cos, sin = _rope_tables()
    gk = jnp.tile(idx_k_norm[0], LANES // IDX_DIM)[None]

    q, k, kb, qi, ki, kib, wi = _proj_a(x_all, g_pre, w_a, cos, sin, gk)
    v, vb, u = _proj_b(x_all, g_pre, w_b)
    gates = _proj_c(x_all, g_pre, w_c)

    def sample_rows(a):
        return a[ROW_S:ROW_S + NS].reshape(DEC_BATCH, DEC_SEQ, a.shape[1])

    pooled = _pool_prompt(u)
    u_s = sample_rows(u)
    state = state_pool[0]
    state16 = jnp.pad(state, ((0, 0), (HALO - POOL_BUF, 0), (0, 0)))
    pooled_s = _pool_sample(u_s, state16)
    pooled = lax.dynamic_update_slice(pooled, pooled_s.reshape(NS, D_POOL), (ROW_S, 0))

    attn = _attn_prompt(qi, wi, q, kib, kb, vb)
    k_s, v_s = sample_rows(k), sample_rows(v)
    bias_past, bias_new = _sample_scores(page_table, sample_rows(qi), sample_rows(wi), sample_rows(ki),
                                         cache_kidx[0])
    n_pool = cache_k.shape[1]
    attn_s = _sample_attn(page_table, sample_rows(q), bias_past, bias_new, k_s, v_s,
                          cache_k[0].reshape(n_pool, PAGE_SIZE, D_KV),
                          cache_v[0].reshape(n_pool, PAGE_SIZE, D_KV))
    attn = lax.dynamic_update_slice(attn, attn_s.reshape(NS, D_ATTN).astype(BF16), (ROW_S, 0))

    x1 = _merge(x_all, pooled, attn, gates, w_pool[0].astype(BF16), pool_scale[0][None],
                w_attn_o[0].astype(BF16), w_out[0].astype(BF16), norm_mix_post[0][None])
    y = _ffn(x1, norm_ffn_pre[0][None], w_up[0].astype(BF16), w_down[0].astype(BF16),
             norm_ffn_post[0][None])

    y_prompt = y[N_META:T][None]
    y_sample = y[ROW_S:ROW_S + NS].reshape(DEC_BATCH, DEC_SEQ, D_MODEL)
    kv_shape = (1, 1, T, N_KV_HEADS, HEAD_DIM)
    kv_s_shape = (1, DEC_BATCH, DEC_SEQ, N_KV_HEADS, HEAD_DIM)
    return (
        y_prompt, y_sample,
        k[:T].reshape(kv_shape), v[:T].reshape(kv_shape), ki[:T, :IDX_DIM][None, None],
        u[T - POOL_BUF:T][None, None],
        k_s.reshape(kv_s_shape), v_s.reshape(kv_s_shape), sample_rows(ki)[None, :, :, :IDX_DIM],
        jnp.concatenate([state[:, DEC_SEQ:], u_s], axis=1)[None],
    )
```

```python
import functools

import jax
import jax.numpy as jnp
import numpy as np
from jax import lax
from jax.experimental import pallas as pl
from jax.experimental.pallas import tpu as pltpu

F32 = jnp.float32
BF16 = jnp.bfloat16
I32 = jnp.int32

D_MODEL = 2048
SEQ = 8192
DEC_BATCH = 32
DEC_SEQ = 8
PAST_LEN = 16384
PAGE_SIZE = 128
N_PAGES = PAST_LEN // PAGE_SIZE
N_META = 16
N_HEADS = 16
N_KV_HEADS = 4
HEAD_DIM = 64
GROUP = N_HEADS // N_KV_HEADS
D_ATTN = N_HEADS * HEAD_DIM
D_KV = N_KV_HEADS * HEAD_DIM
ATTN_SCALE = HEAD_DIM ** -0.5
N_IDX_HEADS = 16
IDX_DIM = 64
INDEX_W_SCALE = (N_IDX_HEADS ** -0.5) * (IDX_DIM ** -0.5)
TOPK = 256
POOL_WINDOWS = (2, 4, 8, 16)
N_POOL_GROUPS = 4
D_POOL = D_MODEL // 2
POOL_GROUP = D_POOL // N_POOL_GROUPS
POOL_OUT_GROUP = D_MODEL // N_POOL_GROUPS
POOL_BUF = max(POOL_WINDOWS) - 1
D_FF = 4 * D_MODEL
ROPE_THETA = 10000.0
EPS = 1e-6

LANES = 128
T = SEQ + N_META
QB = 128
N_QBLK = -(-T // QB)
ROW_S = N_QBLK * QB
NS = DEC_BATCH * DEC_SEQ
R = 8704
VMEM_LIMIT = 56 * 1024 * 1024

NEG = -1e30
KEY_MASKED = -2 ** 31


def _cparams(sem):
    return pltpu.CompilerParams(dimension_semantics=sem, vmem_limit_bytes=VMEM_LIMIT)


def _rms(x, g):
    return x * lax.rsqrt(jnp.mean(x * x, axis=-1, keepdims=True) + EPS) * g


def _swap_halves(x):
    lane = lax.broadcasted_iota(I32, x.shape, 1)
    return jnp.where(lane % HEAD_DIM < HEAD_DIM // 2,
                     pltpu.roll(x, LANES - HEAD_DIM // 2, 1),
                     pltpu.roll(x, HEAD_DIM // 2, 1))


def _rope_cols(x, cos, sin):
    outs = []
    for c in range(x.shape[1] // LANES):
        xc = x[:, c * LANES:(c + 1) * LANES]
        outs.append(xc * cos + _swap_halves(xc) * sin)
    return outs


PROJ_TM = 544
N_ROPE_A = D_ATTN + D_KV
PROJ_TN_A = N_ROPE_A
W_A_COLS = 2 * PROJ_TN_A
W_B_COLS = D_KV + D_POOL
W_C_COLS = 2 * D_MODEL
PROJ_TN_C = 1024


def _norm_to_scratch(x_ref, g_ref, xn_ref):
    @pl.when(pl.program_id(1) == 0)
    def _():
        xn_ref[...] = _rms(x_ref[...], g_ref[...]).astype(BF16)


def _proj_a_kernel(x_ref, g_ref, w_ref, cos_ref, sin_ref, gk_ref,
                   q_ref, k_ref, kb_ref, qi_ref, ki_ref, kib_ref, wi_ref, xn_ref):
    _norm_to_scratch(x_ref, g_ref, xn_ref)
    j = pl.program_id(1)
    p = jnp.dot(xn_ref[...], w_ref[...], preferred_element_type=F32)
    cos = cos_ref[...]
    sin = sin_ref[...]

    @pl.when(j == 0)
    def _():
        cols = _rope_cols(p, cos, sin)
        nq = D_ATTN // LANES
        for c in range(nq):
            q_ref[:, c * LANES:(c + 1) * LANES] = cols[c]
        for c in range(D_KV // LANES):
            k_ref[:, c * LANES:(c + 1) * LANES] = cols[nq + c]
            kb_ref[:, c * LANES:(c + 1) * LANES] = cols[nq + c].astype(BF16)

    @pl.when(j == 1)
    def _():
        nqi = N_IDX_HEADS * IDX_DIM // LANES
        cols = _rope_cols(p[:, :nqi * LANES], cos, sin)
        for c in range(nqi):
            qi_ref[:, c * LANES:(c + 1) * LANES] = cols[c]
        slab = p[:, nqi * LANES:(nqi + 1) * LANES]
        lane = lax.broadcasted_iota(I32, slab.shape, 1)
        is_ki = lane < IDX_DIM
        ms = jnp.sum(jnp.where(is_ki, slab * slab, 0.0), axis=-1, keepdims=True) / IDX_DIM
        kin = slab * lax.rsqrt(ms + EPS) * gk_ref[...]
        kin = kin * cos + _swap_halves(kin) * sin
        ki2 = jnp.where(is_ki, kin, pltpu.roll(kin, IDX_DIM, 1))
        ki_ref[...] = ki2
        kib_ref[...] = ki2.astype(BF16)
        wi_ref[...] = pltpu.roll(slab, LANES - IDX_DIM, 1) * INDEX_W_SCALE


def _proj_a(x_all, g, w_a, cos, sin, gk):
    n_i = R // PROJ_TM
    row = lambda i, j: (i, 0)
    outs = (
        jax.ShapeDtypeStruct((R, D_ATTN), F32),
        jax.ShapeDtypeStruct((R, D_KV), F32),
        jax.ShapeDtypeStruct((R, D_KV), BF16),
        jax.ShapeDtypeStruct((R, N_IDX_HEADS * IDX_DIM), F32),
        jax.ShapeDtypeStruct((R, LANES), F32),
        jax.ShapeDtypeStruct((R, LANES), BF16),
        jax.ShapeDtypeStruct((R, LANES), F32),
    )
    return pl.pallas_call(
        _proj_a_kernel,
        grid=(n_i, 2),
        in_specs=[
            pl.BlockSpec((PROJ_TM, D_MODEL), row),
            pl.BlockSpec((1, D_MODEL), lambda i, j: (0, 0)),
            pl.BlockSpec((D_MODEL, PROJ_TN_A), lambda i, j: (0, j)),
            pl.BlockSpec((PROJ_TM, LANES), row),
            pl.BlockSpec((PROJ_TM, LANES), row),
            pl.BlockSpec((1, LANES), lambda i, j: (0, 0)),
        ],
        out_specs=[pl.BlockSpec((PROJ_TM, o.shape[1]), row) for o in outs],
        out_shape=outs,
        scratch_shapes=[pltpu.VMEM((PROJ_TM, D_MODEL), BF16)],
        compiler_params=_cparams(("arbitrary", "arbitrary")),
        name="proj_rope",
    )(x_all, g, w_a, cos, sin, gk)


def _proj_b_kernel(x_ref, g_ref, w_ref, v_ref, vb_ref, u_ref, xn_ref):
    _norm_to_scratch(x_ref, g_ref, xn_ref)
    p = jnp.dot(xn_ref[...], w_ref[...], preferred_element_type=F32)
    v_ref[...] = p[:, :D_KV]
    vb_ref[...] = p[:, :D_KV].astype(BF16)
    u_ref[...] = p[:, D_KV:]


def _proj_b(x_all, g, w_b):
    n_i = R // PROJ_TM
    row = lambda i, j: (i, 0)
    outs = (
        jax.ShapeDtypeStruct((R, D_KV), F32),
        jax.ShapeDtypeStruct((R, D_KV), BF16),
        jax.ShapeDtypeStruct((R, D_POOL), F32),
    )
    return pl.pallas_call(
        _proj_b_kernel,
        grid=(n_i, 1),
        in_specs=[
            pl.BlockSpec((PROJ_TM, D_MODEL), row),
            pl.BlockSpec((1, D_MODEL), lambda i, j: (0, 0)),
            pl.BlockSpec((D_MODEL, W_B_COLS), lambda i, j: (0, 0)),
        ],
        out_specs=[pl.BlockSpec((PROJ_TM, o.shape[1]), row) for o in outs],
        out_shape=outs,
        scratch_shapes=[pltpu.VMEM((PROJ_TM, D_MODEL), BF16)],
        compiler_params=_cparams(("arbitrary", "arbitrary")),
        name="proj_vu",
    )(x_all, g, w_b)


def _proj_c_kernel(x_ref, g_ref, w_ref, o_ref, xn_ref):
    _norm_to_scratch(x_ref, g_ref, xn_ref)
    o_ref[...] = jnp.dot(xn_ref[...], w_ref[...], preferred_element_type=F32)


def _proj_c(x_all, g, w_c):
    n_i = R // PROJ_TM
    return pl.pallas_call(
        _proj_c_kernel,
        grid=(n_i, W_C_COLS // PROJ_TN_C),
        in_specs=[
            pl.BlockSpec((PROJ_TM, D_MODEL), lambda i, j: (i, 0)),
            pl.BlockSpec((1, D_MODEL), lambda i, j: (0, 0)),
            pl.BlockSpec((D_MODEL, PROJ_TN_C), lambda i, j: (0, j)),
        ],
        out_specs=pl.BlockSpec((PROJ_TM, PROJ_TN_C), lambda i, j: (i, j)),
        out_shape=jax.ShapeDtypeStruct((R, W_C_COLS), F32),
        scratch_shapes=[pltpu.VMEM((PROJ_TM, D_MODEL), BF16)],
        compiler_params=_cparams(("arbitrary", "arbitrary")),
        name="proj_gates",
    )(x_all, g, w_c)


POOL_TM = 512
HALO = 16


def _window_mean_minus_cur(ext_ref, rows, inv_cnt):
    outs = []
    for g, w in enumerate(POOL_WINDOWS):
        cols = slice(g * POOL_GROUP, (g + 1) * POOL_GROUP)
        cur = ext_ref[HALO:HALO + rows, cols]
        acc = cur
        for d in range(1, w):
            acc = acc + ext_ref[HALO - d:HALO - d + rows, cols]
        outs.append(acc * inv_cnt[g] - cur)
    return outs


def _pool_prompt_kernel(u_ref, halo_ref, o_ref, ext_ref):
    i = pl.program_id(0)
    ext_ref[HALO:, :] = u_ref[...]
    ext_ref[:HALO, :] = jnp.where(i == 0, 0.0, halo_ref[...])
    pos = i * POOL_TM + lax.broadcasted_iota(I32, (POOL_TM, 1), 0)
    inv_cnt = [1.0 / jnp.minimum(pos + 1, w).astype(F32) for w in POOL_WINDOWS]
    outs = _window_mean_minus_cur(ext_ref, POOL_TM, inv_cnt)
    for g in range(N_POOL_GROUPS):
        o_ref[:, g * POOL_GROUP:(g + 1) * POOL_GROUP] = outs[g]


def _pool_prompt(u):
    per = POOL_TM // HALO
    return pl.pallas_call(
        _pool_prompt_kernel,
        grid=(R // POOL_TM,),
        in_specs=[
            pl.BlockSpec((POOL_TM, D_POOL), lambda i: (i, 0)),
            pl.BlockSpec((HALO, D_POOL), lambda i: (jnp.maximum(i * per - 1, 0), 0)),
        ],
        out_specs=pl.BlockSpec((POOL_TM, D_POOL), lambda i: (i, 0)),
        out_shape=jax.ShapeDtypeStruct((R, D_POOL), F32),
        scratch_shapes=[pltpu.VMEM((POOL_TM + HALO, D_POOL), F32)],
        compiler_params=_cparams(("arbitrary",)),
        name="pool_prompt",
    )(u, u)


def _pool_sample_kernel(u_ref, st_ref, o_ref, ext_ref):
    for b in range(DEC_BATCH):
        ext_ref[:HALO, :] = st_ref[b]
        ext_ref[HALO:, :] = u_ref[b]
        inv_cnt = [1.0 / w for w in POOL_WINDOWS]
        outs = _window_mean_minus_cur(ext_ref, DEC_SEQ, inv_cnt)
        for g in range(N_POOL_GROUPS):
            o_ref[b, :, g * POOL_GROUP:(g + 1) * POOL_GROUP] = outs[g]


def _pool_sample(u_s, state16):
    return pl.pallas_call(
        _pool_sample_kernel,
        out_shape=jax.ShapeDtypeStruct((DEC_BATCH, DEC_SEQ, D_POOL), F32),
        scratch_shapes=[pltpu.VMEM((HALO + DEC_SEQ, D_POOL), F32)],
        compiler_params=pltpu.CompilerParams(vmem_limit_bytes=VMEM_LIMIT),
        name="pool_sample",
    )(u_s, state16)


MERGE_TM = 256


def _merge_kernel(x_ref, pooled_ref, attn_ref, gate_ref, wp_ref, ps_ref, wa_ref, wo_ref, gn_ref, o_ref):
    pooled = pooled_ref[...].astype(BF16)
    pool_out = jnp.concatenate(
        [jnp.dot(pooled[:, g * POOL_GROUP:(g + 1) * POOL_GROUP], wp_ref[g], preferred_element_type=F32)
         for g in range(N_POOL_GROUPS)], axis=1) * ps_ref[...]
    attn_out = jnp.dot(attn_ref[...], wa_ref[...], preferred_element_type=F32)
    gate = gate_ref[...]
    m = (jax.nn.sigmoid(gate[:, :D_MODEL]) * pool_out
         + jax.nn.sigmoid(gate[:, D_MODEL:]) * attn_out)
    mix = jnp.dot(m.astype(BF16), wo_ref[...], preferred_element_type=F32)
    o_ref[...] = x_ref[...] + _rms(mix, gn_ref[...])


def _merge(x_all, pooled, attn, gates, w_pool, pool_scale, w_attn_o, w_out, g_post):
    row = lambda i: (i, 0)
    const2 = lambda i: (0, 0)
    return pl.pallas_call(
        _merge_kernel,
        grid=(R // MERGE_TM,),
        in_specs=[
            pl.BlockSpec((MERGE_TM, D_MODEL), row),
            pl.BlockSpec((MERGE_TM, D_POOL), row),
            pl.BlockSpec((MERGE_TM, D_ATTN), row),
            pl.BlockSpec((MERGE_TM, 2 * D_MODEL), row),
            pl.BlockSpec((N_POOL_GROUPS, POOL_GROUP, POOL_OUT_GROUP), lambda i: (0, 0, 0)),
            pl.BlockSpec((1, D_MODEL), const2),
            pl.BlockSpec((D_ATTN, D_MODEL), const2),
            pl.BlockSpec((D_MODEL, D_MODEL), const2),
            pl.BlockSpec((1, D_MODEL), const2),
        ],
        out_specs=pl.BlockSpec((MERGE_TM, D_MODEL), row),
        out_shape=jax.ShapeDtypeStruct((R, D_MODEL), F32),
        compiler_params=_cparams(("arbitrary",)),
        name="merge",
    )(x_all, pooled, attn, gates, w_pool, pool_scale, w_attn_o, w_out, g_post)


FFN_TM = 544
FFN_TF = 512


def _ffn_kernel(x_ref, gpre_ref, wu_ref, wd_ref, gpost_ref, o_ref, h_ref, acc_ref):
    j = pl.program_id(1)

    @pl.when(j == 0)
    def _():
        h_ref[...] = _rms(x_ref[...], gpre_ref[...]).astype(BF16)
        acc_ref[...] = jnp.zeros_like(acc_ref)

    a = jnp.maximum(jnp.dot(h_ref[...], wu_ref[...], preferred_element_type=F32), 0.0)
    acc_ref[...] += jnp.dot((a * a).astype(BF16), wd_ref[...], preferred_element_type=F32)

    @pl.when(j == pl.num_programs(1) - 1)
    def _():
        o_ref[...] = x_ref[...] + _rms(acc_ref[...], gpost_ref[...])


def _ffn(x1, g_pre, w_up, w_down, g_post):
    return pl.pallas_call(
        _ffn_kernel,
        grid=(R // FFN_TM, D_FF // FFN_TF),
        in_specs=[
            pl.BlockSpec((FFN_TM, D_MODEL), lambda i, j: (i, 0)),
            pl.BlockSpec((1, D_MODEL), lambda i, j: (0, 0)),
            pl.BlockSpec((D_MODEL, FFN_TF), lambda i, j: (0, j)),
            pl.BlockSpec((FFN_TF, D_MODEL), lambda i, j: (j, 0)),
            pl.BlockSpec((1, D_MODEL), lambda i, j: (0, 0)),
        ],
        out_specs=pl.BlockSpec((FFN_TM, D_MODEL), lambda i, j: (i, 0)),
        out_shape=jax.ShapeDtypeStruct((R, D_MODEL), F32),
        scratch_shapes=[pltpu.VMEM((FFN_TM, D_MODEL), BF16), pltpu.VMEM((FFN_TM, D_MODEL), F32)],
        compiler_params=_cparams(("arbitrary", "arbitrary")),
        name="ffn",
    )(x1, g_pre, w_up, w_down, g_post)


INT_MAX = 2 ** 31 - 1
MAX_BISECT = 34


def _score_key(s):
    bits = pltpu.bitcast(s, I32)
    return jnp.where(bits < 0, bits ^ INT_MAX, bits)


def _floor_avg(lo, hi):
    return (lo >> 1) + (hi >> 1) + (lo & hi & 1)


def _bisect_threshold(count_ge, active0, rows):
    lo0 = jnp.full((rows, 1), KEY_MASKED + 1, I32)
    hi0 = jnp.full((rows, 1), INT_MAX, I32)

    def cond(st):
        it, _, _, active = st
        return jnp.logical_and(it < MAX_BISECT, jnp.max(active) > 0)

    def body(st):
        it, lo, hi, active = st
        mid = _floor_avg(lo, hi)
        cnt = count_ge(mid)
        on = active > 0
        ge = cnt >= TOPK
        lo = jnp.where(jnp.logical_and(on, ge), mid, lo)
        hi = jnp.where(jnp.logical_and(on, jnp.logical_not(ge)), mid, hi)
        still = jnp.logical_and(cnt != TOPK, hi - lo > 1)
        return it + 1, lo, hi, jnp.where(jnp.logical_and(on, still), 1, 0).astype(I32)

    _, lo, _, _ = lax.while_loop(cond, body, (jnp.int32(0), lo0, hi0, active0.astype(I32)))
    return lo


def _head_slab(x_ref, h):
    j = h // 2
    return x_ref[:, j * LANES:(j + 1) * LANES]


def _place_half(slab, src_half, dst_half):
    lane = lax.broadcasted_iota(I32, slab.shape, 1)
    x = slab if src_half == dst_half else pltpu.roll(slab, HEAD_DIM, 1)
    return jnp.where((lane >= HEAD_DIM) == (dst_half == 1), x, 0.0)


def _gather_heads(o, rows):
    outs = []
    for j in range(N_HEADS // 2):
        n = (2 * j) // GROUP
        tc, th = n // 2, n % 2
        a0 = o[(2 * j) * rows:(2 * j + 1) * rows, tc * LANES:(tc + 1) * LANES]
        a1 = o[(2 * j + 1) * rows:(2 * j + 2) * rows, tc * LANES:(tc + 1) * LANES]
        x0 = a0 if th == 0 else pltpu.roll(a0, HEAD_DIM, 1)
        x1 = a1 if th == 1 else pltpu.roll(a1, HEAD_DIM, 1)
        lane = lax.broadcasted_iota(I32, x0.shape, 1)
        outs.append(jnp.where(lane < HEAD_DIM, x0, x1))
    return outs


def _build_qbd(q_ref, qbd_ref, rows):
    for h in range(N_HEADS):
        n = h // GROUP
        tc, th = n // 2, n % 2
        placed = _place_half(_head_slab(q_ref, h) * ATTN_SCALE, h % 2, th).astype(qbd_ref.dtype)
        for c in range(D_KV // LANES):
            qbd_ref[h * rows:(h + 1) * rows, c * LANES:(c + 1) * LANES] = (
                placed if c == tc else jnp.zeros_like(placed))


def _softmax_step(a3, m_ref, l_ref, acc_ref, vc, rows, v_transposed):
    n = N_HEADS * rows
    ch = a3.shape[2]
    m_prev = m_ref[...].reshape(N_HEADS, rows, 1)
    m_new = jnp.maximum(m_prev, jnp.max(a3, axis=2, keepdims=True))
    p = jnp.exp(a3 - m_new)
    alpha = jnp.exp(m_prev - m_new)
    l_ref[...] = (alpha * l_ref[...].reshape(N_HEADS, rows, 1)
                  + jnp.sum(p, axis=2, keepdims=True)).reshape(n, 1)
    m_ref[...] = m_new.reshape(n, 1)
    pb = p.reshape(n, ch).astype(BF16)
    if v_transposed:
        pv = lax.dot_general(pb, vc, NT_DIMS, preferred_element_type=F32)
    else:
        pv = jnp.dot(pb, vc, preferred_element_type=F32)
    acc_ref[...] = alpha.reshape(n, 1) * acc_ref[...] + pv


ATT_CH = 256
N_ATT_CH = R // ATT_CH
NT_DIMS = (((1,), (1,)), ((), ()))


def _attn_prompt_kernel(qi_ref, wi_ref, q_ref, kib_ref, kb_ref, vb_ref, o_ref,
                        s_ref, qim_ref, wb_ref, qbd_ref, m_ref, l_ref, acc_ref, p_ref):
    i = pl.program_id(0)

    @pl.when(i >= N_QBLK)
    def _():
        o_ref[...] = jnp.zeros_like(o_ref)

    @pl.when(i < N_QBLK)
    def _():
        n_ch = (i * QB) // ATT_CH + 1
        qrow = i * QB + lax.broadcasted_iota(I32, (QB, 1), 0)

        wi = wi_ref[...]
        for h in range(N_IDX_HEADS):
            qim_ref[h * QB:(h + 1) * QB, :] = _place_half(_head_slab(qi_ref, h), h % 2, h % 2).astype(BF16)
            wb_ref[h] = jnp.broadcast_to(wi[:, h:h + 1], (QB, LANES))
        _build_qbd(q_ref, qbd_ref, QB)

        def score_chunk(c, carry):
            start = pl.multiple_of(c * ATT_CH, ATT_CH)
            kc = kib_ref[pl.ds(start, ATT_CH), :]
            acc = [jnp.zeros((QB, LANES), F32) for _ in range(ATT_CH // LANES)]
            hpd = 4
            for hg in range(N_IDX_HEADS // hpd):
                d = lax.dot_general(qim_ref[hg * hpd * QB:(hg + 1) * hpd * QB, :], kc, NT_DIMS,
                                    preferred_element_type=F32)
                for hh in range(hpd):
                    w = wb_ref[hg * hpd + hh]
                    for t in range(ATT_CH // LANES):
                        acc[t] = acc[t] + w * jnp.maximum(
                            d[hh * QB:(hh + 1) * QB, t * LANES:(t + 1) * LANES], 0.0)
            for t in range(ATT_CH // LANES):
                kpos = start + t * LANES + lax.broadcasted_iota(I32, (QB, LANES), 1)
                s_ref[c, :, t * LANES:(t + 1) * LANES] = jnp.where(
                    kpos <= qrow, _score_key(acc[t]), KEY_MASKED)
            return carry

        lax.fori_loop(0, n_ch, score_chunk, 0)

        def count_ge(mid):
            mid_b = jnp.broadcast_to(mid, (QB, LANES))

            def body(c, cnt):
                keys = s_ref[c]
                for t in range(ATT_CH // LANES):
                    cnt = cnt + jnp.where(keys[:, t * LANES:(t + 1) * LANES] >= mid_b, 1.0, 0.0)
                return cnt

            cnt = lax.fori_loop(0, n_ch, body, jnp.zeros((QB, LANES), F32))
            return jnp.sum(cnt, axis=1, keepdims=True).astype(I32)

        lo = _bisect_threshold(count_ge, qrow + 1 > TOPK, QB)
        lo_b = jnp.broadcast_to(lo, (QB, ATT_CH))

        m_ref[...] = jnp.full(m_ref.shape, NEG, F32)
        l_ref[...] = jnp.zeros_like(l_ref)
        acc_ref[...] = jnp.zeros_like(acc_ref)
        n_t = ATT_CH // LANES

        def masked_logits(c):
            start = pl.multiple_of(c * ATT_CH, ATT_CH)
            bias = jnp.where(s_ref[c] >= lo_b, 0.0, NEG)
            a = lax.dot_general(qbd_ref[...], kb_ref[pl.ds(start, ATT_CH), :], NT_DIMS,
                                preferred_element_type=F32)
            return start, [[a[h * QB:(h + 1) * QB, t * LANES:(t + 1) * LANES]
                            + bias[:, t * LANES:(t + 1) * LANES] for t in range(n_t)]
                           for h in range(N_HEADS)]

        def max_chunk(c, carry):
            _, a = masked_logits(c)
            for h in range(N_HEADS):
                rows = slice(h * QB, (h + 1) * QB)
                m = m_ref[rows, :]
                for t in range(n_t):
                    m = jnp.maximum(m, a[h][t])
                m_ref[rows, :] = m
            return carry

        lax.fori_loop(0, n_ch, max_chunk, 0)
        m_ref[...] = jnp.broadcast_to(jnp.max(m_ref[...], axis=1, keepdims=True), m_ref.shape)

        def value_chunk(c, carry):
            start, a = masked_logits(c)
            for h in range(N_HEADS):
                rows = slice(h * QB, (h + 1) * QB)
                m = m_ref[rows, :]
                l = l_ref[rows, :]
                for t in range(n_t):
                    p = jnp.exp(a[h][t] - m)
                    l = l + p
                    p_ref[rows, t * LANES:(t + 1) * LANES] = p.astype(BF16)
                l_ref[rows, :] = l
            acc_ref[...] += jnp.dot(p_ref[...], vb_ref[pl.ds(start, ATT_CH), :],
                                    preferred_element_type=F32)
            return carry

        lax.fori_loop(0, n_ch, value_chunk, 0)

        o = acc_ref[...] / jnp.sum(l_ref[...], axis=1, keepdims=True)
        for j, slab in enumerate(_gather_heads(o, QB)):
            o_ref[:, j * LANES:(j + 1) * LANES] = slab.astype(BF16)


def _attn_prompt(qi, wi, q, kib, kb, vb):
    row = lambda i: (i, 0)
    full = lambda i: (0, 0)
    nrow = N_HEADS * QB
    return pl.pallas_call(
        _attn_prompt_kernel,
        grid=(R // QB,),
        in_specs=[
            pl.BlockSpec((QB, N_IDX_HEADS * IDX_DIM), row),
            pl.BlockSpec((QB, LANES), row),
            pl.BlockSpec((QB, D_ATTN), row),
            pl.BlockSpec((R, LANES), full),
            pl.BlockSpec((R, D_KV), full),
            pl.BlockSpec((R, D_KV), full),
        ],
        out_specs=pl.BlockSpec((QB, D_ATTN), row),
        out_shape=jax.ShapeDtypeStruct((R, D_ATTN), BF16),
        scratch_shapes=[
            pltpu.VMEM((N_ATT_CH, QB, ATT_CH), I32),
            pltpu.VMEM((nrow, LANES), BF16),
            pltpu.VMEM((N_IDX_HEADS, QB, LANES), F32),
            pltpu.VMEM((nrow, D_KV), BF16),
            pltpu.VMEM((nrow, LANES), F32),
            pltpu.VMEM((nrow, LANES), F32),
            pltpu.VMEM((nrow, D_KV), F32),
            pltpu.VMEM((nrow, ATT_CH), BF16),
        ],
        compiler_params=_cparams(("arbitrary",)),
        name="attn_prompt",
    )(qi, wi, q, kib, kb, vb)


S_CH = 2048
S_NCH = PAST_LEN // S_CH
S_PPC = S_CH // PAGE_SIZE
S_ROWS = N_HEADS * DEC_SEQ


def _page_lanes(p):
    return pl.ds(pl.multiple_of(p * PAGE_SIZE, PAGE_SIZE), PAGE_SIZE)


def _kidx_page_copy(pt_ref, kidx_hbm, kbuf, sem, b, p, slot):
    return pltpu.make_async_copy(
        kidx_hbm.at[pt_ref[b, p]], kbuf.at[slot, :, _page_lanes(p)], sem.at[slot])


def _sample_score_kernel(pt_ref, qi_ref, wi_ref, kin_ref, kidx_hbm, bp_ref, bn_ref,
                         kbuf, sem, s_ref, sn_ref, knp_ref):
    b = pl.program_id(0)
    nb = pl.num_programs(0)
    slot = b % 2

    def start_batch(bb, sl):
        def body(p, c):
            _kidx_page_copy(pt_ref, kidx_hbm, kbuf, sem, bb, p, sl).start()
            return c
        lax.fori_loop(0, N_PAGES, body, 0)

    @pl.when(b == 0)
    def _():
        start_batch(0, 0)

    @pl.when(b + 1 < nb)
    def _():
        start_batch(b + 1, 1 - slot)

    def wait_body(p, c):
        _kidx_page_copy(pt_ref, kidx_hbm, kbuf, sem, b, p, slot).wait()
        return c
    lax.fori_loop(0, N_PAGES, wait_body, 0)

    qi = qi_ref[0]
    qis = jnp.concatenate(
        [qi[:, h * IDX_DIM:(h + 1) * IDX_DIM] for h in range(N_IDX_HEADS)], axis=0).astype(BF16)
    wi = wi_ref[0]
    wb = [jnp.broadcast_to(wi[:, h:h + 1], (DEC_SEQ, LANES)) for h in range(N_IDX_HEADS)]

    def head_sum(d, width):
        outs = []
        for t in range(width // LANES):
            acc = jnp.zeros((DEC_SEQ, LANES), F32)
            for h in range(N_IDX_HEADS):
                acc = acc + wb[h] * jnp.maximum(
                    d[h * DEC_SEQ:(h + 1) * DEC_SEQ, t * LANES:(t + 1) * LANES], 0.0)
            outs.append(acc)
        return outs

    def score_chunk(c, carry):
        kc = kbuf[slot, :, pl.ds(pl.multiple_of(c * S_CH, S_CH), S_CH)].astype(BF16)
        d = jnp.dot(qis, kc, preferred_element_type=F32)
        for t, acc in enumerate(head_sum(d, S_CH)):
            s_ref[c, :, t * LANES:(t + 1) * LANES] = _score_key(acc)
        return carry
    lax.fori_loop(0, S_NCH, score_chunk, 0)

    knp_ref[...] = jnp.zeros_like(knp_ref)
    knp_ref[:DEC_SEQ, :] = kin_ref[0][:, :IDX_DIM]
    d = lax.dot_general(qis, knp_ref[...].astype(BF16), NT_DIMS, preferred_element_type=F32)
    kpos = lax.broadcasted_iota(I32, (DEC_SEQ, LANES), 1)
    srow = lax.broadcasted_iota(I32, (DEC_SEQ, LANES), 0)
    sn_ref[...] = jnp.where(kpos <= srow, _score_key(head_sum(d, LANES)[0]), KEY_MASKED)

    def count_ge(mid):
        past = jnp.sum(jnp.where(s_ref[...] >= mid[None], 1.0, 0.0), axis=0)
        cnt = jnp.sum(past, axis=1, keepdims=True)
        cnt = cnt + jnp.sum(jnp.where(sn_ref[...] >= mid, 1.0, 0.0), axis=1, keepdims=True)
        return cnt.astype(I32)

    lo = _bisect_threshold(count_ge, jnp.ones((DEC_SEQ, 1), I32) > 0, DEC_SEQ)
    bp_ref[0] = jnp.where(s_ref[...] >= lo[None], 0.0, NEG)
    bn_ref[0] = jnp.where(sn_ref[...] >= lo, 0.0, NEG)


def _sample_scores(page_table, qi_s, wi_s, ki_s, cache_kidx):
    blk = lambda b, pt: (b, 0, 0)
    grid_spec = pltpu.PrefetchScalarGridSpec(
        num_scalar_prefetch=1,
        grid=(DEC_BATCH,),
        in_specs=[
            pl.BlockSpec((1, DEC_SEQ, N_IDX_HEADS * IDX_DIM), blk),
            pl.BlockSpec((1, DEC_SEQ, LANES), blk),
            pl.BlockSpec((1, DEC_SEQ, LANES), blk),
            pl.BlockSpec(memory_space=pl.ANY),
        ],
        out_specs=[
            pl.BlockSpec((1, S_NCH, DEC_SEQ, S_CH), lambda b, pt: (b, 0, 0, 0)),
            pl.BlockSpec((1, DEC_SEQ, LANES), blk),
        ],
        scratch_shapes=[
            pltpu.VMEM((2, IDX_DIM, PAST_LEN), F32),
            pltpu.SemaphoreType.DMA((2,)),
            pltpu.VMEM((S_NCH, DEC_SEQ, S_CH), I32),
            pltpu.VMEM((DEC_SEQ, LANES), I32),
            pltpu.VMEM((LANES, IDX_DIM), F32),
        ],
    )
    return pl.pallas_call(
        _sample_score_kernel,
        grid_spec=grid_spec,
        out_shape=(
            jax.ShapeDtypeStruct((DEC_BATCH, S_NCH, DEC_SEQ, S_CH), F32),
            jax.ShapeDtypeStruct((DEC_BATCH, DEC_SEQ, LANES), F32),
        ),
        compiler_params=_cparams(("arbitrary",)),
        name="sample_scores",
    )(page_table, qi_s, wi_s, ki_s, cache_kidx)


def _kv_page_copies(pt_ref, ck_hbm, cv_hbm, kbuf, vbuf, sem, b, c, p, slot):
    page = pt_ref[b, c * S_PPC + p]
    dst = _page_lanes(p)
    return (pltpu.make_async_copy(ck_hbm.at[page], kbuf.at[slot, :, dst], sem.at[0, slot]),
            pltpu.make_async_copy(cv_hbm.at[page], vbuf.at[slot, :, dst], sem.at[1, slot]))


def _sample_attn_kernel(pt_ref, q_ref, bp_ref, bn_ref, kn_ref, vn_ref, ck_hbm, cv_hbm, o_ref,
                        kbuf, vbuf, sem, qbd_ref, m_ref, l_ref, acc_ref, knp_ref, vnp_ref):
    b = pl.program_id(0)
    c = pl.program_id(1)
    step = b * S_NCH + c
    n_steps = pl.num_programs(0) * S_NCH
    slot = step % 2

    def start_chunk(st, sl):
        bb = st // S_NCH
        cc = st % S_NCH

        def body(p, carry):
            for cp in _kv_page_copies(pt_ref, ck_hbm, cv_hbm, kbuf, vbuf, sem, bb, cc, p, sl):
                cp.start()
            return carry
        lax.fori_loop(0, S_PPC, body, 0)

    @pl.when(step == 0)
    def _():
        start_chunk(0, 0)

    @pl.when(step + 1 < n_steps)
    def _():
        start_chunk(step + 1, 1 - slot)

    def wait_body(p, carry):
        for cp in _kv_page_copies(pt_ref, ck_hbm, cv_hbm, kbuf, vbuf, sem, b, c, p, slot):
            cp.wait()
        return carry
    lax.fori_loop(0, S_PPC, wait_body, 0)

    @pl.when(c == 0)
    def _():
        _build_qbd(q_ref.at[0], qbd_ref, DEC_SEQ)
        m_ref[...] = jnp.full(m_ref.shape, NEG, F32)
        l_ref[...] = jnp.zeros_like(l_ref)
        acc_ref[...] = jnp.zeros_like(acc_ref)

    def update(kc, vc, bias, feature_major):
        qbd = qbd_ref[...].astype(BF16)
        if feature_major:
            a = jnp.dot(qbd, kc, preferred_element_type=F32)
        else:
            a = lax.dot_general(qbd, kc, NT_DIMS, preferred_element_type=F32)
        a3 = a.reshape(N_HEADS, DEC_SEQ, a.shape[1]) + bias[None]
        _softmax_step(a3, m_ref, l_ref, acc_ref, vc, DEC_SEQ, feature_major)

    update(kbuf[slot].astype(BF16), vbuf[slot].astype(BF16), bp_ref[0, 0], True)

    @pl.when(c == S_NCH - 1)
    def _():
        knp_ref[...] = jnp.zeros_like(knp_ref)
        vnp_ref[...] = jnp.zeros_like(vnp_ref)
        knp_ref[:DEC_SEQ, :] = kn_ref[0]
        vnp_ref[:DEC_SEQ, :] = vn_ref[0]
        update(knp_ref[...].astype(BF16), vnp_ref[...].astype(BF16), bn_ref[0], False)
        o = acc_ref[...] / l_ref[...]
        for j, slab in enumerate(_gather_heads(o, DEC_SEQ)):
            o_ref[0, :, j * LANES:(j + 1) * LANES] = slab


def _sample_attn(page_table, q_s, bias_past, bias_new, k_s, v_s, cache_k, cache_v):
    blk = lambda b, c, pt: (b, 0, 0)
    grid_spec = pltpu.PrefetchScalarGridSpec(
        num_scalar_prefetch=1,
        grid=(DEC_BATCH, S_NCH),
        in_specs=[
            pl.BlockSpec((1, DEC_SEQ, D_ATTN), blk),
            pl.BlockSpec((1, 1, DEC_SEQ, S_CH), lambda b, c, pt: (b, c, 0, 0)),
            pl.BlockSpec((1, DEC_SEQ, LANES), blk),
            pl.BlockSpec((1, DEC_SEQ, D_KV), blk),
            pl.BlockSpec((1, DEC_SEQ, D_KV), blk),
            pl.BlockSpec(memory_space=pl.ANY),
            pl.BlockSpec(memory_space=pl.ANY),
        ],
        out_specs=pl.BlockSpec((1, DEC_SEQ, D_ATTN), blk),
        scratch_shapes=[
            pltpu.VMEM((2, D_KV, S_CH), F32),
            pltpu.VMEM((2, D_KV, S_CH), F32),
            pltpu.SemaphoreType.DMA((2, 2)),
            pltpu.VMEM((S_ROWS, D_KV), F32),
            pltpu.VMEM((S_ROWS, 1), F32),
            pltpu.VMEM((S_ROWS, 1), F32),
            pltpu.VMEM((S_ROWS, D_KV), F32),
            pltpu.VMEM((LANES, D_KV), F32),
            pltpu.VMEM((LANES, D_KV), F32),
        ],
    )
    return pl.pallas_call(
        _sample_attn_kernel,
        grid_spec=grid_spec,
        out_shape=jax.ShapeDtypeStruct((DEC_BATCH, DEC_SEQ, D_ATTN), F32),
        compiler_params=_cparams(("arbitrary", "arbitrary")),
        name="sample_attn",
    )(page_table, q_s, bias_past, bias_new, k_s, v_s, cache_k, cache_v)


def _rope_tables():
    pos = np.zeros((R,), np.float32)
    pos[:T] = np.arange(T)
    pos[ROW_S:ROW_S + NS] = np.tile(PAST_LEN + np.arange(DEC_SEQ), DEC_BATCH)
    half = HEAD_DIM // 2
    inv_freq = ROPE_THETA ** (-(jnp.arange(half, dtype=F32) * 2.0 / HEAD_DIM))
    ang = jnp.asarray(pos)[:, None] * inv_freq
    cos = jnp.tile(jnp.cos(ang), (1, LANES // half))
    sin = jnp.sin(ang)
    sin = jnp.tile(jnp.concatenate([-sin, sin], axis=1), (1, LANES // HEAD_DIM))
    return cos, sin


def kernel(x_prompt, x_sample, cache_k, cache_v, cache_kidx, state_pool, page_table, meta_tokens,
           norm_mix_pre, w_in, idx_k_norm, w_pool, pool_scale, w_attn_o, w_out, norm_mix_post,
           norm_ffn_pre, w_up, w_down, norm_ffn_post):
    x_all = jnp.concatenate([
        meta_tokens, x_prompt[0], jnp.zeros((ROW_S - T, D_MODEL), F32),
        x_sample.reshape(NS, D_MODEL), jnp.zeros((R - ROW_S - NS, D_MODEL), F32)], axis=0)

    w = w_in[0]
    o_u, o_q, o_k, o_v, o_qi, o_ki, o_wi, o_gp = np.cumsum((0, D_POOL, D_ATTN, D_KV, D_KV,
                                                            N_IDX_HEADS * IDX_DIM, IDX_DIM, N_IDX_HEADS))
    n_kiwi = IDX_DIM + N_IDX_HEADS
    w_a = jnp.concatenate([
        w[:, o_q:o_v], w[:, o_qi:o_gp],
        jnp.zeros((D_MODEL, W_A_COLS - D_ATTN - D_KV - N_IDX_HEADS * IDX_DIM - n_kiwi), F32)],
        axis=1).astype(BF16)
    w_b = jnp.concatenate([w[:, o_v:o_qi], w[:, o_u:o_q]], axis=1).astype(BF16)
    w_c = w[:, o_gp:].astype(BF16)

    g_pre = norm_mix_pre[0][None]
    cos, sin = _rope_tables()
    gk = jnp.tile(idx_k_norm[0], LANES // IDX_DIM)[None]

    q, k, kb, qi, ki, kib, wi = _proj_a(x_all, g_pre, w_a, cos, sin, gk)
    v, vb, u = _proj_b(x_all, g_pre, w_b)
    gates = _proj_c(x_all, g_pre, w_c)

    def sample_rows(a):
        return a[ROW_S:ROW_S + NS].reshape(DEC_BATCH, DEC_SEQ, a.shape[1])

    pooled = _pool_prompt(u)
    u_s = sample_rows(u)
    state = state_pool[0]
    state16 = jnp.pad(state, ((0, 0), (HALO - POOL_BUF, 0), (0, 0)))
    pooled_s = _pool_sample(u_s, state16)
    pooled = lax.dynamic_update_slice(pooled, pooled_s.reshape(NS, D_POOL), (ROW_S, 0))

    attn = _attn_prompt(qi, wi, q, kib, kb, vb)
    k_s, v_s = sample_rows(k), sample_rows(v)
    n_pool = cache_k.shape[1]
    kidx_fm = jnp.transpose(cache_kidx[0], (0, 2, 1))
    k_fm = jnp.transpose(cache_k[0], (0, 2, 3, 1)).reshape(n_pool, D_KV, PAGE_SIZE)
    v_fm = jnp.transpose(cache_v[0], (0, 2, 3, 1)).reshape(n_pool, D_KV, PAGE_SIZE)
    bias_past, bias_new = _sample_scores(page_table, sample_rows(qi), sample_rows(wi), sample_rows(ki),
                                         kidx_fm)
    attn_s = _sample_attn(page_table, sample_rows(q), bias_past, bias_new, k_s, v_s, k_fm, v_fm)
    attn = lax.dynamic_update_slice(attn, attn_s.reshape(NS, D_ATTN).astype(BF16), (ROW_S, 0))

    x1 = _merge(x_all, pooled, attn, gates, w_pool[0].astype(BF16), pool_scale[0][None],
                w_attn_o[0].astype(BF16), w_out[0].astype(BF16), norm_mix_post[0][None])
    y = _ffn(x1, norm_ffn_pre[0][None], w_up[0].astype(BF16), w_down[0].astype(BF16),
             norm_ffn_post[0][None])

    y_prompt = y[N_META:T][None]
    y_sample = y[ROW_S:ROW_S + NS].reshape(DEC_BATCH, DEC_SEQ, D_MODEL)
    kv_shape = (1, 1, T, N_KV_HEADS, HEAD_DIM)
    kv_s_shape = (1, DEC_BATCH, DEC_SEQ, N_KV_HEADS, HEAD_DIM)
    return (
        y_prompt, y_sample,
        k[:T].reshape(kv_shape), v[:T].reshape(kv_shape), ki[:T, :IDX_DIM][None, None],
        u[T - POOL_BUF:T][None, None],
        k_s.reshape(kv_s_shape), v_s.reshape(kv_s_shape), sample_rows(ki)[None, :, :, :IDX_DIM],
        jnp.concatenate([state[:, DEC_SEQ:], u_s], axis=1)[None],
    )
```

```python
import functools

import jax
import jax.numpy as jnp
import numpy as np
from jax import lax
from jax.experimental import pallas as pl
from jax.experimental.pallas import tpu as pltpu

F32 = jnp.float32
BF16 = jnp.bfloat16
I32 = jnp.int32

D_MODEL = 2048
SEQ = 8192
DEC_BATCH = 32
DEC_SEQ = 8
PAST_LEN = 16384
PAGE_SIZE = 128
N_PAGES = PAST_LEN // PAGE_SIZE
N_META = 16
N_HEADS = 16
N_KV_HEADS = 4
HEAD_DIM = 64
GROUP = N_HEADS // N_KV_HEADS
D_ATTN = N_HEADS * HEAD_DIM
D_KV = N_KV_HEADS * HEAD_DIM
ATTN_SCALE = HEAD_DIM ** -0.5
N_IDX_HEADS = 16
IDX_DIM = 64
INDEX_W_SCALE = (N_IDX_HEADS ** -0.5) * (IDX_DIM ** -0.5)
TOPK = 256
POOL_WINDOWS = (2, 4, 8, 16)
N_POOL_GROUPS = 4
D_POOL = D_MODEL // 2
POOL_GROUP = D_POOL // N_POOL_GROUPS
POOL_OUT_GROUP = D_MODEL // N_POOL_GROUPS
POOL_BUF = max(POOL_WINDOWS) - 1
D_FF = 4 * D_MODEL
ROPE_THETA = 10000.0
EPS = 1e-6

LANES = 128
SUBLANES = 8
T = SEQ + N_META
QB = 128
N_QBLK = -(-T // QB)
ROW_S = N_QBLK * QB
NS = DEC_BATCH * DEC_SEQ
R = 8704
VMEM_LIMIT = 56 * 1024 * 1024

NEG = -1e30
KEY_MASKED = -2 ** 31


def _cparams(sem):
    return pltpu.CompilerParams(dimension_semantics=sem, vmem_limit_bytes=VMEM_LIMIT)


def _rms(x, g):
    return x * lax.rsqrt(jnp.mean(x * x, axis=-1, keepdims=True) + EPS) * g


def _swap_halves(x):
    lane = lax.broadcasted_iota(I32, x.shape, 1)
    return jnp.where(lane % HEAD_DIM < HEAD_DIM // 2,
                     pltpu.roll(x, LANES - HEAD_DIM // 2, 1),
                     pltpu.roll(x, HEAD_DIM // 2, 1))


def _rope_cols(x, cos, sin):
    outs = []
    for c in range(x.shape[1] // LANES):
        xc = x[:, c * LANES:(c + 1) * LANES]
        outs.append(xc * cos + _swap_halves(xc) * sin)
    return outs


PROJ_TM = 544
N_ROPE_A = D_ATTN + D_KV
PROJ_TN_A = N_ROPE_A
W_A_COLS = 2 * PROJ_TN_A
W_B_COLS = D_KV + D_POOL
W_C_COLS = 2 * D_MODEL
PROJ_TN_C = 1024


def _norm_to_scratch(x_ref, g_ref, xn_ref):
    @pl.when(pl.program_id(1) == 0)
    def _():
        xn_ref[...] = _rms(x_ref[...], g_ref[...]).astype(BF16)


def _proj_a_kernel(x_ref, g_ref, w_ref, cos_ref, sin_ref, gk_ref,
                   q_ref, k_ref, kb_ref, qi_ref, ki_ref, kib_ref, wi_ref, xn_ref):
    _norm_to_scratch(x_ref, g_ref, xn_ref)
    j = pl.program_id(1)
    p = jnp.dot(xn_ref[...], w_ref[...], preferred_element_type=F32)
    cos = cos_ref[...]
    sin = sin_ref[...]

    @pl.when(j == 0)
    def _():
        cols = _rope_cols(p, cos, sin)
        nq = D_ATTN // LANES
        for c in range(nq):
            q_ref[:, c * LANES:(c + 1) * LANES] = cols[c]
        for c in range(D_KV // LANES):
            k_ref[:, c * LANES:(c + 1) * LANES] = cols[nq + c]
            kb_ref[:, c * LANES:(c + 1) * LANES] = cols[nq + c].astype(BF16)

    @pl.when(j == 1)
    def _():
        nqi = N_IDX_HEADS * IDX_DIM // LANES
        cols = _rope_cols(p[:, :nqi * LANES], cos, sin)
        for c in range(nqi):
            qi_ref[:, c * LANES:(c + 1) * LANES] = cols[c]
        slab = p[:, nqi * LANES:(nqi + 1) * LANES]
        lane = lax.broadcasted_iota(I32, slab.shape, 1)
        is_ki = lane < IDX_DIM
        ms = jnp.sum(jnp.where(is_ki, slab * slab, 0.0), axis=-1, keepdims=True) / IDX_DIM
        kin = slab * lax.rsqrt(ms + EPS) * gk_ref[...]
        kin = kin * cos + _swap_halves(kin) * sin
        ki2 = jnp.where(is_ki, kin, pltpu.roll(kin, IDX_DIM, 1))
        ki_ref[...] = ki2
        kib_ref[...] = ki2.astype(BF16)
        wi_ref[...] = pltpu.roll(slab, LANES - IDX_DIM, 1) * INDEX_W_SCALE


def _proj_a(x_all, g, w_a, cos, sin, gk):
    n_i = R // PROJ_TM
    row = lambda i, j: (i, 0)
    outs = (
        jax.ShapeDtypeStruct((R, D_ATTN), F32),
        jax.ShapeDtypeStruct((R, D_KV), F32),
        jax.ShapeDtypeStruct((R, D_KV), BF16),
        jax.ShapeDtypeStruct((R, N_IDX_HEADS * IDX_DIM), F32),
        jax.ShapeDtypeStruct((R, LANES), F32),
        jax.ShapeDtypeStruct((R, LANES), BF16),
        jax.ShapeDtypeStruct((R, LANES), F32),
    )
    return pl.pallas_call(
        _proj_a_kernel,
        grid=(n_i, 2),
        in_specs=[
            pl.BlockSpec((PROJ_TM, D_MODEL), row),
            pl.BlockSpec((1, D_MODEL), lambda i, j: (0, 0)),
            pl.BlockSpec((D_MODEL, PROJ_TN_A), lambda i, j: (0, j)),
            pl.BlockSpec((PROJ_TM, LANES), row),
            pl.BlockSpec((PROJ_TM, LANES), row),
            pl.BlockSpec((1, LANES), lambda i, j: (0, 0)),
        ],
        out_specs=[pl.BlockSpec((PROJ_TM, o.shape[1]), row) for o in outs],
        out_shape=outs,
        scratch_shapes=[pltpu.VMEM((PROJ_TM, D_MODEL), BF16)],
        compiler_params=_cparams(("arbitrary", "arbitrary")),
        name="proj_rope",
    )(x_all, g, w_a, cos, sin, gk)


def _proj_b_kernel(x_ref, g_ref, w_ref, v_ref, vb_ref, u_ref, xn_ref):
    _norm_to_scratch(x_ref, g_ref, xn_ref)
    p = jnp.dot(xn_ref[...], w_ref[...], preferred_element_type=F32)
    v_ref[...] = p[:, :D_KV]
    vb_ref[...] = p[:, :D_KV].astype(BF16)
    u_ref[...] = p[:, D_KV:]


def _proj_b(x_all, g, w_b):
    n_i = R // PROJ_TM
    row = lambda i, j: (i, 0)
    outs = (
        jax.ShapeDtypeStruct((R, D_KV), F32),
        jax.ShapeDtypeStruct((R, D_KV), BF16),
        jax.ShapeDtypeStruct((R, D_POOL), F32),
    )
    return pl.pallas_call(
        _proj_b_kernel,
        grid=(n_i, 1),
        in_specs=[
            pl.BlockSpec((PROJ_TM, D_MODEL), row),
            pl.BlockSpec((1, D_MODEL), lambda i, j: (0, 0)),
            pl.BlockSpec((D_MODEL, W_B_COLS), lambda i, j: (0, 0)),
        ],
        out_specs=[pl.BlockSpec((PROJ_TM, o.shape[1]), row) for o in outs],
        out_shape=outs,
        scratch_shapes=[pltpu.VMEM((PROJ_TM, D_MODEL), BF16)],
        compiler_params=_cparams(("arbitrary", "arbitrary")),
        name="proj_vu",
    )(x_all, g, w_b)


def _proj_c_kernel(x_ref, g_ref, w_ref, o_ref, xn_ref):
    _norm_to_scratch(x_ref, g_ref, xn_ref)
    o_ref[...] = jnp.dot(xn_ref[...], w_ref[...], preferred_element_type=F32)


def _proj_c(x_all, g, w_c):
    n_i = R // PROJ_TM
    return pl.pallas_call(
        _proj_c_kernel,
        grid=(n_i, W_C_COLS // PROJ_TN_C),
        in_specs=[
            pl.BlockSpec((PROJ_TM, D_MODEL), lambda i, j: (i, 0)),
            pl.BlockSpec((1, D_MODEL), lambda i, j: (0, 0)),
            pl.BlockSpec((D_MODEL, PROJ_TN_C), lambda i, j: (0, j)),
        ],
        out_specs=pl.BlockSpec((PROJ_TM, PROJ_TN_C), lambda i, j: (i, j)),
        out_shape=jax.ShapeDtypeStruct((R, W_C_COLS), F32),
        scratch_shapes=[pltpu.VMEM((PROJ_TM, D_MODEL), BF16)],
        compiler_params=_cparams(("arbitrary", "arbitrary")),
        name="proj_gates",
    )(x_all, g, w_c)


POOL_TM = 512
HALO = 16


def _window_mean_minus_cur(ext_ref, rows, inv_cnt):
    outs = []
    for g, w in enumerate(POOL_WINDOWS):
        cols = slice(g * POOL_GROUP, (g + 1) * POOL_GROUP)
        cur = ext_ref[HALO:HALO + rows, cols]
        acc = cur
        for d in range(1, w):
            acc = acc + ext_ref[HALO - d:HALO - d + rows, cols]
        outs.append(acc * inv_cnt[g] - cur)
    return outs


def _pool_prompt_kernel(u_ref, halo_ref, o_ref, ext_ref):
    i = pl.program_id(0)
    ext_ref[HALO:, :] = u_ref[...]
    ext_ref[:HALO, :] = jnp.where(i == 0, 0.0, halo_ref[...])
    pos = i * POOL_TM + lax.broadcasted_iota(I32, (POOL_TM, 1), 0)
    inv_cnt = [1.0 / jnp.minimum(pos + 1, w).astype(F32) for w in POOL_WINDOWS]
    outs = _window_mean_minus_cur(ext_ref, POOL_TM, inv_cnt)
    for g in range(N_POOL_GROUPS):
        o_ref[:, g * POOL_GROUP:(g + 1) * POOL_GROUP] = outs[g]


def _pool_prompt(u):
    per = POOL_TM // HALO
    return pl.pallas_call(
        _pool_prompt_kernel,
        grid=(R // POOL_TM,),
        in_specs=[
            pl.BlockSpec((POOL_TM, D_POOL), lambda i: (i, 0)),
            pl.BlockSpec((HALO, D_POOL), lambda i: (jnp.maximum(i * per - 1, 0), 0)),
        ],
        out_specs=pl.BlockSpec((POOL_TM, D_POOL), lambda i: (i, 0)),
        out_shape=jax.ShapeDtypeStruct((R, D_POOL), F32),
        scratch_shapes=[pltpu.VMEM((POOL_TM + HALO, D_POOL), F32)],
        compiler_params=_cparams(("arbitrary",)),
        name="pool_prompt",
    )(u, u)


def _pool_sample_kernel(u_ref, st_ref, o_ref, ext_ref):
    for b in range(DEC_BATCH):
        ext_ref[:HALO, :] = st_ref[b]
        ext_ref[HALO:, :] = u_ref[b]
        inv_cnt = [1.0 / w for w in POOL_WINDOWS]
        outs = _window_mean_minus_cur(ext_ref, DEC_SEQ, inv_cnt)
        for g in range(N_POOL_GROUPS):
            o_ref[b, :, g * POOL_GROUP:(g + 1) * POOL_GROUP] = outs[g]


def _pool_sample(u_s, state16):
    return pl.pallas_call(
        _pool_sample_kernel,
        out_shape=jax.ShapeDtypeStruct((DEC_BATCH, DEC_SEQ, D_POOL), F32),
        scratch_shapes=[pltpu.VMEM((HALO + DEC_SEQ, D_POOL), F32)],
        compiler_params=pltpu.CompilerParams(vmem_limit_bytes=VMEM_LIMIT),
        name="pool_sample",
    )(u_s, state16)


MERGE_TM = 256


def _merge_kernel(x_ref, pooled_ref, attn_ref, gate_ref, wp_ref, ps_ref, wa_ref, wo_ref, gn_ref, o_ref):
    pooled = pooled_ref[...].astype(BF16)
    pool_out = jnp.concatenate(
        [jnp.dot(pooled[:, g * POOL_GROUP:(g + 1) * POOL_GROUP], wp_ref[g], preferred_element_type=F32)
         for g in range(N_POOL_GROUPS)], axis=1) * ps_ref[...]
    attn_out = jnp.dot(attn_ref[...], wa_ref[...], preferred_element_type=F32)
    gate = gate_ref[...]
    m = (jax.nn.sigmoid(gate[:, :D_MODEL]) * pool_out
         + jax.nn.sigmoid(gate[:, D_MODEL:]) * attn_out)
    mix = jnp.dot(m.astype(BF16), wo_ref[...], preferred_element_type=F32)
    o_ref[...] = x_ref[...] + _rms(mix, gn_ref[...])


def _merge(x_all, pooled, attn, gates, w_pool, pool_scale, w_attn_o, w_out, g_post):
    row = lambda i: (i, 0)
    const2 = lambda i: (0, 0)
    return pl.pallas_call(
        _merge_kernel,
        grid=(R // MERGE_TM,),
        in_specs=[
            pl.BlockSpec((MERGE_TM, D_MODEL), row),
            pl.BlockSpec((MERGE_TM, D_POOL), row),
            pl.BlockSpec((MERGE_TM, D_ATTN), row),
            pl.BlockSpec((MERGE_TM, 2 * D_MODEL), row),
            pl.BlockSpec((N_POOL_GROUPS, POOL_GROUP, POOL_OUT_GROUP), lambda i: (0, 0, 0)),
            pl.BlockSpec((1, D_MODEL), const2),
            pl.BlockSpec((D_ATTN, D_MODEL), const2),
            pl.BlockSpec((D_MODEL, D_MODEL), const2),
            pl.BlockSpec((1, D_MODEL), const2),
        ],
        out_specs=pl.BlockSpec((MERGE_TM, D_MODEL), row),
        out_shape=jax.ShapeDtypeStruct((R, D_MODEL), F32),
        compiler_params=_cparams(("arbitrary",)),
        name="merge",
    )(x_all, pooled, attn, gates, w_pool, pool_scale, w_attn_o, w_out, g_post)


FFN_TM = 544
FFN_TF = 512


def _ffn_kernel(x_ref, gpre_ref, wu_ref, wd_ref, gpost_ref, o_ref, h_ref, acc_ref):
    j = pl.program_id(1)

    @pl.when(j == 0)
    def _():
        h_ref[...] = _rms(x_ref[...], gpre_ref[...]).astype(BF16)
        acc_ref[...] = jnp.zeros_like(acc_ref)

    a = jnp.maximum(jnp.dot(h_ref[...], wu_ref[...], preferred_element_type=F32), 0.0)
    acc_ref[...] += jnp.dot((a * a).astype(BF16), wd_ref[...], preferred_element_type=F32)

    @pl.when(j == pl.num_programs(1) - 1)
    def _():
        o_ref[...] = x_ref[...] + _rms(acc_ref[...], gpost_ref[...])


def _ffn(x1, g_pre, w_up, w_down, g_post):
    return pl.pallas_call(
        _ffn_kernel,
        grid=(R // FFN_TM, D_FF // FFN_TF),
        in_specs=[
            pl.BlockSpec((FFN_TM, D_MODEL), lambda i, j: (i, 0)),
            pl.BlockSpec((1, D_MODEL), lambda i, j: (0, 0)),
            pl.BlockSpec((D_MODEL, FFN_TF), lambda i, j: (0, j)),
            pl.BlockSpec((FFN_TF, D_MODEL), lambda i, j: (j, 0)),
            pl.BlockSpec((1, D_MODEL), lambda i, j: (0, 0)),
        ],
        out_specs=pl.BlockSpec((FFN_TM, D_MODEL), lambda i, j: (i, 0)),
        out_shape=jax.ShapeDtypeStruct((R, D_MODEL), F32),
        scratch_shapes=[pltpu.VMEM((FFN_TM, D_MODEL), BF16), pltpu.VMEM((FFN_TM, D_MODEL), F32)],
        compiler_params=_cparams(("arbitrary", "arbitrary")),
        name="ffn",
    )(x1, g_pre, w_up, w_down, g_post)


INT_MAX = 2 ** 31 - 1
MAX_BISECT = 34


def _score_key(s):
    bits = pltpu.bitcast(s, I32)
    return jnp.where(bits < 0, bits ^ INT_MAX, bits)


def _floor_avg(lo, hi):
    return (lo >> 1) + (hi >> 1) + (lo & hi & 1)


def _bisect_threshold(count_ge, active0):
    lo0 = jnp.full(active0.shape, KEY_MASKED + 1, I32)
    hi0 = jnp.full(active0.shape, INT_MAX, I32)

    def cond(st):
        it, _, _, active = st
        return jnp.logical_and(it < MAX_BISECT, jnp.max(active) > 0)

    def step(lo, hi, active):
        mid = _floor_avg(lo, hi)
        cnt = count_ge(mid)
        on = active > 0
        ge = cnt >= TOPK
        lo = jnp.where(jnp.logical_and(on, ge), mid, lo)
        hi = jnp.where(jnp.logical_and(on, jnp.logical_not(ge)), mid, hi)
        still = jnp.logical_and(cnt != TOPK, hi - lo > 1)
        return lo, hi, jnp.where(jnp.logical_and(on, still), 1, 0).astype(I32)

    def body(st):
        it, lo, hi, active = st
        lo, hi, active = step(*step(lo, hi, active))
        return it + 2, lo, hi, active

    _, lo, _, _ = lax.while_loop(cond, body, (jnp.int32(0), lo0, hi0, active0.astype(I32)))
    return lo


def _head_slab(x_ref, h):
    j = h // 2
    return x_ref[:, j * LANES:(j + 1) * LANES]


def _place_half(slab, src_half, dst_half):
    lane = lax.broadcasted_iota(I32, slab.shape, 1)
    x = slab if src_half == dst_half else pltpu.roll(slab, HEAD_DIM, 1)
    return jnp.where((lane >= HEAD_DIM) == (dst_half == 1), x, 0.0)


def _gather_heads(o, rows):
    outs = []
    for j in range(N_HEADS // 2):
        n = (2 * j) // GROUP
        tc, th = n // 2, n % 2
        a0 = o[(2 * j) * rows:(2 * j + 1) * rows, tc * LANES:(tc + 1) * LANES]
        a1 = o[(2 * j + 1) * rows:(2 * j + 2) * rows, tc * LANES:(tc + 1) * LANES]
        x0 = a0 if th == 0 else pltpu.roll(a0, HEAD_DIM, 1)
        x1 = a1 if th == 1 else pltpu.roll(a1, HEAD_DIM, 1)
        lane = lax.broadcasted_iota(I32, x0.shape, 1)
        outs.append(jnp.where(lane < HEAD_DIM, x0, x1))
    return outs


def _build_qbd(q_ref, qbd_ref, rows):
    for h in range(N_HEADS):
        n = h // GROUP
        tc, th = n // 2, n % 2
        placed = _place_half(_head_slab(q_ref, h) * ATTN_SCALE, h % 2, th).astype(qbd_ref.dtype)
        for c in range(D_KV // LANES):
            qbd_ref[h * rows:(h + 1) * rows, c * LANES:(c + 1) * LANES] = (
                placed if c == tc else jnp.zeros_like(placed))


def _softmax_step(a3, m_ref, l_ref, acc_ref, vc, rows, v_transposed):
    n = N_HEADS * rows
    ch = a3.shape[2]
    m_prev = m_ref[...].reshape(N_HEADS, rows, 1)
    m_new = jnp.maximum(m_prev, jnp.max(a3, axis=2, keepdims=True))
    p = jnp.exp(a3 - m_new)
    alpha = jnp.exp(m_prev - m_new)
    l_ref[...] = (alpha * l_ref[...].reshape(N_HEADS, rows, 1)
                  + jnp.sum(p, axis=2, keepdims=True)).reshape(n, 1)
    m_ref[...] = m_new.reshape(n, 1)
    pb = p.reshape(n, ch).astype(BF16)
    if v_transposed:
        pv = lax.dot_general(pb, vc, NT_DIMS, preferred_element_type=F32)
    else:
        pv = jnp.dot(pb, vc, preferred_element_type=F32)
    acc_ref[...] = alpha.reshape(n, 1) * acc_ref[...] + pv


ATT_CH = 512
CNT_ROWS = 8 * SUBLANES
N_ATT_CH = R // ATT_CH
NT_DIMS = (((1,), (1,)), ((), ()))


def _attn_prompt_kernel(qi_ref, wi_ref, q_ref, kib_ref, kb_ref, vb_ref, o_ref,
                        s_ref, st_ref, qim_ref, wb_ref, qbd_ref, m_ref, l_ref, acc_ref, p_ref):
    i = pl.program_id(0)

    @pl.when(i >= N_QBLK)
    def _():
        o_ref[...] = jnp.zeros_like(o_ref)

    @pl.when(i < N_QBLK)
    def _():
        n_ch = (i * QB) // ATT_CH + 1
        qrow = i * QB + lax.broadcasted_iota(I32, (QB, 1), 0)

        wi = wi_ref[...]
        for h in range(N_IDX_HEADS):
            qim_ref[h * QB:(h + 1) * QB, :] = _place_half(_head_slab(qi_ref, h), h % 2, h % 2).astype(BF16)
            wb_ref[h] = jnp.broadcast_to(wi[:, h:h + 1], (QB, LANES))
        _build_qbd(q_ref, qbd_ref, QB)

        def score_chunk(c, carry):
            start = pl.multiple_of(c * ATT_CH, ATT_CH)
            kc = kib_ref[pl.ds(start, ATT_CH), :]
            acc = [jnp.zeros((QB, LANES), F32) for _ in range(ATT_CH // LANES)]
            hpd = 4
            for hg in range(N_IDX_HEADS // hpd):
                d = lax.dot_general(qim_ref[hg * hpd * QB:(hg + 1) * hpd * QB, :], kc, NT_DIMS,
                                    preferred_element_type=F32)
                for hh in range(hpd):
                    w = wb_ref[hg * hpd + hh]
                    for t in range(ATT_CH // LANES):
                        acc[t] = acc[t] + w * jnp.maximum(
                            d[hh * QB:(hh + 1) * QB, t * LANES:(t + 1) * LANES], 0.0)
            for t in range(ATT_CH // LANES):
                kpos = start + t * LANES + lax.broadcasted_iota(I32, (QB, LANES), 1)
                keys = jnp.where(kpos <= qrow, _score_key(acc[t]), KEY_MASKED)
                s_ref[c, :, t * LANES:(t + 1) * LANES] = keys
                st_ref[pl.ds(pl.multiple_of(start + t * LANES, LANES), LANES), :] = keys.T
            return carry

        lax.fori_loop(0, n_ch, score_chunk, 0)

        def count_ge(mid):
            def body(c, cnt):
                keys = st_ref[pl.ds(pl.multiple_of(c * ATT_CH, ATT_CH), ATT_CH), :]
                hit = jnp.where(keys >= mid, 1.0, 0.0)
                return cnt + jnp.sum(hit.reshape(ATT_CH // CNT_ROWS, CNT_ROWS, QB), axis=0)

            cnt = lax.fori_loop(0, n_ch, body, jnp.zeros((CNT_ROWS, QB), F32))
            return jnp.sum(cnt, axis=0, keepdims=True).astype(I32)

        qrow_l = i * QB + lax.broadcasted_iota(I32, (1, QB), 1)
        lo = _bisect_threshold(count_ge, qrow_l + 1 > TOPK)
        lo_b = jnp.broadcast_to(lo, (QB, QB)).T

        m_ref[...] = jnp.full(m_ref.shape, NEG, F32)
        l_ref[...] = jnp.zeros_like(l_ref)
        acc_ref[...] = jnp.zeros_like(acc_ref)
        n_t = ATT_CH // LANES

        def masked_logits(c):
            start = pl.multiple_of(c * ATT_CH, ATT_CH)
            bias = [jnp.where(s_ref[c, :, t * LANES:(t + 1) * LANES] >= lo_b, 0.0, NEG)
                    for t in range(n_t)]
            a = lax.dot_general(qbd_ref[...], kb_ref[pl.ds(start, ATT_CH), :], NT_DIMS,
                                preferred_element_type=F32)
            return start, [[a[h * QB:(h + 1) * QB, t * LANES:(t + 1) * LANES] + bias[t]
                            for t in range(n_t)] for h in range(N_HEADS)]

        def max_chunk(c, carry):
            _, a = masked_logits(c)
            for h in range(N_HEADS):
                rows = slice(h * QB, (h + 1) * QB)
                m = m_ref[rows, :]
                for t in range(n_t):
                    m = jnp.maximum(m, a[h][t])
                m_ref[rows, :] = m
            return carry

        lax.fori_loop(0, n_ch, max_chunk, 0)
        m_ref[...] = jnp.broadcast_to(jnp.max(m_ref[...], axis=1, keepdims=True), m_ref.shape)

        def value_chunk(c, carry):
            start, a = masked_logits(c)
            for h in range(N_HEADS):
                rows = slice(h * QB, (h + 1) * QB)
                m = m_ref[rows, :]
                l = l_ref[rows, :]
                for t in range(n_t):
                    p = jnp.exp(a[h][t] - m)
                    l = l + p
                    p_ref[rows, t * LANES:(t + 1) * LANES] = p.astype(BF16)
                l_ref[rows, :] = l
            acc_ref[...] += jnp.dot(p_ref[...], vb_ref[pl.ds(start, ATT_CH), :],
                                    preferred_element_type=F32)
            return carry

        lax.fori_loop(0, n_ch, value_chunk, 0)

        o = acc_ref[...] / jnp.sum(l_ref[...], axis=1, keepdims=True)
        for j, slab in enumerate(_gather_heads(o, QB)):
            o_ref[:, j * LANES:(j + 1) * LANES] = slab.astype(BF16)


def _attn_prompt(qi, wi, q, kib, kb, vb):
    row = lambda i: (i, 0)
    full = lambda i: (0, 0)
    nrow = N_HEADS * QB
    return pl.pallas_call(
        _attn_prompt_kernel,
        grid=(R // QB,),
        in_specs=[
            pl.BlockSpec((QB, N_IDX_HEADS * IDX_DIM), row),
            pl.BlockSpec((QB, LANES), row),
            pl.BlockSpec((QB, D_ATTN), row),
            pl.BlockSpec((R, LANES), full),
            pl.BlockSpec((R, D_KV), full),
            pl.BlockSpec((R, D_KV), full),
        ],
        out_specs=pl.BlockSpec((QB, D_ATTN), row),
        out_shape=jax.ShapeDtypeStruct((R, D_ATTN), BF16),
        scratch_shapes=[
            pltpu.VMEM((N_ATT_CH, QB, ATT_CH), I32),
            pltpu.VMEM((R, QB), I32),
            pltpu.VMEM((nrow, LANES), BF16),
            pltpu.VMEM((N_IDX_HEADS, QB, LANES), F32),
            pltpu.VMEM((nrow, D_KV), BF16),
            pltpu.VMEM((nrow, LANES), F32),
            pltpu.VMEM((nrow, LANES), F32),
            pltpu.VMEM((nrow, D_KV), F32),
            pltpu.VMEM((nrow, ATT_CH), BF16),
        ],
        compiler_params=_cparams(("arbitrary",)),
        name="attn_prompt",
    )(qi, wi, q, kib, kb, vb)


S_CH = 2048
S_NCH = PAST_LEN // S_CH
S_PPC = S_CH // PAGE_SIZE
S_ROWS = N_HEADS * DEC_SEQ


def _page_lanes(p):
    return pl.ds(pl.multiple_of(p * PAGE_SIZE, PAGE_SIZE), PAGE_SIZE)


def _kidx_page_copy(pt_ref, kidx_hbm, kbuf, sem, b, p, slot):
    return pltpu.make_async_copy(
        kidx_hbm.at[pt_ref[b, p]], kbuf.at[slot, :, _page_lanes(p)], sem.at[slot])


def _sample_score_kernel(pt_ref, qi_ref, wi_ref, kin_ref, kidx_hbm, s_ref, sn_ref,
                         kbuf, sem, knp_ref):
    b = pl.program_id(0)
    nb = pl.num_programs(0)
    slot = b % 2

    def start_batch(bb, sl):
        def body(p, c):
            _kidx_page_copy(pt_ref, kidx_hbm, kbuf, sem, bb, p, sl).start()
            return c
        lax.fori_loop(0, N_PAGES, body, 0)

    @pl.when(b == 0)
    def _():
        start_batch(0, 0)

    @pl.when(b + 1 < nb)
    def _():
        start_batch(b + 1, 1 - slot)

    def wait_body(p, c):
        _kidx_page_copy(pt_ref, kidx_hbm, kbuf, sem, b, p, slot).wait()
        return c
    lax.fori_loop(0, N_PAGES, wait_body, 0)

    qi = qi_ref[0]
    qis = jnp.concatenate(
        [qi[:, h * IDX_DIM:(h + 1) * IDX_DIM] for h in range(N_IDX_HEADS)], axis=0).astype(BF16)
    wi = wi_ref[0]
    wb = [jnp.broadcast_to(wi[:, h:h + 1], (DEC_SEQ, LANES)) for h in range(N_IDX_HEADS)]

    def head_sum(d, width):
        outs = []
        for t in range(width // LANES):
            acc = jnp.zeros((DEC_SEQ, LANES), F32)
            for h in range(N_IDX_HEADS):
                acc = acc + wb[h] * jnp.maximum(
                    d[h * DEC_SEQ:(h + 1) * DEC_SEQ, t * LANES:(t + 1) * LANES], 0.0)
            outs.append(acc)
        return outs

    def score_chunk(c, carry):
        kc = kbuf[slot, :, pl.ds(pl.multiple_of(c * S_CH, S_CH), S_CH)].astype(BF16)
        d = jnp.dot(qis, kc, preferred_element_type=F32)
        for t, acc in enumerate(head_sum(d, S_CH)):
            s_ref[c, :, t * LANES:(t + 1) * LANES] = _score_key(acc)
        return carry
    lax.fori_loop(0, S_NCH, score_chunk, 0)

    knp_ref[...] = jnp.zeros_like(knp_ref)
    knp_ref[:DEC_SEQ, :] = kin_ref[0][:, :IDX_DIM]
    d = lax.dot_general(qis, knp_ref[...].astype(BF16), NT_DIMS, preferred_element_type=F32)
    kpos = lax.broadcasted_iota(I32, (DEC_SEQ, LANES), 1)
    srow = lax.broadcasted_iota(I32, (DEC_SEQ, LANES), 0)
    sn_ref[...] = jnp.where(kpos <= srow, _score_key(head_sum(d, LANES)[0]), KEY_MASKED)


def _sample_scores(page_table, qi_s, wi_s, ki_s, cache_kidx):
    blk = lambda b, pt: (b, 0, 0)
    grid_spec = pltpu.PrefetchScalarGridSpec(
        num_scalar_prefetch=1,
        grid=(DEC_BATCH,),
        in_specs=[
            pl.BlockSpec((1, DEC_SEQ, N_IDX_HEADS * IDX_DIM), blk),
            pl.BlockSpec((1, DEC_SEQ, LANES), blk),
            pl.BlockSpec((1, DEC_SEQ, LANES), blk),
            pl.BlockSpec(memory_space=pl.ANY),
        ],
        out_specs=[
            pl.BlockSpec((S_NCH, DEC_SEQ, S_CH), lambda b, pt: (0, b, 0)),
            pl.BlockSpec((DEC_SEQ, LANES), lambda b, pt: (b, 0)),
        ],
        scratch_shapes=[
            pltpu.VMEM((2, IDX_DIM, PAST_LEN), F32),
            pltpu.SemaphoreType.DMA((2,)),
            pltpu.VMEM((LANES, IDX_DIM), F32),
        ],
    )
    return pl.pallas_call(
        _sample_score_kernel,
        grid_spec=grid_spec,
        out_shape=(
            jax.ShapeDtypeStruct((S_NCH, NS, S_CH), I32),
            jax.ShapeDtypeStruct((NS, LANES), I32),
        ),
        compiler_params=_cparams(("arbitrary",)),
        name="sample_scores",
    )(page_table, qi_s, wi_s, ki_s, cache_kidx)


def _sample_threshold_kernel(s_ref, sn_ref, bp_ref, bn_ref):
    def count_ge(mid):
        def body(c, cnt):
            hit = jnp.where(s_ref[c] >= mid, 1.0, 0.0)
            for t in range(S_CH // LANES):
                cnt = cnt + hit[:, t * LANES:(t + 1) * LANES]
            return cnt

        cnt = lax.fori_loop(0, S_NCH, body, jnp.where(sn_ref[...] >= mid, 1.0, 0.0))
        return jnp.sum(cnt, axis=1, keepdims=True).astype(I32)

    lo = _bisect_threshold(count_ge, jnp.ones((NS, 1), I32) > 0)
    for c in range(S_NCH):
        bp_ref[c] = jnp.where(s_ref[c] >= lo, 0.0, NEG)
    bn_ref[...] = jnp.where(sn_ref[...] >= lo, 0.0, NEG)


def _sample_threshold(keys_past, keys_new):
    vmem = pl.BlockSpec(memory_space=pltpu.VMEM)
    return pl.pallas_call(
        _sample_threshold_kernel,
        in_specs=[vmem, vmem],
        out_specs=[vmem, vmem],
        out_shape=(
            jax.ShapeDtypeStruct((S_NCH, NS, S_CH), F32),
            jax.ShapeDtypeStruct((NS, LANES), F32),
        ),
        compiler_params=pltpu.CompilerParams(vmem_limit_bytes=VMEM_LIMIT),
        name="sample_threshold",
    )(keys_past, keys_new)


def _kv_page_copies(pt_ref, ck_hbm, cv_hbm, kbuf, vbuf, sem, b, c, p, slot):
    page = pt_ref[b, c * S_PPC + p]
    dst = _page_lanes(p)
    return (pltpu.make_async_copy(ck_hbm.at[page], kbuf.at[slot, :, dst], sem.at[0, slot]),
            pltpu.make_async_copy(cv_hbm.at[page], vbuf.at[slot, :, dst], sem.at[1, slot]))


def _sample_attn_kernel(pt_ref, q_ref, bp_ref, bn_ref, kn_ref, vn_ref, ck_hbm, cv_hbm, o_ref,
                        kbuf, vbuf, sem, qbd_ref, m_ref, l_ref, acc_ref, knp_ref, vnp_ref):
    b = pl.program_id(0)
    c = pl.program_id(1)
    step = b * S_NCH + c
    n_steps = pl.num_programs(0) * S_NCH
    slot = step % 2

    def start_chunk(st, sl):
        bb = st // S_NCH
        cc = st % S_NCH

        def body(p, carry):
            for cp in _kv_page_copies(pt_ref, ck_hbm, cv_hbm, kbuf, vbuf, sem, bb, cc, p, sl):
                cp.start()
            return carry
        lax.fori_loop(0, S_PPC, body, 0)

    @pl.when(step == 0)
    def _():
        start_chunk(0, 0)

    @pl.when(step + 1 < n_steps)
    def _():
        start_chunk(step + 1, 1 - slot)

    def wait_body(p, carry):
        for cp in _kv_page_copies(pt_ref, ck_hbm, cv_hbm, kbuf, vbuf, sem, b, c, p, slot):
            cp.wait()
        return carry
    lax.fori_loop(0, S_PPC, wait_body, 0)

    @pl.when(c == 0)
    def _():
        _build_qbd(q_ref.at[0], qbd_ref, DEC_SEQ)
        m_ref[...] = jnp.full(m_ref.shape, NEG, F32)
        l_ref[...] = jnp.zeros_like(l_ref)
        acc_ref[...] = jnp.zeros_like(acc_ref)

    def update(kc, vc, bias, feature_major):
        qbd = qbd_ref[...].astype(BF16)
        if feature_major:
            a = jnp.dot(qbd, kc, preferred_element_type=F32)
        else:
            a = lax.dot_general(qbd, kc, NT_DIMS, preferred_element_type=F32)
        a3 = a.reshape(N_HEADS, DEC_SEQ, a.shape[1]) + bias[None]
        _softmax_step(a3, m_ref, l_ref, acc_ref, vc, DEC_SEQ, feature_major)

    update(kbuf[slot].astype(BF16), vbuf[slot].astype(BF16), bp_ref[0], True)

    @pl.when(c == S_NCH - 1)
    def _():
        knp_ref[...] = jnp.zeros_like(knp_ref)
        vnp_ref[...] = jnp.zeros_like(vnp_ref)
        knp_ref[:DEC_SEQ, :] = kn_ref[0]
        vnp_ref[:DEC_SEQ, :] = vn_ref[0]
        update(knp_ref[...].astype(BF16), vnp_ref[...].astype(BF16), bn_ref[...], False)
        o = acc_ref[...] / l_ref[...]
        for j, slab in enumerate(_gather_heads(o, DEC_SEQ)):
            o_ref[0, :, j * LANES:(j + 1) * LANES] = slab


def _sample_attn(page_table, q_s, bias_past, bias_new, k_s, v_s, cache_k, cache_v):
    blk = lambda b, c, pt: (b, 0, 0)
    grid_spec = pltpu.PrefetchScalarGridSpec(
        num_scalar_prefetch=1,
        grid=(DEC_BATCH, S_NCH),
        in_specs=[
            pl.BlockSpec((1, DEC_SEQ, D_ATTN), blk),
            pl.BlockSpec((1, DEC_SEQ, S_CH), lambda b, c, pt: (c, b, 0)),
            pl.BlockSpec((DEC_SEQ, LANES), lambda b, c, pt: (b, 0)),
            pl.BlockSpec((1, DEC_SEQ, D_KV), blk),
            pl.BlockSpec((1, DEC_SEQ, D_KV), blk),
            pl.BlockSpec(memory_space=pl.ANY),
            pl.BlockSpec(memory_space=pl.ANY),
        ],
        out_specs=pl.BlockSpec((1, DEC_SEQ, D_ATTN), blk),
        scratch_shapes=[
            pltpu.VMEM((2, D_KV, S_CH), F32),
            pltpu.VMEM((2, D_KV, S_CH), F32),
            pltpu.SemaphoreType.DMA((2, 2)),
            pltpu.VMEM((S_ROWS, D_KV), F32),
            pltpu.VMEM((S_ROWS, 1), F32),
            pltpu.VMEM((S_ROWS, 1), F32),
            pltpu.VMEM((S_ROWS, D_KV), F32),
            pltpu.VMEM((LANES, D_KV), F32),
            pltpu.VMEM((LANES, D_KV), F32),
        ],
    )
    return pl.pallas_call(
        _sample_attn_kernel,
        grid_spec=grid_spec,
        out_shape=jax.ShapeDtypeStruct((DEC_BATCH, DEC_SEQ, D_ATTN), F32),
        compiler_params=_cparams(("arbitrary", "arbitrary")),
        name="sample_attn",
    )(page_table, q_s, bias_past, bias_new, k_s, v_s, cache_k, cache_v)


def _rope_tables():
    pos = np.zeros((R,), np.float32)
    pos[:T] = np.arange(T)
    pos[ROW_S:ROW_S + NS] = np.tile(PAST_LEN + np.arange(DEC_SEQ), DEC_BATCH)
    half = HEAD_DIM // 2
    inv_freq = ROPE_THETA ** (-(jnp.arange(half, dtype=F32) * 2.0 / HEAD_DIM))
    ang = jnp.asarray(pos)[:, None] * inv_freq
    cos = jnp.tile(jnp.cos(ang), (1, LANES // half))
    sin = jnp.sin(ang)
    sin = jnp.tile(jnp.concatenate([-sin, sin], axis=1), (1, LANES // HEAD_DIM))
    return cos, sin


def kernel(x_prompt, x_sample, cache_k, cache_v, cache_kidx, state_pool, page_table, meta_tokens,
           norm_mix_pre, w_in, idx_k_norm, w_pool, pool_scale, w_attn_o, w_out, norm_mix_post,
           norm_ffn_pre, w_up, w_down, norm_ffn_post):
    x_all = jnp.concatenate([
        meta_tokens, x_prompt[0], jnp.zeros((ROW_S - T, D_MODEL), F32),
        x_sample.reshape(NS, D_MODEL), jnp.zeros((R - ROW_S - NS, D_MODEL), F32)], axis=0)

    w = w_in[0]
    o_u, o_q, o_k, o_v, o_qi, o_ki, o_wi, o_gp = np.cumsum((0, D_POOL, D_ATTN, D_KV, D_KV,
                                                            N_IDX_HEADS * IDX_DIM, IDX_DIM, N_IDX_HEADS))
    n_kiwi = IDX_DIM + N_IDX_HEADS
    w_a = jnp.concatenate([
        w[:, o_q:o_v], w[:, o_qi:o_gp],
        jnp.zeros((D_MODEL, W_A_COLS - D_ATTN - D_KV - N_IDX_HEADS * IDX_DIM - n_kiwi), F32)],
        axis=1).astype(BF16)
    w_b = jnp.concatenate([w[:, o_v:o_qi], w[:, o_u:o_q]], axis=1).astype(BF16)
    w_c = w[:, o_gp:].astype(BF16)

    g_pre = norm_mix_pre[0][None]
    cos, sin = _rope_tables()
    gk = jnp.tile(idx_k_norm[0], LANES // IDX_DIM)[None]

    q, k, kb, qi, ki, kib, wi = _proj_a(x_all, g_pre, w_a, cos, sin, gk)
    v, vb, u = _proj_b(x_all, g_pre, w_b)
    gates = _proj_c(x_all, g_pre, w_c)

    def sample_rows(a):
        return a[ROW_S:ROW_S + NS].reshape(DEC_BATCH, DEC_SEQ, a.shape[1])

    pooled = _pool_prompt(u)
    u_s = sample_rows(u)
    state = state_pool[0]
    state16 = jnp.pad(state, ((0, 0), (HALO - POOL_BUF, 0), (0, 0)))
    pooled_s = _pool_sample(u_s, state16)
    pooled = lax.dynamic_update_slice(pooled, pooled_s.reshape(NS, D_POOL), (ROW_S, 0))

    attn = _attn_prompt(qi, wi, q, kib, kb, vb)
    k_s, v_s = sample_rows(k), sample_rows(v)
    n_pool = cache_k.shape[1]
    kidx_fm = jnp.transpose(cache_kidx[0], (0, 2, 1))
    k_fm = jnp.transpose(cache_k[0], (0, 2, 3, 1)).reshape(n_pool, D_KV, PAGE_SIZE)
    v_fm = jnp.transpose(cache_v[0], (0, 2, 3, 1)).reshape(n_pool, D_KV, PAGE_SIZE)
    keys_past, keys_new = _sample_scores(page_table, sample_rows(qi), sample_rows(wi), sample_rows(ki),
                                         kidx_fm)
    bias_past, bias_new = _sample_threshold(keys_past, keys_new)
    attn_s = _sample_attn(page_table, sample_rows(q), bias_past, bias_new, k_s, v_s, k_fm, v_fm)
    attn = lax.dynamic_update_slice(attn, attn_s.reshape(NS, D_ATTN).astype(BF16), (ROW_S, 0))

    x1 = _merge(x_all, pooled, attn, gates, w_pool[0].astype(BF16), pool_scale[0][None],
                w_attn_o[0].astype(BF16), w_out[0].astype(BF16), norm_mix_post[0][None])
    y = _ffn(x1, norm_ffn_pre[0][None], w_up[0].astype(BF16), w_down[0].astype(BF16),
             norm_ffn_post[0][None])

    y_prompt = y[N_META:T][None]
    y_sample = y[ROW_S:ROW_S + NS].reshape(DEC_BATCH, DEC_SEQ, D_MODEL)
    kv_shape = (1, 1, T, N_KV_HEADS, HEAD_DIM)
    kv_s_shape = (1, DEC_BATCH, DEC_SEQ, N_KV_HEADS, HEAD_DIM)
    return (
        y_prompt, y_sample,
        k[:T].reshape(kv_shape), v[:T].reshape(kv_shape), ki[:T, :IDX_DIM][None, None],
        u[T - POOL_BUF:T][None, None],
        k_s.reshape(kv_s_shape), v_s.reshape(kv_s_shape), sample_rows(ki)[None, :, :, :IDX_DIM],
        jnp.concatenate([state[:, DEC_SEQ:], u_s], axis=1)[None],
    )
```

```python
import functools

import jax
import jax.numpy as jnp
import numpy as np
from jax import lax
from jax.experimental import pallas as pl
from jax.experimental.pallas import tpu as pltpu

F32 = jnp.float32
BF16 = jnp.bfloat16
I32 = jnp.int32

D_MODEL = 2048
SEQ = 8192
DEC_BATCH = 32
DEC_SEQ = 8
PAST_LEN = 16384
PAGE_SIZE = 128
N_PAGES = PAST_LEN // PAGE_SIZE
N_META = 16
N_HEADS = 16
N_KV_HEADS = 4
HEAD_DIM = 64
GROUP = N_HEADS // N_KV_HEADS
D_ATTN = N_HEADS * HEAD_DIM
D_KV = N_KV_HEADS * HEAD_DIM
ATTN_SCALE = HEAD_DIM ** -0.5
N_IDX_HEADS = 16
IDX_DIM = 64
INDEX_W_SCALE = (N_IDX_HEADS ** -0.5) * (IDX_DIM ** -0.5)
TOPK = 256
POOL_WINDOWS = (2, 4, 8, 16)
N_POOL_GROUPS = 4
D_POOL = D_MODEL // 2
POOL_GROUP = D_POOL // N_POOL_GROUPS
POOL_OUT_GROUP = D_MODEL // N_POOL_GROUPS
POOL_BUF = max(POOL_WINDOWS) - 1
D_FF = 4 * D_MODEL
ROPE_THETA = 10000.0
EPS = 1e-6

LANES = 128
SUBLANES = 8
T = SEQ + N_META
QB = 128
N_QBLK = -(-T // QB)
ROW_S = N_QBLK * QB
NS = DEC_BATCH * DEC_SEQ
R = 8704
VMEM_LIMIT = 56 * 1024 * 1024

NEG = -1e30
KEY_MASKED = -2 ** 31


def _cparams(sem):
    return pltpu.CompilerParams(dimension_semantics=sem, vmem_limit_bytes=VMEM_LIMIT)


def _rms(x, g):
    return x * lax.rsqrt(jnp.mean(x * x, axis=-1, keepdims=True) + EPS) * g


def _swap_halves(x):
    lane = lax.broadcasted_iota(I32, x.shape, 1)
    return jnp.where(lane % HEAD_DIM < HEAD_DIM // 2,
                     pltpu.roll(x, LANES - HEAD_DIM // 2, 1),
                     pltpu.roll(x, HEAD_DIM // 2, 1))


def _rope_cols(x, cos, sin):
    outs = []
    for c in range(x.shape[1] // LANES):
        xc = x[:, c * LANES:(c + 1) * LANES]
        outs.append(xc * cos + _swap_halves(xc) * sin)
    return outs


PROJ_TM = 544
N_ROPE_A = D_ATTN + D_KV
PROJ_TN_A = N_ROPE_A
W_A_COLS = 2 * PROJ_TN_A
W_B_COLS = D_KV + D_POOL
W_C_COLS = 2 * D_MODEL
PROJ_TN_C = 1024


def _norm_to_scratch(x_ref, g_ref, xn_ref):
    @pl.when(pl.program_id(1) == 0)
    def _():
        xn_ref[...] = _rms(x_ref[...], g_ref[...]).astype(BF16)


def _proj_a_kernel(x_ref, g_ref, w_ref, cos_ref, sin_ref, gk_ref,
                   q_ref, k_ref, kb_ref, qi_ref, ki_ref, kib_ref, wi_ref, xn_ref):
    _norm_to_scratch(x_ref, g_ref, xn_ref)
    j = pl.program_id(1)
    p = jnp.dot(xn_ref[...], w_ref[...], preferred_element_type=F32)
    cos = cos_ref[...]
    sin = sin_ref[...]

    @pl.when(j == 0)
    def _():
        cols = _rope_cols(p, cos, sin)
        nq = D_ATTN // LANES
        for c in range(nq):
            q_ref[:, c * LANES:(c + 1) * LANES] = cols[c]
        for c in range(D_KV // LANES):
            k_ref[:, c * LANES:(c + 1) * LANES] = cols[nq + c]
            kb_ref[:, c * LANES:(c + 1) * LANES] = cols[nq + c].astype(BF16)

    @pl.when(j == 1)
    def _():
        nqi = N_IDX_HEADS * IDX_DIM // LANES
        cols = _rope_cols(p[:, :nqi * LANES], cos, sin)
        for c in range(nqi):
            qi_ref[:, c * LANES:(c + 1) * LANES] = cols[c]
        slab = p[:, nqi * LANES:(nqi + 1) * LANES]
        lane = lax.broadcasted_iota(I32, slab.shape, 1)
        is_ki = lane < IDX_DIM
        ms = jnp.sum(jnp.where(is_ki, slab * slab, 0.0), axis=-1, keepdims=True) / IDX_DIM
        kin = slab * lax.rsqrt(ms + EPS) * gk_ref[...]
        kin = kin * cos + _swap_halves(kin) * sin
        ki2 = jnp.where(is_ki, kin, pltpu.roll(kin, IDX_DIM, 1))
        ki_ref[...] = ki2
        kib_ref[...] = ki2.astype(BF16)
        wi_ref[...] = pltpu.roll(slab, LANES - IDX_DIM, 1) * INDEX_W_SCALE


def _proj_a(x_all, g, w_a, cos, sin, gk):
    n_i = R // PROJ_TM
    row = lambda i, j: (i, 0)
    outs = (
        jax.ShapeDtypeStruct((R, D_ATTN), F32),
        jax.ShapeDtypeStruct((R, D_KV), F32),
        jax.ShapeDtypeStruct((R, D_KV), BF16),
        jax.ShapeDtypeStruct((R, N_IDX_HEADS * IDX_DIM), F32),
        jax.ShapeDtypeStruct((R, LANES), F32),
        jax.ShapeDtypeStruct((R, LANES), BF16),
        jax.ShapeDtypeStruct((R, LANES), F32),
    )
    return pl.pallas_call(
        _proj_a_kernel,
        grid=(n_i, 2),
        in_specs=[
            pl.BlockSpec((PROJ_TM, D_MODEL), row),
            pl.BlockSpec((1, D_MODEL), lambda i, j: (0, 0)),
            pl.BlockSpec((D_MODEL, PROJ_TN_A), lambda i, j: (0, j)),
            pl.BlockSpec((PROJ_TM, LANES), row),
            pl.BlockSpec((PROJ_TM, LANES), row),
            pl.BlockSpec((1, LANES), lambda i, j: (0, 0)),
        ],
        out_specs=[pl.BlockSpec((PROJ_TM, o.shape[1]), row) for o in outs],
        out_shape=outs,
        scratch_shapes=[pltpu.VMEM((PROJ_TM, D_MODEL), BF16)],
        compiler_params=_cparams(("arbitrary", "arbitrary")),
        name="proj_rope",
    )(x_all, g, w_a, cos, sin, gk)


def _proj_b_kernel(x_ref, g_ref, w_ref, v_ref, vb_ref, u_ref, xn_ref):
    _norm_to_scratch(x_ref, g_ref, xn_ref)
    p = jnp.dot(xn_ref[...], w_ref[...], preferred_element_type=F32)
    v_ref[...] = p[:, :D_KV]
    vb_ref[...] = p[:, :D_KV].astype(BF16)
    u_ref[...] = p[:, D_KV:]


def _proj_b(x_all, g, w_b):
    n_i = R // PROJ_TM
    row = lambda i, j: (i, 0)
    outs = (
        jax.ShapeDtypeStruct((R, D_KV), F32),
        jax.ShapeDtypeStruct((R, D_KV), BF16),
        jax.ShapeDtypeStruct((R, D_POOL), F32),
    )
    return pl.pallas_call(
        _proj_b_kernel,
        grid=(n_i, 1),
        in_specs=[
            pl.BlockSpec((PROJ_TM, D_MODEL), row),
            pl.BlockSpec((1, D_MODEL), lambda i, j: (0, 0)),
            pl.BlockSpec((D_MODEL, W_B_COLS), lambda i, j: (0, 0)),
        ],
        out_specs=[pl.BlockSpec((PROJ_TM, o.shape[1]), row) for o in outs],
        out_shape=outs,
        scratch_shapes=[pltpu.VMEM((PROJ_TM, D_MODEL), BF16)],
        compiler_params=_cparams(("arbitrary", "arbitrary")),
        name="proj_vu",
    )(x_all, g, w_b)


def _proj_c_kernel(x_ref, g_ref, w_ref, o_ref, xn_ref):
    _norm_to_scratch(x_ref, g_ref, xn_ref)
    o_ref[...] = jnp.dot(xn_ref[...], w_ref[...], preferred_element_type=F32)


def _proj_c(x_all, g, w_c):
    n_i = R // PROJ_TM
    return pl.pallas_call(
        _proj_c_kernel,
        grid=(n_i, W_C_COLS // PROJ_TN_C),
        in_specs=[
            pl.BlockSpec((PROJ_TM, D_MODEL), lambda i, j: (i, 0)),
            pl.BlockSpec((1, D_MODEL), lambda i, j: (0, 0)),
            pl.BlockSpec((D_MODEL, PROJ_TN_C), lambda i, j: (0, j)),
        ],
        out_specs=pl.BlockSpec((PROJ_TM, PROJ_TN_C), lambda i, j: (i, j)),
        out_shape=jax.ShapeDtypeStruct((R, W_C_COLS), F32),
        scratch_shapes=[pltpu.VMEM((PROJ_TM, D_MODEL), BF16)],
        compiler_params=_cparams(("arbitrary", "arbitrary")),
        name="proj_gates",
    )(x_all, g, w_c)


POOL_TM = 512
HALO = 16


def _window_mean_minus_cur(ext_ref, rows, inv_cnt):
    outs = []
    for g, w in enumerate(POOL_WINDOWS):
        cols = slice(g * POOL_GROUP, (g + 1) * POOL_GROUP)
        cur = ext_ref[HALO:HALO + rows, cols]
        acc = cur
        for d in range(1, w):
            acc = acc + ext_ref[HALO - d:HALO - d + rows, cols]
        outs.append(acc * inv_cnt[g] - cur)
    return outs


def _pool_prompt_kernel(u_ref, halo_ref, o_ref, ext_ref):
    i = pl.program_id(0)
    ext_ref[HALO:, :] = u_ref[...]
    ext_ref[:HALO, :] = jnp.where(i == 0, 0.0, halo_ref[...])
    pos = i * POOL_TM + lax.broadcasted_iota(I32, (POOL_TM, 1), 0)
    inv_cnt = [1.0 / jnp.minimum(pos + 1, w).astype(F32) for w in POOL_WINDOWS]
    outs = _window_mean_minus_cur(ext_ref, POOL_TM, inv_cnt)
    for g in range(N_POOL_GROUPS):
        o_ref[:, g * POOL_GROUP:(g + 1) * POOL_GROUP] = outs[g]


def _pool_prompt(u):
    per = POOL_TM // HALO
    return pl.pallas_call(
        _pool_prompt_kernel,
        grid=(R // POOL_TM,),
        in_specs=[
            pl.BlockSpec((POOL_TM, D_POOL), lambda i: (i, 0)),
            pl.BlockSpec((HALO, D_POOL), lambda i: (jnp.maximum(i * per - 1, 0), 0)),
        ],
        out_specs=pl.BlockSpec((POOL_TM, D_POOL), lambda i: (i, 0)),
        out_shape=jax.ShapeDtypeStruct((R, D_POOL), F32),
        scratch_shapes=[pltpu.VMEM((POOL_TM + HALO, D_POOL), F32)],
        compiler_params=_cparams(("arbitrary",)),
        name="pool_prompt",
    )(u, u)


def _pool_sample_kernel(u_ref, st_ref, o_ref, ext_ref):
    for b in range(DEC_BATCH):
        ext_ref[:HALO, :] = st_ref[b]
        ext_ref[HALO:, :] = u_ref[b]
        inv_cnt = [1.0 / w for w in POOL_WINDOWS]
        outs = _window_mean_minus_cur(ext_ref, DEC_SEQ, inv_cnt)
        for g in range(N_POOL_GROUPS):
            o_ref[b, :, g * POOL_GROUP:(g + 1) * POOL_GROUP] = outs[g]


def _pool_sample(u_s, state16):
    return pl.pallas_call(
        _pool_sample_kernel,
        out_shape=jax.ShapeDtypeStruct((DEC_BATCH, DEC_SEQ, D_POOL), F32),
        scratch_shapes=[pltpu.VMEM((HALO + DEC_SEQ, D_POOL), F32)],
        compiler_params=pltpu.CompilerParams(vmem_limit_bytes=VMEM_LIMIT),
        name="pool_sample",
    )(u_s, state16)


MERGE_TM = 256


def _merge_kernel(x_ref, pooled_ref, attn_ref, gate_ref, wp_ref, ps_ref, wa_ref, wo_ref, gn_ref, o_ref):
    pooled = pooled_ref[...].astype(BF16)
    pool_out = jnp.concatenate(
        [jnp.dot(pooled[:, g * POOL_GROUP:(g + 1) * POOL_GROUP], wp_ref[g], preferred_element_type=F32)
         for g in range(N_POOL_GROUPS)], axis=1) * ps_ref[...]
    attn_out = jnp.dot(attn_ref[...], wa_ref[...], preferred_element_type=F32)
    gate = gate_ref[...]
    m = (jax.nn.sigmoid(gate[:, :D_MODEL]) * pool_out
         + jax.nn.sigmoid(gate[:, D_MODEL:]) * attn_out)
    mix = jnp.dot(m.astype(BF16), wo_ref[...], preferred_element_type=F32)
    o_ref[...] = x_ref[...] + _rms(mix, gn_ref[...])


def _merge(x_all, pooled, attn, gates, w_pool, pool_scale, w_attn_o, w_out, g_post):
    row = lambda i: (i, 0)
    const2 = lambda i: (0, 0)
    return pl.pallas_call(
        _merge_kernel,
        grid=(R // MERGE_TM,),
        in_specs=[
            pl.BlockSpec((MERGE_TM, D_MODEL), row),
            pl.BlockSpec((MERGE_TM, D_POOL), row),
            pl.BlockSpec((MERGE_TM, D_ATTN), row),
            pl.BlockSpec((MERGE_TM, 2 * D_MODEL), row),
            pl.BlockSpec((N_POOL_GROUPS, POOL_GROUP, POOL_OUT_GROUP), lambda i: (0, 0, 0)),
            pl.BlockSpec((1, D_MODEL), const2),
            pl.BlockSpec((D_ATTN, D_MODEL), const2),
            pl.BlockSpec((D_MODEL, D_MODEL), const2),
            pl.BlockSpec((1, D_MODEL), const2),
        ],
        out_specs=pl.BlockSpec((MERGE_TM, D_MODEL), row),
        out_shape=jax.ShapeDtypeStruct((R, D_MODEL), F32),
        compiler_params=_cparams(("arbitrary",)),
        name="merge",
    )(x_all, pooled, attn, gates, w_pool, pool_scale, w_attn_o, w_out, g_post)


FFN_TM = 544
FFN_TF = 512


def _ffn_kernel(x_ref, gpre_ref, wu_ref, wd_ref, gpost_ref, o_ref, h_ref, acc_ref):
    j = pl.program_id(1)

    @pl.when(j == 0)
    def _():
        h_ref[...] = _rms(x_ref[...], gpre_ref[...]).astype(BF16)
        acc_ref[...] = jnp.zeros_like(acc_ref)

    a = jnp.maximum(jnp.dot(h_ref[...], wu_ref[...], preferred_element_type=F32), 0.0)
    acc_ref[...] += jnp.dot((a * a).astype(BF16), wd_ref[...], preferred_element_type=F32)

    @pl.when(j == pl.num_programs(1) - 1)
    def _():
        o_ref[...] = x_ref[...] + _rms(acc_ref[...], gpost_ref[...])


def _ffn(x1, g_pre, w_up, w_down, g_post):
    return pl.pallas_call(
        _ffn_kernel,
        grid=(R // FFN_TM, D_FF // FFN_TF),
        in_specs=[
            pl.BlockSpec((FFN_TM, D_MODEL), lambda i, j: (i, 0)),
            pl.BlockSpec((1, D_MODEL), lambda i, j: (0, 0)),
            pl.BlockSpec((D_MODEL, FFN_TF), lambda i, j: (0, j)),
            pl.BlockSpec((FFN_TF, D_MODEL), lambda i, j: (j, 0)),
            pl.BlockSpec((1, D_MODEL), lambda i, j: (0, 0)),
        ],
        out_specs=pl.BlockSpec((FFN_TM, D_MODEL), lambda i, j: (i, 0)),
        out_shape=jax.ShapeDtypeStruct((R, D_MODEL), F32),
        scratch_shapes=[pltpu.VMEM((FFN_TM, D_MODEL), BF16), pltpu.VMEM((FFN_TM, D_MODEL), F32)],
        compiler_params=_cparams(("arbitrary", "arbitrary")),
        name="ffn",
    )(x1, g_pre, w_up, w_down, g_post)


INT_MAX = 2 ** 31 - 1
MAX_BISECT = 34


def _score_key(s):
    bits = pltpu.bitcast(s, I32)
    return jnp.where(bits < 0, bits ^ INT_MAX, bits)


def _floor_avg(lo, hi):
    return (lo >> 1) + (hi >> 1) + (lo & hi & 1)


def _bisect_threshold(count_ge, active0):
    lo0 = jnp.full(active0.shape, KEY_MASKED + 1, I32)
    hi0 = jnp.full(active0.shape, INT_MAX, I32)

    def cond(st):
        it, _, _, active = st
        return jnp.logical_and(it < MAX_BISECT, jnp.max(active) > 0)

    def step(lo, hi, active):
        mid = _floor_avg(lo, hi)
        cnt = count_ge(mid)
        on = active > 0
        ge = cnt >= TOPK
        lo = jnp.where(jnp.logical_and(on, ge), mid, lo)
        hi = jnp.where(jnp.logical_and(on, jnp.logical_not(ge)), mid, hi)
        still = jnp.logical_and(cnt != TOPK, hi - lo > 1)
        return lo, hi, jnp.where(jnp.logical_and(on, still), 1, 0).astype(I32)

    def body(st):
        it, lo, hi, active = st
        lo, hi, active = step(*step(lo, hi, active))
        return it + 2, lo, hi, active

    _, lo, _, _ = lax.while_loop(cond, body, (jnp.int32(0), lo0, hi0, active0.astype(I32)))
    return lo


def _head_slab(x_ref, h):
    j = h // 2
    return x_ref[:, j * LANES:(j + 1) * LANES]


def _place_half(slab, src_half, dst_half):
    lane = lax.broadcasted_iota(I32, slab.shape, 1)
    x = slab if src_half == dst_half else pltpu.roll(slab, HEAD_DIM, 1)
    return jnp.where((lane >= HEAD_DIM) == (dst_half == 1), x, 0.0)


def _gather_heads(o, rows):
    outs = []
    for j in range(N_HEADS // 2):
        n = (2 * j) // GROUP
        tc, th = n // 2, n % 2
        a0 = o[(2 * j) * rows:(2 * j + 1) * rows, tc * LANES:(tc + 1) * LANES]
        a1 = o[(2 * j + 1) * rows:(2 * j + 2) * rows, tc * LANES:(tc + 1) * LANES]
        x0 = a0 if th == 0 else pltpu.roll(a0, HEAD_DIM, 1)
        x1 = a1 if th == 1 else pltpu.roll(a1, HEAD_DIM, 1)
        lane = lax.broadcasted_iota(I32, x0.shape, 1)
        outs.append(jnp.where(lane < HEAD_DIM, x0, x1))
    return outs


def _build_qbd(q_ref, qbd_ref, rows):
    for h in range(N_HEADS):
        n = h // GROUP
        tc, th = n // 2, n % 2
        placed = _place_half(_head_slab(q_ref, h) * ATTN_SCALE, h % 2, th).astype(qbd_ref.dtype)
        for c in range(D_KV // LANES):
            qbd_ref[h * rows:(h + 1) * rows, c * LANES:(c + 1) * LANES] = (
                placed if c == tc else jnp.zeros_like(placed))


def _softmax_step(a3, m_ref, l_ref, acc_ref, vc, rows, v_transposed):
    n = N_HEADS * rows
    ch = a3.shape[2]
    m_prev = m_ref[...].reshape(N_HEADS, rows, 1)
    m_new = jnp.maximum(m_prev, jnp.max(a3, axis=2, keepdims=True))
    p = jnp.exp(a3 - m_new)
    alpha = jnp.exp(m_prev - m_new)
    l_ref[...] = (alpha * l_ref[...].reshape(N_HEADS, rows, 1)
                  + jnp.sum(p, axis=2, keepdims=True)).reshape(n, 1)
    m_ref[...] = m_new.reshape(n, 1)
    pb = p.reshape(n, ch).astype(BF16)
    if v_transposed:
        pv = lax.dot_general(pb, vc, NT_DIMS, preferred_element_type=F32)
    else:
        pv = jnp.dot(pb, vc, preferred_element_type=F32)
    acc_ref[...] = alpha.reshape(n, 1) * acc_ref[...] + pv


ATT_CH = 512
CNT_ROWS = 8 * SUBLANES
N_ATT_CH = R // ATT_CH
NT_DIMS = (((1,), (1,)), ((), ()))
BOUND_MARGIN = 1.0 + 2.0 ** -5
MIN_ROW_SUM = 1e-30


def _attn_prompt_kernel(qi_ref, wi_ref, q_ref, kib_ref, kb_ref, vb_ref, o_ref, lmin_ref,
                        s_ref, st_ref, qim_ref, wb_ref, qbd_ref, m_ref, l_ref, acc_ref, p_ref,
                        kmax_ref, *, bound_max):
    i = pl.program_id(0)

    @pl.when(i >= N_QBLK)
    def _():
        o_ref[...] = jnp.zeros_like(o_ref)
        lmin_ref[...] = jnp.ones_like(lmin_ref)

    if bound_max:
        @pl.when(i == 0)
        def _():
            r_i = lax.broadcasted_iota(I32, (D_KV, LANES), 0)
            c_i = lax.broadcasted_iota(I32, (D_KV, LANES), 1)
            sel = jnp.where(r_i // HEAD_DIM == c_i, 1.0, 0.0).astype(BF16)

            def body(c, mx):
                kc = kb_ref[pl.ds(pl.multiple_of(c * ATT_CH, ATT_CH), ATT_CH), :].astype(F32)
                n2 = jnp.dot((kc * kc).astype(BF16), sel, preferred_element_type=F32)
                return jnp.maximum(mx, n2)

            mx = lax.fori_loop(0, N_ATT_CH, body, jnp.zeros((ATT_CH, LANES), F32))
            kmax = jnp.max(mx, axis=0, keepdims=True)
            for n in range(N_KV_HEADS):
                kmax_ref[n] = jnp.broadcast_to(kmax[:, n:n + 1], (QB, LANES))

    @pl.when(i < N_QBLK)
    def _():
        n_ch = (i * QB) // ATT_CH + 1
        qrow = i * QB + lax.broadcasted_iota(I32, (QB, 1), 0)

        wi = wi_ref[...]
        for h in range(N_IDX_HEADS):
            qim_ref[h * QB:(h + 1) * QB, :] = _place_half(_head_slab(qi_ref, h), h % 2, h % 2).astype(BF16)
            wb_ref[h] = jnp.broadcast_to(wi[:, h:h + 1], (QB, LANES))
        _build_qbd(q_ref, qbd_ref, QB)

        def score_chunk(c, carry):
            start = pl.multiple_of(c * ATT_CH, ATT_CH)
            kc = kib_ref[pl.ds(start, ATT_CH), :]
            acc = [jnp.zeros((QB, LANES), F32) for _ in range(ATT_CH // LANES)]
            hpd = 4
            for hg in range(N_IDX_HEADS // hpd):
                d = lax.dot_general(qim_ref[hg * hpd * QB:(hg + 1) * hpd * QB, :], kc, NT_DIMS,
                                    preferred_element_type=F32)
                for hh in range(hpd):
                    w = wb_ref[hg * hpd + hh]
                    for t in range(ATT_CH // LANES):
                        acc[t] = acc[t] + w * jnp.maximum(
                            d[hh * QB:(hh + 1) * QB, t * LANES:(t + 1) * LANES], 0.0)
            for t in range(ATT_CH // LANES):
                kpos = start + t * LANES + lax.broadcasted_iota(I32, (QB, LANES), 1)
                keys = jnp.where(kpos <= qrow, _score_key(acc[t]), KEY_MASKED)
                s_ref[c, :, t * LANES:(t + 1) * LANES] = keys
                st_ref[pl.ds(pl.multiple_of(start + t * LANES, LANES), LANES), :] = keys.T
            return carry

        lax.fori_loop(0, n_ch, score_chunk, 0)

        def count_ge(mid):
            def body(c, cnt):
                keys = st_ref[pl.ds(pl.multiple_of(c * ATT_CH, ATT_CH), ATT_CH), :]
                hit = jnp.where(keys >= mid, 1.0, 0.0)
                return cnt + jnp.sum(hit.reshape(ATT_CH // CNT_ROWS, CNT_ROWS, QB), axis=0)

            cnt = lax.fori_loop(0, n_ch, body, jnp.zeros((CNT_ROWS, QB), F32))
            return jnp.sum(cnt, axis=0, keepdims=True).astype(I32)

        qrow_l = i * QB + lax.broadcasted_iota(I32, (1, QB), 1)
        lo = _bisect_threshold(count_ge, qrow_l + 1 > TOPK)
        lo_b = jnp.broadcast_to(lo, (QB, QB)).T

        m_ref[...] = jnp.full(m_ref.shape, NEG, F32)
        l_ref[...] = jnp.zeros_like(l_ref)
        acc_ref[...] = jnp.zeros_like(acc_ref)
        n_t = ATT_CH // LANES

        def masked_logits(c):
            start = pl.multiple_of(c * ATT_CH, ATT_CH)
            bias = [jnp.where(s_ref[c, :, t * LANES:(t + 1) * LANES] >= lo_b, 0.0, NEG)
                    for t in range(n_t)]
            a = lax.dot_general(qbd_ref[...], kb_ref[pl.ds(start, ATT_CH), :], NT_DIMS,
                                preferred_element_type=F32)
            return start, [[a[h * QB:(h + 1) * QB, t * LANES:(t + 1) * LANES] + bias[t]
                            for t in range(n_t)] for h in range(N_HEADS)]

        def max_chunk(c, carry):
            _, a = masked_logits(c)
            for h in range(N_HEADS):
                rows = slice(h * QB, (h + 1) * QB)
                m = m_ref[rows, :]
                for t in range(n_t):
                    m = jnp.maximum(m, a[h][t])
                m_ref[rows, :] = m
            return carry

        if bound_max:
            for h in range(N_HEADS):
                rows = slice(h * QB, (h + 1) * QB)
                qf = qbd_ref[rows, :].astype(F32)
                qn2 = jnp.sum(qf * qf, axis=1, keepdims=True)
                m_ref[rows, :] = jnp.sqrt(jnp.broadcast_to(qn2, (QB, LANES))
                                          * kmax_ref[h // GROUP]) * BOUND_MARGIN
        else:
            lax.fori_loop(0, n_ch, max_chunk, 0)
            m_ref[...] = jnp.broadcast_to(jnp.max(m_ref[...], axis=1, keepdims=True), m_ref.shape)

        def value_chunk(c, carry):
            start, a = masked_logits(c)
            for h in range(N_HEADS):
                rows = slice(h * QB, (h + 1) * QB)
                m = m_ref[rows, :]
                l = l_ref[rows, :]
                for t in range(n_t):
                    p = jnp.exp(a[h][t] - m)
                    l = l + p
                    p_ref[rows, t * LANES:(t + 1) * LANES] = p.astype(BF16)
                l_ref[rows, :] = l
            acc_ref[...] += jnp.dot(p_ref[...], vb_ref[pl.ds(start, ATT_CH), :],
                                    preferred_element_type=F32)
            return carry

        lax.fori_loop(0, n_ch, value_chunk, 0)

        l = jnp.sum(l_ref[...], axis=1, keepdims=True)
        lmin_ref[...] = jnp.broadcast_to(jnp.min(l, axis=0, keepdims=True), lmin_ref.shape)
        o = acc_ref[...] / l
        for j, slab in enumerate(_gather_heads(o, QB)):
            o_ref[:, j * LANES:(j + 1) * LANES] = slab.astype(BF16)


def _attn_prompt(qi, wi, q, kib, kb, vb, bound_max):
    row = lambda i: (i, 0)
    full = lambda i: (0, 0)
    nrow = N_HEADS * QB
    return pl.pallas_call(
        functools.partial(_attn_prompt_kernel, bound_max=bound_max),
        grid=(R // QB,),
        in_specs=[
            pl.BlockSpec((QB, N_IDX_HEADS * IDX_DIM), row),
            pl.BlockSpec((QB, LANES), row),
            pl.BlockSpec((QB, D_ATTN), row),
            pl.BlockSpec((R, LANES), full),
            pl.BlockSpec((R, D_KV), full),
            pl.BlockSpec((R, D_KV), full),
        ],
        out_specs=[pl.BlockSpec((QB, D_ATTN), row),
                   pl.BlockSpec((1, SUBLANES, LANES), lambda i: (i, 0, 0))],
        out_shape=(jax.ShapeDtypeStruct((R, D_ATTN), BF16),
                   jax.ShapeDtypeStruct((R // QB, SUBLANES, LANES), F32)),
        scratch_shapes=[
            pltpu.VMEM((N_ATT_CH, QB, ATT_CH), I32),
            pltpu.VMEM((R, QB), I32),
            pltpu.VMEM((nrow, LANES), BF16),
            pltpu.VMEM((N_IDX_HEADS, QB, LANES), F32),
            pltpu.VMEM((nrow, D_KV), BF16),
            pltpu.VMEM((nrow, LANES), F32),
            pltpu.VMEM((nrow, LANES), F32),
            pltpu.VMEM((nrow, D_KV), F32),
            pltpu.VMEM((nrow, ATT_CH), BF16),
            pltpu.VMEM((N_KV_HEADS, QB, LANES), F32),
        ],
        compiler_params=_cparams(("arbitrary",)),
        name="attn_prompt_bound" if bound_max else "attn_prompt",
    )(qi, wi, q, kib, kb, vb)


S_CH = 2048
S_NCH = PAST_LEN // S_CH
S_PPC = S_CH // PAGE_SIZE
S_ROWS = N_HEADS * DEC_SEQ


def _page_lanes(p):
    return pl.ds(pl.multiple_of(p * PAGE_SIZE, PAGE_SIZE), PAGE_SIZE)


def _kidx_page_copy(pt_ref, kidx_hbm, kbuf, sem, b, p, slot):
    return pltpu.make_async_copy(
        kidx_hbm.at[pt_ref[b, p]], kbuf.at[slot, :, _page_lanes(p)], sem.at[slot])


def _sample_score_kernel(pt_ref, qi_ref, wi_ref, kin_ref, kidx_hbm, s_ref, sn_ref,
                         kbuf, sem, knp_ref):
    b = pl.program_id(0)
    nb = pl.num_programs(0)
    slot = b % 2

    def start_batch(bb, sl):
        def body(p, c):
            _kidx_page_copy(pt_ref, kidx_hbm, kbuf, sem, bb, p, sl).start()
            return c
        lax.fori_loop(0, N_PAGES, body, 0)

    @pl.when(b == 0)
    def _():
        start_batch(0, 0)

    @pl.when(b + 1 < nb)
    def _():
        start_batch(b + 1, 1 - slot)

    def wait_body(p, c):
        _kidx_page_copy(pt_ref, kidx_hbm, kbuf, sem, b, p, slot).wait()
        return c
    lax.fori_loop(0, N_PAGES, wait_body, 0)

    qi = qi_ref[0]
    qis = jnp.concatenate(
        [qi[:, h * IDX_DIM:(h + 1) * IDX_DIM] for h in range(N_IDX_HEADS)], axis=0).astype(BF16)
    wi = wi_ref[0]
    wb = [jnp.broadcast_to(wi[:, h:h + 1], (DEC_SEQ, LANES)) for h in range(N_IDX_HEADS)]

    def head_sum(d, width):
        outs = []
        for t in range(width // LANES):
            acc = jnp.zeros((DEC_SEQ, LANES), F32)
            for h in range(N_IDX_HEADS):
                acc = acc + wb[h] * jnp.maximum(
                    d[h * DEC_SEQ:(h + 1) * DEC_SEQ, t * LANES:(t + 1) * LANES], 0.0)
            outs.append(acc)
        return outs

    def score_chunk(c, carry):
        kc = kbuf[slot, :, pl.ds(pl.multiple_of(c * S_CH, S_CH), S_CH)].astype(BF16)
        d = jnp.dot(qis, kc, preferred_element_type=F32)
        for t, acc in enumerate(head_sum(d, S_CH)):
            s_ref[c, :, t * LANES:(t + 1) * LANES] = _score_key(acc)
        return carry
    lax.fori_loop(0, S_NCH, score_chunk, 0)

    knp_ref[...] = jnp.zeros_like(knp_ref)
    knp_ref[:DEC_SEQ, :] = kin_ref[0][:, :IDX_DIM]
    d = lax.dot_general(qis, knp_ref[...].astype(BF16), NT_DIMS, preferred_element_type=F32)
    kpos = lax.broadcasted_iota(I32, (DEC_SEQ, LANES), 1)
    srow = lax.broadcasted_iota(I32, (DEC_SEQ, LANES), 0)
    sn_ref[...] = jnp.where(kpos <= srow, _score_key(head_sum(d, LANES)[0]), KEY_MASKED)


def _sample_scores(page_table, qi_s, wi_s, ki_s, cache_kidx):
    blk = lambda b, pt: (b, 0, 0)
    grid_spec = pltpu.PrefetchScalarGridSpec(
        num_scalar_prefetch=1,
        grid=(DEC_BATCH,),
        in_specs=[
            pl.BlockSpec((1, DEC_SEQ, N_IDX_HEADS * IDX_DIM), blk),
            pl.BlockSpec((1, DEC_SEQ, LANES), blk),
            pl.BlockSpec((1, DEC_SEQ, LANES), blk),
            pl.BlockSpec(memory_space=pl.ANY),
        ],
        out_specs=[
            pl.BlockSpec((S_NCH, DEC_SEQ, S_CH), lambda b, pt: (0, b, 0)),
            pl.BlockSpec((DEC_SEQ, LANES), lambda b, pt: (b, 0)),
        ],
        scratch_shapes=[
            pltpu.VMEM((2, IDX_DIM, PAST_LEN), F32),
            pltpu.SemaphoreType.DMA((2,)),
            pltpu.VMEM((LANES, IDX_DIM), F32),
        ],
    )
    return pl.pallas_call(
        _sample_score_kernel,
        grid_spec=grid_spec,
        out_shape=(
            jax.ShapeDtypeStruct((S_NCH, NS, S_CH), I32),
            jax.ShapeDtypeStruct((NS, LANES), I32),
        ),
        compiler_params=_cparams(("arbitrary",)),
        name="sample_scores",
    )(page_table, qi_s, wi_s, ki_s, cache_kidx)


def _sample_threshold_kernel(s_ref, sn_ref, bp_ref, bn_ref):
    def count_ge(mid):
        def body(c, cnt):
            hit = jnp.where(s_ref[c] >= mid, 1.0, 0.0)
            for t in range(S_CH // LANES):
                cnt = cnt + hit[:, t * LANES:(t + 1) * LANES]
            return cnt

        cnt = lax.fori_loop(0, S_NCH, body, jnp.where(sn_ref[...] >= mid, 1.0, 0.0))
        return jnp.sum(cnt, axis=1, keepdims=True).astype(I32)

    lo = _bisect_threshold(count_ge, jnp.ones((NS, 1), I32) > 0)
    for c in range(S_NCH):
        bp_ref[c] = jnp.where(s_ref[c] >= lo, 0.0, NEG)
    bn_ref[...] = jnp.where(sn_ref[...] >= lo, 0.0, NEG)


def _sample_threshold(keys_past, keys_new):
    vmem = pl.BlockSpec(memory_space=pltpu.VMEM)
    return pl.pallas_call(
        _sample_threshold_kernel,
        in_specs=[vmem, vmem],
        out_specs=[vmem, vmem],
        out_shape=(
            jax.ShapeDtypeStruct((S_NCH, NS, S_CH), F32),
            jax.ShapeDtypeStruct((NS, LANES), F32),
        ),
        compiler_params=pltpu.CompilerParams(vmem_limit_bytes=VMEM_LIMIT),
        name="sample_threshold",
    )(keys_past, keys_new)


def _kv_page_copies(pt_ref, ck_hbm, cv_hbm, kbuf, vbuf, sem, b, c, p, slot):
    page = pt_ref[b, c * S_PPC + p]
    dst = _page_lanes(p)
    return (pltpu.make_async_copy(ck_hbm.at[page], kbuf.at[slot, :, dst], sem.at[0, slot]),
            pltpu.make_async_copy(cv_hbm.at[page], vbuf.at[slot, :, dst], sem.at[1, slot]))


def _sample_attn_kernel(pt_ref, q_ref, bp_ref, bn_ref, kn_ref, vn_ref, ck_hbm, cv_hbm, o_ref,
                        kbuf, vbuf, sem, qbd_ref, m_ref, l_ref, acc_ref, knp_ref, vnp_ref):
    b = pl.program_id(0)
    c = pl.program_id(1)
    step = b * S_NCH + c
    n_steps = pl.num_programs(0) * S_NCH
    slot = step % 2

    def start_chunk(st, sl):
        bb = st // S_NCH
        cc = st % S_NCH

        def body(p, carry):
            for cp in _kv_page_copies(pt_ref, ck_hbm, cv_hbm, kbuf, vbuf, sem, bb, cc, p, sl):
                cp.start()
            return carry
        lax.fori_loop(0, S_PPC, body, 0)

    @pl.when(step == 0)
    def _():
        start_chunk(0, 0)

    @pl.when(step + 1 < n_steps)
    def _():
        start_chunk(step + 1, 1 - slot)

    def wait_body(p, carry):
        for cp in _kv_page_copies(pt_ref, ck_hbm, cv_hbm, kbuf, vbuf, sem, b, c, p, slot):
            cp.wait()
        return carry
    lax.fori_loop(0, S_PPC, wait_body, 0)

    @pl.when(c == 0)
    def _():
        _build_qbd(q_ref.at[0], qbd_ref, DEC_SEQ)
        m_ref[...] = jnp.full(m_ref.shape, NEG, F32)
        l_ref[...] = jnp.zeros_like(l_ref)
        acc_ref[...] = jnp.zeros_like(acc_ref)

    def update(kc, vc, bias, feature_major):
        qbd = qbd_ref[...].astype(BF16)
        if feature_major:
            a = jnp.dot(qbd, kc, preferred_element_type=F32)
        else:
            a = lax.dot_general(qbd, kc, NT_DIMS, preferred_element_type=F32)
        a3 = a.reshape(N_HEADS, DEC_SEQ, a.shape[1]) + bias[None]
        _softmax_step(a3, m_ref, l_ref, acc_ref, vc, DEC_SEQ, feature_major)

    update(kbuf[slot].astype(BF16), vbuf[slot].astype(BF16), bp_ref[0], True)

    @pl.when(c == S_NCH - 1)
    def _():
        knp_ref[...] = jnp.zeros_like(knp_ref)
        vnp_ref[...] = jnp.zeros_like(vnp_ref)
        knp_ref[:DEC_SEQ, :] = kn_ref[0]
        vnp_ref[:DEC_SEQ, :] = vn_ref[0]
        update(knp_ref[...].astype(BF16), vnp_ref[...].astype(BF16), bn_ref[...], False)
        o = acc_ref[...] / l_ref[...]
        for j, slab in enumerate(_gather_heads(o, DEC_SEQ)):
            o_ref[0, :, j * LANES:(j + 1) * LANES] = slab


def _sample_attn(page_table, q_s, bias_past, bias_new, k_s, v_s, cache_k, cache_v):
    blk = lambda b, c, pt: (b, 0, 0)
    grid_spec = pltpu.PrefetchScalarGridSpec(
        num_scalar_prefetch=1,
        grid=(DEC_BATCH, S_NCH),
        in_specs=[
            pl.BlockSpec((1, DEC_SEQ, D_ATTN), blk),
            pl.BlockSpec((1, DEC_SEQ, S_CH), lambda b, c, pt: (c, b, 0)),
            pl.BlockSpec((DEC_SEQ, LANES), lambda b, c, pt: (b, 0)),
            pl.BlockSpec((1, DEC_SEQ, D_KV), blk),
            pl.BlockSpec((1, DEC_SEQ, D_KV), blk),
            pl.BlockSpec(memory_space=pl.ANY),
            pl.BlockSpec(memory_space=pl.ANY),
        ],
        out_specs=pl.BlockSpec((1, DEC_SEQ, D_ATTN), blk),
        scratch_shapes=[
            pltpu.VMEM((2, D_KV, S_CH), F32),
            pltpu.VMEM((2, D_KV, S_CH), F32),
            pltpu.SemaphoreType.DMA((2, 2)),
            pltpu.VMEM((S_ROWS, D_KV), F32),
            pltpu.VMEM((S_ROWS, 1), F32),
            pltpu.VMEM((S_ROWS, 1), F32),
            pltpu.VMEM((S_ROWS, D_KV), F32),
            pltpu.VMEM((LANES, D_KV), F32),
            pltpu.VMEM((LANES, D_KV), F32),
        ],
    )
    return pl.pallas_call(
        _sample_attn_kernel,
        grid_spec=grid_spec,
        out_shape=jax.ShapeDtypeStruct((DEC_BATCH, DEC_SEQ, D_ATTN), F32),
        compiler_params=_cparams(("arbitrary", "arbitrary")),
        name="sample_attn",
    )(page_table, q_s, bias_past, bias_new, k_s, v_s, cache_k, cache_v)


def _rope_tables():
    pos = np.zeros((R,), np.float32)
    pos[:T] = np.arange(T)
    pos[ROW_S:ROW_S + NS] = np.tile(PAST_LEN + np.arange(DEC_SEQ), DEC_BATCH)
    half = HEAD_DIM // 2
    inv_freq = ROPE_THETA ** (-(jnp.arange(half, dtype=F32) * 2.0 / HEAD_DIM))
    ang = jnp.asarray(pos)[:, None] * inv_freq
    cos = jnp.tile(jnp.cos(ang), (1, LANES // half))
    sin = jnp.sin(ang)
    sin = jnp.tile(jnp.concatenate([-sin, sin], axis=1), (1, LANES // HEAD_DIM))
    return cos, sin


def kernel(x_prompt, x_sample, cache_k, cache_v, cache_kidx, state_pool, page_table, meta_tokens,
           norm_mix_pre, w_in, idx_k_norm, w_pool, pool_scale, w_attn_o, w_out, norm_mix_post,
           norm_ffn_pre, w_up, w_down, norm_ffn_post):
    x_all = jnp.concatenate([
        meta_tokens, x_prompt[0], jnp.zeros((ROW_S - T, D_MODEL), F32),
        x_sample.reshape(NS, D_MODEL), jnp.zeros((R - ROW_S - NS, D_MODEL), F32)], axis=0)

    w = w_in[0]
    o_u, o_q, o_k, o_v, o_qi, o_ki, o_wi, o_gp = np.cumsum((0, D_POOL, D_ATTN, D_KV, D_KV,
                                                            N_IDX_HEADS * IDX_DIM, IDX_DIM, N_IDX_HEADS))
    n_kiwi = IDX_DIM + N_IDX_HEADS
    w_a = jnp.concatenate([
        w[:, o_q:o_v], w[:, o_qi:o_gp],
        jnp.zeros((D_MODEL, W_A_COLS - D_ATTN - D_KV - N_IDX_HEADS * IDX_DIM - n_kiwi), F32)],
        axis=1).astype(BF16)
    w_b = jnp.concatenate([w[:, o_v:o_qi], w[:, o_u:o_q]], axis=1).astype(BF16)
    w_c = w[:, o_gp:].astype(BF16)

    g_pre = norm_mix_pre[0][None]
    cos, sin = _rope_tables()
    gk = jnp.tile(idx_k_norm[0], LANES // IDX_DIM)[None]

    q, k, kb, qi, ki, kib, wi = _proj_a(x_all, g_pre, w_a, cos, sin, gk)
    v, vb, u = _proj_b(x_all, g_pre, w_b)
    gates = _proj_c(x_all, g_pre, w_c)

    def sample_rows(a):
        return a[ROW_S:ROW_S + NS].reshape(DEC_BATCH, DEC_SEQ, a.shape[1])

    pooled = _pool_prompt(u)
    u_s = sample_rows(u)
    state = state_pool[0]
    state16 = jnp.pad(state, ((0, 0), (HALO - POOL_BUF, 0), (0, 0)))
    pooled_s = _pool_sample(u_s, state16)
    pooled = lax.dynamic_update_slice(pooled, pooled_s.reshape(NS, D_POOL), (ROW_S, 0))

    attn, lmin = _attn_prompt(qi, wi, q, kib, kb, vb, bound_max=True)
    attn = lax.cond(jnp.min(lmin) >= MIN_ROW_SUM, lambda: attn,
                    lambda: _attn_prompt(qi, wi, q, kib, kb, vb, bound_max=False)[0])
    k_s, v_s = sample_rows(k), sample_rows(v)
    n_pool = cache_k.shape[1]
    kidx_fm = jnp.transpose(cache_kidx[0], (0, 2, 1))
    k_fm = jnp.transpose(cache_k[0], (0, 2, 3, 1)).reshape(n_pool, D_KV, PAGE_SIZE)
    v_fm = jnp.transpose(cache_v[0], (0, 2, 3, 1)).reshape(n_pool, D_KV, PAGE_SIZE)
    keys_past, keys_new = _sample_scores(page_table, sample_rows(qi), sample_rows(wi), sample_rows(ki),
                                         kidx_fm)
    bias_past, bias_new = _sample_threshold(keys_past, keys_new)
    attn_s = _sample_attn(page_table, sample_rows(q), bias_past, bias_new, k_s, v_s, k_fm, v_fm)
    attn = lax.dynamic_update_slice(attn, attn_s.reshape(NS, D_ATTN).astype(BF16), (ROW_S, 0))

    x1 = _merge(x_all, pooled, attn, gates, w_pool[0].astype(BF16), pool_scale[0][None],
                w_attn_o[0].astype(BF16), w_out[0].astype(BF16), norm_mix_post[0][None])
    y = _ffn(x1, norm_ffn_pre[0][None], w_up[0].astype(BF16), w_down[0].astype(BF16),
             norm_ffn_post[0][None])

    y_prompt = y[N_META:T][None]
    y_sample = y[ROW_S:ROW_S + NS].reshape(DEC_BATCH, DEC_SEQ, D_MODEL)
    kv_shape = (1, 1, T, N_KV_HEADS, HEAD_DIM)
    kv_s_shape = (1, DEC_BATCH, DEC_SEQ, N_KV_HEADS, HEAD_DIM)
    return (
        y_prompt, y_sample,
        k[:T].reshape(kv_shape), v[:T].reshape(kv_shape), ki[:T, :IDX_DIM][None, None],
        u[T - POOL_BUF:T][None, None],
        k_s.reshape(kv_s_shape), v_s.reshape(kv_s_shape), sample_rows(ki)[None, :, :, :IDX_DIM],
        jnp.concatenate([state[:, DEC_SEQ:], u_s], axis=1)[None],
    )
```

```python
import functools

import jax
import jax.numpy as jnp
import numpy as np
from jax import lax
from jax.experimental import pallas as pl
from jax.experimental.pallas import tpu as pltpu

F32 = jnp.float32
BF16 = jnp.bfloat16
I32 = jnp.int32

D_MODEL = 2048
SEQ = 8192
DEC_BATCH = 32
DEC_SEQ = 8
PAST_LEN = 16384
PAGE_SIZE = 128
N_PAGES = PAST_LEN // PAGE_SIZE
N_META = 16
N_HEADS = 16
N_KV_HEADS = 4
HEAD_DIM = 64
GROUP = N_HEADS // N_KV_HEADS
D_ATTN = N_HEADS * HEAD_DIM
D_KV = N_KV_HEADS * HEAD_DIM
ATTN_SCALE = HEAD_DIM ** -0.5
N_IDX_HEADS = 16
IDX_DIM = 64
INDEX_W_SCALE = (N_IDX_HEADS ** -0.5) * (IDX_DIM ** -0.5)
TOPK = 256
POOL_WINDOWS = (2, 4, 8, 16)
N_POOL_GROUPS = 4
D_POOL = D_MODEL // 2
POOL_GROUP = D_POOL // N_POOL_GROUPS
POOL_OUT_GROUP = D_MODEL // N_POOL_GROUPS
POOL_BUF = max(POOL_WINDOWS) - 1
D_FF = 4 * D_MODEL
ROPE_THETA = 10000.0
EPS = 1e-6

LANES = 128
SUBLANES = 8
T = SEQ + N_META
QB = 128
N_QBLK = -(-T // QB)
ROW_S = N_QBLK * QB
NS = DEC_BATCH * DEC_SEQ
R = 8704
VMEM_LIMIT = 56 * 1024 * 1024

NEG = -1e30


def _cparams(sem):
    return pltpu.CompilerParams(dimension_semantics=sem, vmem_limit_bytes=VMEM_LIMIT)


def _rms(x, g):
    return x * lax.rsqrt(jnp.mean(x * x, axis=-1, keepdims=True) + EPS) * g


def _swap_halves(x):
    lane = lax.broadcasted_iota(I32, x.shape, 1)
    return jnp.where(lane % HEAD_DIM < HEAD_DIM // 2,
                     pltpu.roll(x, LANES - HEAD_DIM // 2, 1),
                     pltpu.roll(x, HEAD_DIM // 2, 1))


def _rope_cols(x, cos, sin):
    outs = []
    for c in range(x.shape[1] // LANES):
        xc = x[:, c * LANES:(c + 1) * LANES]
        outs.append(xc * cos + _swap_halves(xc) * sin)
    return outs


PROJ_TM = 544
N_ROPE_A = D_ATTN + D_KV
PROJ_TN_A = N_ROPE_A
W_A_COLS = 2 * PROJ_TN_A
W_B_COLS = D_KV + D_POOL
W_C_COLS = 2 * D_MODEL
PROJ_TN_C = 2048
W_ROW_B = 0
W_ROW_A = W_B_COLS
W_ROW_C = 2 * PROJ_TN_C
W_ROWS = W_ROW_C + W_C_COLS
assert W_ROW_A % PROJ_TN_A == 0 and W_ROW_A + W_A_COLS <= W_ROW_C


def _norm_to_scratch(x_ref, g_ref, xn_ref):
    @pl.when(pl.program_id(1) == 0)
    def _():
        xn_ref[...] = _rms(x_ref[...], g_ref[...]).astype(BF16)


def _proj_dot(xn_ref, w_ref):
    return lax.dot_general(xn_ref[...], w_ref[...], (((1,), (1,)), ((), ())),
                           preferred_element_type=F32)


def _proj_a_kernel(x_ref, g_ref, w_ref, cos_ref, sin_ref, gk_ref,
                   q_ref, k_ref, kb_ref, qi_ref, ki_ref, kib_ref, wi_ref, xn_ref):
    _norm_to_scratch(x_ref, g_ref, xn_ref)
    j = pl.program_id(1)
    p = _proj_dot(xn_ref, w_ref)
    cos = cos_ref[...]
    sin = sin_ref[...]

    @pl.when(j == 0)
    def _():
        cols = _rope_cols(p, cos, sin)
        nq = D_ATTN // LANES
        for c in range(nq):
            q_ref[:, c * LANES:(c + 1) * LANES] = cols[c]
        for c in range(D_KV // LANES):
            k_ref[:, c * LANES:(c + 1) * LANES] = cols[nq + c]
            kb_ref[:, c * LANES:(c + 1) * LANES] = cols[nq + c].astype(BF16)

    @pl.when(j == 1)
    def _():
        nqi = N_IDX_HEADS * IDX_DIM // LANES
        cols = _rope_cols(p[:, :nqi * LANES], cos, sin)
        for c in range(nqi):
            qi_ref[:, c * LANES:(c + 1) * LANES] = cols[c]
        slab = p[:, nqi * LANES:(nqi + 1) * LANES]
        lane = lax.broadcasted_iota(I32, slab.shape, 1)
        is_ki = lane < IDX_DIM
        ms = jnp.sum(jnp.where(is_ki, slab * slab, 0.0), axis=-1, keepdims=True) / IDX_DIM
        kin = slab * lax.rsqrt(ms + EPS) * gk_ref[...]
        kin = kin * cos + _swap_halves(kin) * sin
        ki2 = jnp.where(is_ki, kin, pltpu.roll(kin, IDX_DIM, 1))
        ki_ref[...] = ki2
        kib_ref[...] = ki2.astype(BF16)
        wi_ref[...] = pltpu.roll(slab, LANES - IDX_DIM, 1) * INDEX_W_SCALE


def _proj_a(x_all, g, w_a, cos, sin, gk):
    n_i = R // PROJ_TM
    row = lambda i, j: (i, 0)
    outs = (
        jax.ShapeDtypeStruct((R, D_ATTN), F32),
        jax.ShapeDtypeStruct((R, D_KV), F32),
        jax.ShapeDtypeStruct((R, D_KV), BF16),
        jax.ShapeDtypeStruct((R, N_IDX_HEADS * IDX_DIM), F32),
        jax.ShapeDtypeStruct((R, LANES), F32),
        jax.ShapeDtypeStruct((R, LANES), BF16),
        jax.ShapeDtypeStruct((R, LANES), F32),
    )
    return pl.pallas_call(
        _proj_a_kernel,
        grid=(n_i, 2),
        in_specs=[
            pl.BlockSpec((PROJ_TM, D_MODEL), row),
            pl.BlockSpec((1, D_MODEL), lambda i, j: (0, 0)),
            pl.BlockSpec((PROJ_TN_A, D_MODEL), lambda i, j: (W_ROW_A // PROJ_TN_A + j, 0)),
            pl.BlockSpec((PROJ_TM, LANES), row),
            pl.BlockSpec((PROJ_TM, LANES), row),
            pl.BlockSpec((1, LANES), lambda i, j: (0, 0)),
        ],
        out_specs=[pl.BlockSpec((PROJ_TM, o.shape[1]), row) for o in outs],
        out_shape=outs,
        scratch_shapes=[pltpu.VMEM((PROJ_TM, D_MODEL), BF16)],
        compiler_params=_cparams(("arbitrary", "arbitrary")),
        name="proj_rope",
    )(x_all, g, w_a, cos, sin, gk)


def _proj_b_kernel(x_ref, g_ref, w_ref, v_ref, vb_ref, u_ref, xn_ref):
    _norm_to_scratch(x_ref, g_ref, xn_ref)
    p = _proj_dot(xn_ref, w_ref)
    v_ref[...] = p[:, :D_KV]
    vb_ref[...] = p[:, :D_KV].astype(BF16)
    u_ref[...] = p[:, D_KV:]


def _proj_b(x_all, g, w_b):
    n_i = R // PROJ_TM
    row = lambda i, j: (i, 0)
    outs = (
        jax.ShapeDtypeStruct((R, D_KV), F32),
        jax.ShapeDtypeStruct((R, D_KV), BF16),
        jax.ShapeDtypeStruct((R, D_POOL), F32),
    )
    return pl.pallas_call(
        _proj_b_kernel,
        grid=(n_i, 1),
        in_specs=[
            pl.BlockSpec((PROJ_TM, D_MODEL), row),
            pl.BlockSpec((1, D_MODEL), lambda i, j: (0, 0)),
            pl.BlockSpec((W_B_COLS, D_MODEL), lambda i, j: (W_ROW_B // W_B_COLS, 0)),
        ],
        out_specs=[pl.BlockSpec((PROJ_TM, o.shape[1]), row) for o in outs],
        out_shape=outs,
        scratch_shapes=[pltpu.VMEM((PROJ_TM, D_MODEL), BF16)],
        compiler_params=_cparams(("arbitrary", "arbitrary")),
        name="proj_vu",
    )(x_all, g, w_b)


def _proj_c_kernel(x_ref, g_ref, w_ref, o_ref, xn_ref):
    _norm_to_scratch(x_ref, g_ref, xn_ref)
    o_ref[...] = _proj_dot(xn_ref, w_ref)


def _proj_c(x_all, g, w_c):
    n_i = R // PROJ_TM
    return pl.pallas_call(
        _proj_c_kernel,
        grid=(n_i, W_C_COLS // PROJ_TN_C),
        in_specs=[
            pl.BlockSpec((PROJ_TM, D_MODEL), lambda i, j: (i, 0)),
            pl.BlockSpec((1, D_MODEL), lambda i, j: (0, 0)),
            pl.BlockSpec((PROJ_TN_C, D_MODEL), lambda i, j: (W_ROW_C // PROJ_TN_C + j, 0)),
        ],
        out_specs=pl.BlockSpec((PROJ_TM, PROJ_TN_C), lambda i, j: (i, j)),
        out_shape=jax.ShapeDtypeStruct((R, W_C_COLS), F32),
        scratch_shapes=[pltpu.VMEM((PROJ_TM, D_MODEL), BF16)],
        compiler_params=_cparams(("arbitrary", "arbitrary")),
        name="proj_gates",
    )(x_all, g, w_c)


POOL_TM = 512
HALO = 16


def _window_mean_minus_cur(ext_ref, rows, inv_cnt):
    outs = []
    for g, w in enumerate(POOL_WINDOWS):
        cols = slice(g * POOL_GROUP, (g + 1) * POOL_GROUP)
        cur = ext_ref[HALO:HALO + rows, cols]
        acc = cur
        for d in range(1, w):
            acc = acc + ext_ref[HALO - d:HALO - d + rows, cols]
        outs.append(acc * inv_cnt[g] - cur)
    return outs


def _pool_prompt_kernel(u_ref, halo_ref, o_ref, ext_ref):
    i = pl.program_id(0)
    ext_ref[HALO:, :] = u_ref[...]
    ext_ref[:HALO, :] = jnp.where(i == 0, 0.0, halo_ref[...])
    pos = i * POOL_TM + lax.broadcasted_iota(I32, (POOL_TM, 1), 0)
    inv_cnt = [1.0 / jnp.minimum(pos + 1, w).astype(F32) for w in POOL_WINDOWS]
    outs = _window_mean_minus_cur(ext_ref, POOL_TM, inv_cnt)
    for g in range(N_POOL_GROUPS):
        o_ref[:, g * POOL_GROUP:(g + 1) * POOL_GROUP] = outs[g]


def _pool_prompt(u):
    per = POOL_TM // HALO
    return pl.pallas_call(
        _pool_prompt_kernel,
        grid=(R // POOL_TM,),
        in_specs=[
            pl.BlockSpec((POOL_TM, D_POOL), lambda i: (i, 0)),
            pl.BlockSpec((HALO, D_POOL), lambda i: (jnp.maximum(i * per - 1, 0), 0)),
        ],
        out_specs=pl.BlockSpec((POOL_TM, D_POOL), lambda i: (i, 0)),
        out_shape=jax.ShapeDtypeStruct((R, D_POOL), F32),
        scratch_shapes=[pltpu.VMEM((POOL_TM + HALO, D_POOL), F32)],
        compiler_params=_cparams(("arbitrary",)),
        name="pool_prompt",
    )(u, u)


def _pool_sample_kernel(u_ref, st_ref, o_ref, ext_ref):
    for b in range(DEC_BATCH):
        ext_ref[:HALO, :] = st_ref[b]
        ext_ref[HALO:, :] = u_ref[b]
        inv_cnt = [1.0 / w for w in POOL_WINDOWS]
        outs = _window_mean_minus_cur(ext_ref, DEC_SEQ, inv_cnt)
        for g in range(N_POOL_GROUPS):
            o_ref[b, :, g * POOL_GROUP:(g + 1) * POOL_GROUP] = outs[g]


def _pool_sample(u_s, state16):
    return pl.pallas_call(
        _pool_sample_kernel,
        out_shape=jax.ShapeDtypeStruct((DEC_BATCH, DEC_SEQ, D_POOL), F32),
        scratch_shapes=[pltpu.VMEM((HALO + DEC_SEQ, D_POOL), F32)],
        compiler_params=pltpu.CompilerParams(vmem_limit_bytes=VMEM_LIMIT),
        name="pool_sample",
    )(u_s, state16)


MERGE_TM = 256


def _merge_kernel(x_ref, pooled_ref, attn_ref, gate_ref, wp_ref, ps_ref, wa_ref, wo_ref, gn_ref, o_ref):
    pooled = pooled_ref[...].astype(BF16)
    pool_out = jnp.concatenate(
        [jnp.dot(pooled[:, g * POOL_GROUP:(g + 1) * POOL_GROUP], wp_ref[g], preferred_element_type=F32)
         for g in range(N_POOL_GROUPS)], axis=1) * ps_ref[...]
    attn_out = jnp.dot(attn_ref[...], wa_ref[...], preferred_element_type=F32)
    gate = gate_ref[...]
    m = (jax.nn.sigmoid(gate[:, :D_MODEL]) * pool_out
         + jax.nn.sigmoid(gate[:, D_MODEL:]) * attn_out)
    mix = jnp.dot(m.astype(BF16), wo_ref[...], preferred_element_type=F32)
    o_ref[...] = x_ref[...] + _rms(mix, gn_ref[...])


def _merge(x_all, pooled, attn, gates, w_pool, pool_scale, w_attn_o, w_out, g_post):
    row = lambda i: (i, 0)
    const2 = lambda i: (0, 0)
    return pl.pallas_call(
        _merge_kernel,
        grid=(R // MERGE_TM,),
        in_specs=[
            pl.BlockSpec((MERGE_TM, D_MODEL), row),
            pl.BlockSpec((MERGE_TM, D_POOL), row),
            pl.BlockSpec((MERGE_TM, D_ATTN), row),
            pl.BlockSpec((MERGE_TM, 2 * D_MODEL), row),
            pl.BlockSpec((N_POOL_GROUPS, POOL_GROUP, POOL_OUT_GROUP), lambda i: (0, 0, 0)),
            pl.BlockSpec((1, D_MODEL), const2),
            pl.BlockSpec((D_ATTN, D_MODEL), const2),
            pl.BlockSpec((D_MODEL, D_MODEL), const2),
            pl.BlockSpec((1, D_MODEL), const2),
        ],
        out_specs=pl.BlockSpec((MERGE_TM, D_MODEL), row),
        out_shape=jax.ShapeDtypeStruct((R, D_MODEL), F32),
        compiler_params=_cparams(("arbitrary",)),
        name="merge",
    )(x_all, pooled, attn, gates, w_pool, pool_scale, w_attn_o, w_out, g_post)


FFN_TM = 544
FFN_TF = 1024


def _ffn_kernel(x_ref, gpre_ref, wu_ref, wd_ref, gpost_ref, o_ref, h_ref, acc_ref):
    j = pl.program_id(1)

    @pl.when(j == 0)
    def _():
        h_ref[...] = _rms(x_ref[...], gpre_ref[...]).astype(BF16)
        acc_ref[...] = jnp.zeros_like(acc_ref)

    a = jnp.maximum(jnp.dot(h_ref[...], wu_ref[...], preferred_element_type=F32), 0.0)
    acc_ref[...] += jnp.dot((a * a).astype(BF16), wd_ref[...], preferred_element_type=F32)

    @pl.when(j == pl.num_programs(1) - 1)
    def _():
        o_ref[...] = x_ref[...] + _rms(acc_ref[...], gpost_ref[...])


def _ffn(x1, g_pre, w_up, w_down, g_post):
    return pl.pallas_call(
        _ffn_kernel,
        grid=(R // FFN_TM, D_FF // FFN_TF),
        in_specs=[
            pl.BlockSpec((FFN_TM, D_MODEL), lambda i, j: (i, 0)),
            pl.BlockSpec((1, D_MODEL), lambda i, j: (0, 0)),
            pl.BlockSpec((D_MODEL, FFN_TF), lambda i, j: (0, j)),
            pl.BlockSpec((FFN_TF, D_MODEL), lambda i, j: (j, 0)),
            pl.BlockSpec((1, D_MODEL), lambda i, j: (0, 0)),
        ],
        out_specs=pl.BlockSpec((FFN_TM, D_MODEL), lambda i, j: (i, 0)),
        out_shape=jax.ShapeDtypeStruct((R, D_MODEL), F32),
        scratch_shapes=[pltpu.VMEM((FFN_TM, D_MODEL), BF16), pltpu.VMEM((FFN_TM, D_MODEL), F32)],
        compiler_params=_cparams(("arbitrary", "arbitrary")),
        name="ffn",
    )(x1, g_pre, w_up, w_down, g_post)


NEG_INF = float("-inf")
MAX_BISECT = 192
STEPS_PER_TRIP = 3


def _above(x):
    return x + (jnp.abs(x) * 2.0 ** -20 + 1e-30)


def _bisect_threshold(count_ge, smin, smax, active0):
    def cond(st):
        it, _, _, active = st
        return jnp.logical_and(it < MAX_BISECT, jnp.max(active) > 0)

    def step(lo, hi, active):
        mid = 0.5 * lo + 0.5 * hi
        cnt = count_ge(mid)
        on = active > 0
        ge = cnt >= TOPK
        still = jnp.logical_and(cnt != TOPK, jnp.logical_and(mid != lo, mid != hi))
        lo = jnp.where(jnp.logical_and(on, ge), mid, lo)
        hi = jnp.where(jnp.logical_and(on, jnp.logical_not(ge)), mid, hi)
        return lo, hi, jnp.where(jnp.logical_and(on, still), 1, 0).astype(I32)

    def body(st):
        it, lo, hi, active = st
        for _ in range(STEPS_PER_TRIP):
            lo, hi, active = step(lo, hi, active)
        return it + STEPS_PER_TRIP, lo, hi, active

    _, lo, _, _ = lax.while_loop(cond, body, (jnp.int32(0), smin, _above(smax), active0.astype(I32)))
    return lo


def _head_slab(x_ref, h):
    j = h // 2
    return x_ref[:, j * LANES:(j + 1) * LANES]


def _place_half(slab, src_half, dst_half):
    lane = lax.broadcasted_iota(I32, slab.shape, 1)
    x = slab if src_half == dst_half else pltpu.roll(slab, HEAD_DIM, 1)
    return jnp.where((lane >= HEAD_DIM) == (dst_half == 1), x, 0.0)


def _gather_heads(o, rows):
    outs = []
    for j in range(N_HEADS // 2):
        n = (2 * j) // GROUP
        tc, th = n // 2, n % 2
        a0 = o[(2 * j) * rows:(2 * j + 1) * rows, tc * LANES:(tc + 1) * LANES]
        a1 = o[(2 * j + 1) * rows:(2 * j + 2) * rows, tc * LANES:(tc + 1) * LANES]
        x0 = a0 if th == 0 else pltpu.roll(a0, HEAD_DIM, 1)
        x1 = a1 if th == 1 else pltpu.roll(a1, HEAD_DIM, 1)
        lane = lax.broadcasted_iota(I32, x0.shape, 1)
        outs.append(jnp.where(lane < HEAD_DIM, x0, x1))
    return outs


def _build_qbd(q_ref, qbd_ref, rows):
    for h in range(N_HEADS):
        n = h // GROUP
        tc, th = n // 2, n % 2
        placed = _place_half(_head_slab(q_ref, h) * ATTN_SCALE, h % 2, th).astype(qbd_ref.dtype)
        for c in range(D_KV // LANES):
            qbd_ref[h * rows:(h + 1) * rows, c * LANES:(c + 1) * LANES] = (
                placed if c == tc else jnp.zeros_like(placed))


def _softmax_step(a3, m_ref, l_ref, acc_ref, vc, rows, v_transposed):
    n = N_HEADS * rows
    ch = a3.shape[2]
    m_prev = m_ref[...].reshape(N_HEADS, rows, 1)
    m_new = jnp.maximum(m_prev, jnp.max(a3, axis=2, keepdims=True))
    p = jnp.exp(a3 - m_new)
    alpha = jnp.exp(m_prev - m_new)
    l_ref[...] = (alpha * l_ref[...].reshape(N_HEADS, rows, 1)
                  + jnp.sum(p, axis=2, keepdims=True)).reshape(n, 1)
    m_ref[...] = m_new.reshape(n, 1)
    pb = p.reshape(n, ch).astype(BF16)
    if v_transposed:
        pv = lax.dot_general(pb, vc, NT_DIMS, preferred_element_type=F32)
    else:
        pv = jnp.dot(pb, vc, preferred_element_type=F32)
    acc_ref[...] = alpha.reshape(n, 1) * acc_ref[...] + pv


ATT_CH = 512
CNT_ROWS = 8 * SUBLANES
N_ATT_CH = R // ATT_CH
NT_DIMS = (((1,), (1,)), ((), ()))
BOUND_MARGIN = 1.0 + 2.0 ** -5
MIN_ROW_SUM = 1e-30


def _inclusive_prefix_matrix(n):
    r_i = lax.broadcasted_iota(I32, (n, n), 0)
    c_i = lax.broadcasted_iota(I32, (n, n), 1)
    return jnp.where(r_i <= c_i, 1.0, 0.0).astype(BF16)


def _attn_prompt_kernel(qi_ref, wi_ref, q_ref, kib_ref, kb_ref, vb_ref, o_ref, lmin_ref,
                        s_ref, st_ref, qim_ref, wb_ref, qbd_ref, m_ref, l_ref, acc_ref, p_ref,
                        kmax_ref, tri_ref, *, bound_max):
    i = pl.program_id(0)

    @pl.when(i >= N_QBLK)
    def _():
        o_ref[...] = jnp.zeros_like(o_ref)
        lmin_ref[...] = jnp.ones_like(lmin_ref)

    @pl.when(i == 0)
    def _():
        tri_ref[...] = _inclusive_prefix_matrix(ATT_CH)

    if bound_max:
        @pl.when(i == 0)
        def _():
            r_i = lax.broadcasted_iota(I32, (D_KV, LANES), 0)
            c_i = lax.broadcasted_iota(I32, (D_KV, LANES), 1)
            sel = jnp.where(r_i // HEAD_DIM == c_i, 1.0, 0.0).astype(BF16)

            def body(c, mx):
                kc = kb_ref[pl.ds(pl.multiple_of(c * ATT_CH, ATT_CH), ATT_CH), :].astype(F32)
                n2 = jnp.dot((kc * kc).astype(BF16), sel, preferred_element_type=F32)
                return jnp.maximum(mx, n2)

            mx = lax.fori_loop(0, N_ATT_CH, body, jnp.zeros((ATT_CH, LANES), F32))
            kmax = jnp.max(mx, axis=0, keepdims=True)
            for n in range(N_KV_HEADS):
                kmax_ref[n] = jnp.broadcast_to(kmax[:, n:n + 1], (QB, LANES))

    @pl.when(i < N_QBLK)
    def _():
        n_ch = (i * QB) // ATT_CH + 1
        qrow = i * QB + lax.broadcasted_iota(I32, (QB, 1), 0)

        wi = wi_ref[...]
        for h in range(N_IDX_HEADS):
            qim_ref[h * QB:(h + 1) * QB, :] = _place_half(_head_slab(qi_ref, h), h % 2, h % 2).astype(BF16)
            wb_ref[h] = jnp.broadcast_to(wi[:, h:h + 1], (QB, LANES))
        _build_qbd(q_ref, qbd_ref, QB)

        def score_chunk(c, carry):
            start = pl.multiple_of(c * ATT_CH, ATT_CH)
            kc = kib_ref[pl.ds(start, ATT_CH), :]
            acc = [jnp.zeros((QB, LANES), F32) for _ in range(ATT_CH // LANES)]
            hpd = 4
            for hg in range(N_IDX_HEADS // hpd):
                d = lax.dot_general(qim_ref[hg * hpd * QB:(hg + 1) * hpd * QB, :], kc, NT_DIMS,
                                    preferred_element_type=F32)
                for hh in range(hpd):
                    w = wb_ref[hg * hpd + hh]
                    for t in range(ATT_CH // LANES):
                        acc[t] = acc[t] + w * jnp.maximum(
                            d[hh * QB:(hh + 1) * QB, t * LANES:(t + 1) * LANES], 0.0)
            for t in range(ATT_CH // LANES):
                kpos = start + t * LANES + lax.broadcasted_iota(I32, (QB, LANES), 1)
                sc = jnp.where(kpos <= qrow, acc[t], NEG_INF)
                s_ref[c, :, t * LANES:(t + 1) * LANES] = sc
                st_ref[pl.ds(pl.multiple_of(start + t * LANES, LANES), LANES), :] = sc.T
            return carry

        lax.fori_loop(0, n_ch, score_chunk, 0)

        def key_major_chunk(c):
            return st_ref[pl.ds(pl.multiple_of(c * ATT_CH, ATT_CH), ATT_CH), :]

        def fold(x):
            return x.reshape(ATT_CH // CNT_ROWS, CNT_ROWS, QB)

        def count(t, strict):
            def body(c, cnt):
                sc = key_major_chunk(c)
                hit = jnp.where(sc > t if strict else sc >= t, 1.0, 0.0)
                return cnt + jnp.sum(fold(hit), axis=0)

            cnt = lax.fori_loop(0, n_ch, body, jnp.zeros((CNT_ROWS, QB), F32))
            return jnp.sum(cnt, axis=0, keepdims=True)

        def extremes(c, mm):
            sc = key_major_chunk(c)
            return (jnp.minimum(mm[0], jnp.min(fold(jnp.where(sc > NEG_INF, sc, -NEG_INF)), axis=0)),
                    jnp.maximum(mm[1], jnp.max(fold(sc), axis=0)))

        smin, smax = lax.fori_loop(0, n_ch, extremes, (jnp.full((CNT_ROWS, QB), -NEG_INF, F32),
                                                       jnp.full((CNT_ROWS, QB), NEG_INF, F32)))
        smin = jnp.min(smin, axis=0, keepdims=True)
        smax = jnp.max(smax, axis=0, keepdims=True)
        qrow_l = i * QB + lax.broadcasted_iota(I32, (1, QB), 1)
        lo = _bisect_threshold(functools.partial(count, strict=False), smin, smax,
                               jnp.logical_and(qrow_l + 1 > TOPK, smax > smin))
        lo_b = jnp.broadcast_to(lo, (QB, QB)).T

        excess = count(lo, strict=False) > TOPK
        tie_quota = jnp.where(excess, TOPK - count(lo, strict=True), TOPK)
        has_excess = jnp.max(jnp.where(excess, 1, 0)) > 0
        quota_b = jnp.broadcast_to(tie_quota, (QB, QB)).T

        m_ref[...] = jnp.full(m_ref.shape, NEG, F32)
        l_ref[...] = jnp.zeros_like(l_ref)
        acc_ref[...] = jnp.zeros_like(acc_ref)
        n_t = ATT_CH // LANES

        def plain_bias(c):
            return [jnp.where(s_ref[c, :, t * LANES:(t + 1) * LANES] >= lo_b, 0.0, NEG)
                    for t in range(n_t)]

        def quota_bias(c, seen):
            keys = [s_ref[c, :, t * LANES:(t + 1) * LANES] for t in range(n_t)]
            eq = [jnp.where(k == lo_b, 1.0, 0.0) for k in keys]
            incl = jnp.dot(jnp.concatenate(eq, axis=1).astype(BF16), tri_ref[...],
                           preferred_element_type=F32)
            bias = []
            for t in range(n_t):
                rank = seen + incl[:, t * LANES:(t + 1) * LANES] - eq[t]
                keep = jnp.logical_or(keys[t] > lo_b,
                                      jnp.logical_and(keys[t] == lo_b, rank < quota_b))
                bias.append(jnp.where(keep, 0.0, NEG))
            return bias, seen + jnp.broadcast_to(incl[:, ATT_CH - 1:ATT_CH], (QB, LANES))

        def masked_logits(c, bias):
            start = pl.multiple_of(c * ATT_CH, ATT_CH)
            a = lax.dot_general(qbd_ref[...], kb_ref[pl.ds(start, ATT_CH), :], NT_DIMS,
                                preferred_element_type=F32)
            return start, [[a[h * QB:(h + 1) * QB, t * LANES:(t + 1) * LANES] + bias[t]
                            for t in range(n_t)] for h in range(N_HEADS)]

        def max_chunk(c, carry):
            _, a = masked_logits(c, plain_bias(c))
            for h in range(N_HEADS):
                rows = slice(h * QB, (h + 1) * QB)
                m = m_ref[rows, :]
                for t in range(n_t):
                    m = jnp.maximum(m, a[h][t])
                m_ref[rows, :] = m
            return carry

        if bound_max:
            for h in range(N_HEADS):
                rows = slice(h * QB, (h + 1) * QB)
                qf = qbd_ref[rows, :].astype(F32)
                qn2 = jnp.sum(qf * qf, axis=1, keepdims=True)
                m_ref[rows, :] = jnp.sqrt(jnp.broadcast_to(qn2, (QB, LANES))
                                          * kmax_ref[h // GROUP]) * BOUND_MARGIN
        else:
            lax.fori_loop(0, n_ch, max_chunk, 0)
            m_ref[...] = jnp.broadcast_to(jnp.max(m_ref[...], axis=1, keepdims=True), m_ref.shape)

        def value_chunk(c, seen, with_quota):
            if with_quota:
                bias, seen = quota_bias(c, seen)
            else:
                bias = plain_bias(c)
            start, a = masked_logits(c, bias)
            for h in range(N_HEADS):
                rows = slice(h * QB, (h + 1) * QB)
                m = m_ref[rows, :]
                l = l_ref[rows, :]
                for t in range(n_t):
                    p = jnp.exp(a[h][t] - m)
                    l = l + p
                    p_ref[rows, t * LANES:(t + 1) * LANES] = p.astype(BF16)
                l_ref[rows, :] = l
            acc_ref[...] += jnp.dot(p_ref[...], vb_ref[pl.ds(start, ATT_CH), :],
                                    preferred_element_type=F32)
            return seen

        seen0 = jnp.zeros((QB, LANES), F32)

        @pl.when(has_excess)
        def _():
            lax.fori_loop(0, n_ch, functools.partial(value_chunk, with_quota=True), seen0)

        @pl.when(jnp.logical_not(has_excess))
        def _():
            lax.fori_loop(0, n_ch, functools.partial(value_chunk, with_quota=False), seen0)

        l = jnp.sum(l_ref[...], axis=1, keepdims=True)
        lmin_ref[...] = jnp.broadcast_to(jnp.min(l, axis=0, keepdims=True), lmin_ref.shape)
        o = acc_ref[...] / l
        for j, slab in enumerate(_gather_heads(o, QB)):
            o_ref[:, j * LANES:(j + 1) * LANES] = slab.astype(BF16)


def _attn_prompt(qi, wi, q, kib, kb, vb, bound_max):
    row = lambda i: (i, 0)
    full = lambda i: (0, 0)
    nrow = N_HEADS * QB
    return pl.pallas_call(
        functools.partial(_attn_prompt_kernel, bound_max=bound_max),
        grid=(R // QB,),
        in_specs=[
            pl.BlockSpec((QB, N_IDX_HEADS * IDX_DIM), row),
            pl.BlockSpec((QB, LANES), row),
            pl.BlockSpec((QB, D_ATTN), row),
            pl.BlockSpec((R, LANES), full),
            pl.BlockSpec((R, D_KV), full),
            pl.BlockSpec((R, D_KV), full),
        ],
        out_specs=[pl.BlockSpec((QB, D_ATTN), row),
                   pl.BlockSpec((1, SUBLANES, LANES), lambda i: (i, 0, 0))],
        out_shape=(jax.ShapeDtypeStruct((R, D_ATTN), BF16),
                   jax.ShapeDtypeStruct((R // QB, SUBLANES, LANES), F32)),
        scratch_shapes=[
            pltpu.VMEM((N_ATT_CH, QB, ATT_CH), F32),
            pltpu.VMEM((R, QB), F32),
            pltpu.VMEM((nrow, LANES), BF16),
            pltpu.VMEM((N_IDX_HEADS, QB, LANES), F32),
            pltpu.VMEM((nrow, D_KV), BF16),
            pltpu.VMEM((nrow, LANES), F32),
            pltpu.VMEM((nrow, LANES), F32),
            pltpu.VMEM((nrow, D_KV), F32),
            pltpu.VMEM((nrow, ATT_CH), BF16),
            pltpu.VMEM((N_KV_HEADS, QB, LANES), F32),
            pltpu.VMEM((ATT_CH, ATT_CH), BF16),
        ],
        compiler_params=_cparams(("arbitrary",)),
        name="attn_prompt_bound" if bound_max else "attn_prompt",
    )(qi, wi, q, kib, kb, vb)


S_CH = 2048
S_NCH = PAST_LEN // S_CH
S_PPC = S_CH // PAGE_SIZE
S_TIE = 512
S_ROWS = N_HEADS * DEC_SEQ


def _page_lanes(p):
    return pl.ds(pl.multiple_of(p * PAGE_SIZE, PAGE_SIZE), PAGE_SIZE)


def _kidx_page_copy(pt_ref, kidx_hbm, kbuf, sem, b, p, slot):
    return pltpu.make_async_copy(
        kidx_hbm.at[pt_ref[b, p]], kbuf.at[slot, :, _page_lanes(p)], sem.at[slot])


def _sample_score_kernel(pt_ref, qi_ref, wi_ref, kin_ref, kidx_hbm, s_ref, sn_ref,
                         kbuf, sem, knp_ref):
    b = pl.program_id(0)
    nb = pl.num_programs(0)
    slot = b % 2

    def start_batch(bb, sl):
        def body(p, c):
            _kidx_page_copy(pt_ref, kidx_hbm, kbuf, sem, bb, p, sl).start()
            return c
        lax.fori_loop(0, N_PAGES, body, 0)

    @pl.when(b == 0)
    def _():
        start_batch(0, 0)

    @pl.when(b + 1 < nb)
    def _():
        start_batch(b + 1, 1 - slot)

    def wait_body(p, c):
        _kidx_page_copy(pt_ref, kidx_hbm, kbuf, sem, b, p, slot).wait()
        return c
    lax.fori_loop(0, N_PAGES, wait_body, 0)

    qi = qi_ref[0]
    qis = jnp.concatenate(
        [qi[:, h * IDX_DIM:(h + 1) * IDX_DIM] for h in range(N_IDX_HEADS)], axis=0).astype(BF16)
    wi = wi_ref[0]
    wb = [jnp.broadcast_to(wi[:, h:h + 1], (DEC_SEQ, LANES)) for h in range(N_IDX_HEADS)]

    def head_sum(d, width):
        outs = []
        for t in range(width // LANES):
            acc = jnp.zeros((DEC_SEQ, LANES), F32)
            for h in range(N_IDX_HEADS):
                acc = acc + wb[h] * jnp.maximum(
                    d[h * DEC_SEQ:(h + 1) * DEC_SEQ, t * LANES:(t + 1) * LANES], 0.0)
            outs.append(acc)
        return outs

    def score_chunk(c, carry):
        kc = kbuf[slot, :, pl.ds(pl.multiple_of(c * S_CH, S_CH), S_CH)].astype(BF16)
        d = jnp.dot(qis, kc, preferred_element_type=F32)
        for t, acc in enumerate(head_sum(d, S_CH)):
            s_ref[c, :, t * LANES:(t + 1) * LANES] = acc
        return carry
    lax.fori_loop(0, S_NCH, score_chunk, 0)

    knp_ref[...] = jnp.zeros_like(knp_ref)
    knp_ref[:DEC_SEQ, :] = kin_ref[0][:, :IDX_DIM]
    d = lax.dot_general(qis, knp_ref[...].astype(BF16), NT_DIMS, preferred_element_type=F32)
    kpos = lax.broadcasted_iota(I32, (DEC_SEQ, LANES), 1)
    srow = lax.broadcasted_iota(I32, (DEC_SEQ, LANES), 0)
    sn_ref[...] = jnp.where(kpos <= srow, head_sum(d, LANES)[0], NEG_INF)


def _sample_scores(page_table, qi_s, wi_s, ki_s, cache_kidx):
    blk = lambda b, pt: (b, 0, 0)
    grid_spec = pltpu.PrefetchScalarGridSpec(
        num_scalar_prefetch=1,
        grid=(DEC_BATCH,),
        in_specs=[
            pl.BlockSpec((1, DEC_SEQ, N_IDX_HEADS * IDX_DIM), blk),
            pl.BlockSpec((1, DEC_SEQ, LANES), blk),
            pl.BlockSpec((1, DEC_SEQ, LANES), blk),
            pl.BlockSpec(memory_space=pl.ANY),
        ],
        out_specs=[
            pl.BlockSpec((S_NCH, DEC_SEQ, S_CH), lambda b, pt: (0, b, 0)),
            pl.BlockSpec((DEC_SEQ, LANES), lambda b, pt: (b, 0)),
        ],
        scratch_shapes=[
            pltpu.VMEM((2, IDX_DIM, PAST_LEN), F32),
            pltpu.SemaphoreType.DMA((2,)),
            pltpu.VMEM((LANES, IDX_DIM), F32),
        ],
    )
    return pl.pallas_call(
        _sample_score_kernel,
        grid_spec=grid_spec,
        out_shape=(
            jax.ShapeDtypeStruct((S_NCH, NS, S_CH), F32),
            jax.ShapeDtypeStruct((NS, LANES), F32),
        ),
        compiler_params=_cparams(("arbitrary",)),
        name="sample_scores",
    )(page_table, qi_s, wi_s, ki_s, cache_kidx)


def _sample_threshold_kernel(s_ref, sn_ref, bp_ref, bn_ref):
    def lane_fold(x, op):
        out = x[:, :LANES]
        for t in range(1, S_CH // LANES):
            out = op(out, x[:, t * LANES:(t + 1) * LANES])
        return out

    def count(t, strict):
        def hits(sc):
            return jnp.where(sc > t if strict else sc >= t, 1.0, 0.0)

        cnt = lax.fori_loop(0, S_NCH, lambda c, cnt: cnt + lane_fold(hits(s_ref[c]), jnp.add),
                            hits(sn_ref[...]))
        return jnp.sum(cnt, axis=1, keepdims=True)

    def extremes(c, mm):
        return (jnp.minimum(mm[0], lane_fold(s_ref[c], jnp.minimum)),
                jnp.maximum(mm[1], lane_fold(s_ref[c], jnp.maximum)))

    new = sn_ref[...]
    smin, smax = lax.fori_loop(0, S_NCH, extremes, (jnp.where(new > NEG_INF, new, -NEG_INF), new))
    smin = jnp.min(smin, axis=1, keepdims=True)
    smax = jnp.max(smax, axis=1, keepdims=True)
    lo = _bisect_threshold(functools.partial(count, strict=False), smin, smax, smax > smin)

    quota = jnp.where(count(lo, strict=False) > TOPK, TOPK - count(lo, strict=True), TOPK)
    tri = _inclusive_prefix_matrix(S_TIE)

    def mask_tile(keys, seen, tri_t):
        eq = jnp.where(keys == lo, 1.0, 0.0)
        incl = jnp.dot(eq.astype(BF16), tri_t, preferred_element_type=F32)
        keep = jnp.logical_or(keys > lo, jnp.logical_and(keys == lo, seen + incl - eq < quota))
        return jnp.where(keep, 0.0, NEG), seen + incl[:, keys.shape[1] - 1:]

    def chunk(c, seen):
        for t in range(S_CH // S_TIE):
            cols = slice(t * S_TIE, (t + 1) * S_TIE)
            bp_ref[c, :, cols], seen = mask_tile(s_ref[c, :, cols], seen, tri)
        return seen

    seen = lax.fori_loop(0, S_NCH, chunk, jnp.zeros((NS, 1), F32))
    bn_ref[...], _ = mask_tile(sn_ref[...], seen, tri[:LANES, :LANES])


def _sample_threshold(keys_past, keys_new):
    vmem = pl.BlockSpec(memory_space=pltpu.VMEM)
    return pl.pallas_call(
        _sample_threshold_kernel,
        in_specs=[vmem, vmem],
        out_specs=[vmem, vmem],
        out_shape=(
            jax.ShapeDtypeStruct((S_NCH, NS, S_CH), F32),
            jax.ShapeDtypeStruct((NS, LANES), F32),
        ),
        compiler_params=pltpu.CompilerParams(vmem_limit_bytes=VMEM_LIMIT),
        name="sample_threshold",
    )(keys_past, keys_new)


def _kv_page_copies(pt_ref, ck_hbm, cv_hbm, kbuf, vbuf, sem, b, c, p, slot):
    page = pt_ref[b, c * S_PPC + p]
    dst = _page_lanes(p)
    return (pltpu.make_async_copy(ck_hbm.at[page], kbuf.at[slot, :, dst], sem.at[0, slot]),
            pltpu.make_async_copy(cv_hbm.at[page], vbuf.at[slot, :, dst], sem.at[1, slot]))


def _sample_attn_kernel(pt_ref, q_ref, bp_ref, bn_ref, kn_ref, vn_ref, ck_hbm, cv_hbm, o_ref,
                        kbuf, vbuf, sem, qbd_ref, m_ref, l_ref, acc_ref, knp_ref, vnp_ref):
    b = pl.program_id(0)
    c = pl.program_id(1)
    step = b * S_NCH + c
    n_steps = pl.num_programs(0) * S_NCH
    slot = step % 2

    def start_chunk(st, sl):
        bb = st // S_NCH
        cc = st % S_NCH

        def body(p, carry):
            for cp in _kv_page_copies(pt_ref, ck_hbm, cv_hbm, kbuf, vbuf, sem, bb, cc, p, sl):
                cp.start()
            return carry
        lax.fori_loop(0, S_PPC, body, 0)

    @pl.when(step == 0)
    def _():
        start_chunk(0, 0)

    @pl.when(step + 1 < n_steps)
    def _():
        start_chunk(step + 1, 1 - slot)

    def wait_body(p, carry):
        for cp in _kv_page_copies(pt_ref, ck_hbm, cv_hbm, kbuf, vbuf, sem, b, c, p, slot):
            cp.wait()
        return carry
    lax.fori_loop(0, S_PPC, wait_body, 0)

    @pl.when(c == 0)
    def _():
        _build_qbd(q_ref.at[0], qbd_ref, DEC_SEQ)
        m_ref[...] = jnp.full(m_ref.shape, NEG, F32)
        l_ref[...] = jnp.zeros_like(l_ref)
        acc_ref[...] = jnp.zeros_like(acc_ref)

    def update(kc, vc, bias, feature_major):
        qbd = qbd_ref[...].astype(BF16)
        if feature_major:
            a = jnp.dot(qbd, kc, preferred_element_type=F32)
        else:
            a = lax.dot_general(qbd, kc, NT_DIMS, preferred_element_type=F32)
        a3 = a.reshape(N_HEADS, DEC_SEQ, a.shape[1]) + bias[None]
        _softmax_step(a3, m_ref, l_ref, acc_ref, vc, DEC_SEQ, feature_major)

    update(kbuf[slot].astype(BF16), vbuf[slot].astype(BF16), bp_ref[0], True)

    @pl.when(c == S_NCH - 1)
    def _():
        knp_ref[...] = jnp.zeros_like(knp_ref)
        vnp_ref[...] = jnp.zeros_like(vnp_ref)
        knp_ref[:DEC_SEQ, :] = kn_ref[0]
        vnp_ref[:DEC_SEQ, :] = vn_ref[0]
        update(knp_ref[...].astype(BF16), vnp_ref[...].astype(BF16), bn_ref[...], False)
        o = acc_ref[...] / l_ref[...]
        for j, slab in enumerate(_gather_heads(o, DEC_SEQ)):
            o_ref[0, :, j * LANES:(j + 1) * LANES] = slab


def _sample_attn(page_table, q_s, bias_past, bias_new, k_s, v_s, cache_k, cache_v):
    blk = lambda b, c, pt: (b, 0, 0)
    grid_spec = pltpu.PrefetchScalarGridSpec(
        num_scalar_prefetch=1,
        grid=(DEC_BATCH, S_NCH),
        in_specs=[
            pl.BlockSpec((1, DEC_SEQ, D_ATTN), blk),
            pl.BlockSpec((1, DEC_SEQ, S_CH), lambda b, c, pt: (c, b, 0)),
            pl.BlockSpec((DEC_SEQ, LANES), lambda b, c, pt: (b, 0)),
            pl.BlockSpec((1, DEC_SEQ, D_KV), blk),
            pl.BlockSpec((1, DEC_SEQ, D_KV), blk),
            pl.BlockSpec(memory_space=pl.ANY),
            pl.BlockSpec(memory_space=pl.ANY),
        ],
        out_specs=pl.BlockSpec((1, DEC_SEQ, D_ATTN), blk),
        scratch_shapes=[
            pltpu.VMEM((2, D_KV, S_CH), F32),
            pltpu.VMEM((2, D_KV, S_CH), F32),
            pltpu.SemaphoreType.DMA((2, 2)),
            pltpu.VMEM((S_ROWS, D_KV), F32),
            pltpu.VMEM((S_ROWS, 1), F32),
            pltpu.VMEM((S_ROWS, 1), F32),
            pltpu.VMEM((S_ROWS, D_KV), F32),
            pltpu.VMEM((LANES, D_KV), F32),
            pltpu.VMEM((LANES, D_KV), F32),
        ],
    )
    return pl.pallas_call(
        _sample_attn_kernel,
        grid_spec=grid_spec,
        out_shape=jax.ShapeDtypeStruct((DEC_BATCH, DEC_SEQ, D_ATTN), F32),
        compiler_params=_cparams(("arbitrary", "arbitrary")),
        name="sample_attn",
    )(page_table, q_s, bias_past, bias_new, k_s, v_s, cache_k, cache_v)


def _rope_tables():
    pos = np.zeros((R,), np.float32)
    pos[:T] = np.arange(T)
    pos[ROW_S:ROW_S + NS] = np.tile(PAST_LEN + np.arange(DEC_SEQ), DEC_BATCH)
    half = HEAD_DIM // 2
    inv_freq = ROPE_THETA ** (-(jnp.arange(half, dtype=F32) * 2.0 / HEAD_DIM))
    ang = jnp.asarray(pos)[:, None] * inv_freq
    cos = jnp.tile(jnp.cos(ang), (1, LANES // half))
    sin = jnp.sin(ang)
    sin = jnp.tile(jnp.concatenate([-sin, sin], axis=1), (1, LANES // HEAD_DIM))
    return cos, sin


def kernel(x_prompt, x_sample, cache_k, cache_v, cache_kidx, state_pool, page_table, meta_tokens,
           norm_mix_pre, w_in, idx_k_norm, w_pool, pool_scale, w_attn_o, w_out, norm_mix_post,
           norm_ffn_pre, w_up, w_down, norm_ffn_post):
    x_all = jnp.concatenate([
        meta_tokens, x_prompt[0], jnp.zeros((ROW_S - T, D_MODEL), F32),
        x_sample.reshape(NS, D_MODEL), jnp.zeros((R - ROW_S - NS, D_MODEL), F32)], axis=0)

    wt = jnp.transpose(w_in[0])
    o_u, o_q, o_k, o_v, o_qi, o_ki, o_wi, o_gp = np.cumsum((0, D_POOL, D_ATTN, D_KV, D_KV,
                                                            N_IDX_HEADS * IDX_DIM, IDX_DIM, N_IDX_HEADS))
    n_pad = W_ROW_C - W_B_COLS - (o_v - o_q) - (o_gp - o_qi)
    w_all = jnp.concatenate([
        wt[o_v:o_qi], wt[o_u:o_q], wt[o_q:o_v], wt[o_qi:o_gp],
        jnp.zeros((n_pad, D_MODEL), F32), wt[o_gp:]], axis=0).astype(BF16)
    assert w_all.shape == (W_ROWS, D_MODEL)

    g_pre = norm_mix_pre[0][None]
    cos, sin = _rope_tables()
    gk = jnp.tile(idx_k_norm[0], LANES // IDX_DIM)[None]

    q, k, kb, qi, ki, kib, wi = _proj_a(x_all, g_pre, w_all, cos, sin, gk)
    v, vb, u = _proj_b(x_all, g_pre, w_all)
    gates = _proj_c(x_all, g_pre, w_all)

    def sample_rows(a):
        return a[ROW_S:ROW_S + NS].reshape(DEC_BATCH, DEC_SEQ, a.shape[1])

    pooled = _pool_prompt(u)
    u_s = sample_rows(u)
    state = state_pool[0]
    state16 = jnp.pad(state, ((0, 0), (HALO - POOL_BUF, 0), (0, 0)))
    pooled_s = _pool_sample(u_s, state16)
    pooled = lax.dynamic_update_slice(pooled, pooled_s.reshape(NS, D_POOL), (ROW_S, 0))

    attn, lmin = _attn_prompt(qi, wi, q, kib, kb, vb, bound_max=True)
    attn = lax.cond(jnp.min(lmin) >= MIN_ROW_SUM, lambda: attn,
                    lambda: _attn_prompt(qi, wi, q, kib, kb, vb, bound_max=False)[0])
    k_s, v_s = sample_rows(k), sample_rows(v)
    n_pool = cache_k.shape[1]
    kidx_fm = jnp.transpose(cache_kidx[0], (0, 2, 1))
    k_fm = jnp.transpose(cache_k[0], (0, 2, 3, 1)).reshape(n_pool, D_KV, PAGE_SIZE)
    v_fm = jnp.transpose(cache_v[0], (0, 2, 3, 1)).reshape(n_pool, D_KV, PAGE_SIZE)
    keys_past, keys_new = _sample_scores(page_table, sample_rows(qi), sample_rows(wi), sample_rows(ki),
                                         kidx_fm)
    bias_past, bias_new = _sample_threshold(keys_past, keys_new)
    attn_s = _sample_attn(page_table, sample_rows(q), bias_past, bias_new, k_s, v_s, k_fm, v_fm)
    attn = lax.dynamic_update_slice(attn, attn_s.reshape(NS, D_ATTN).astype(BF16), (ROW_S, 0))

    x1 = _merge(x_all, pooled, attn, gates, w_pool[0].astype(BF16), pool_scale[0][None],
                w_attn_o[0].astype(BF16), w_out[0].astype(BF16), norm_mix_post[0][None])
    y = _ffn(x1, norm_ffn_pre[0][None], w_up[0].astype(BF16), w_down[0].astype(BF16),
             norm_ffn_post[0][None])

    y_prompt = y[N_META:T][None]
    y_sample = y[ROW_S:ROW_S + NS].reshape(DEC_BATCH, DEC_SEQ, D_MODEL)
    kv_shape = (1, 1, T, N_KV_HEADS, HEAD_DIM)
    kv_s_shape = (1, DEC_BATCH, DEC_SEQ, N_KV_HEADS, HEAD_DIM)
    return (
        y_prompt, y_sample,
        k[:T].reshape(kv_shape), v[:T].reshape(kv_shape), ki[:T, :IDX_DIM][None, None],
        u[T - POOL_BUF:T][None, None],
        k_s.reshape(kv_s_shape), v_s.reshape(kv_s_shape), sample_rows(ki)[None, :, :, :IDX_DIM],
        jnp.concatenate([state[:, DEC_SEQ:], u_s], axis=1)[None],
    )
```

```python
import functools

import jax
import jax.numpy as jnp
import numpy as np
from jax import lax
from jax.experimental import pallas as pl
from jax.experimental.pallas import tpu as pltpu

F32 = jnp.float32
BF16 = jnp.bfloat16
I32 = jnp.int32

D_MODEL = 2048
SEQ = 8192
DEC_BATCH = 32
DEC_SEQ = 8
PAST_LEN = 16384
PAGE_SIZE = 128
N_PAGES = PAST_LEN // PAGE_SIZE
N_META = 16
N_HEADS = 16
N_KV_HEADS = 4
HEAD_DIM = 64
GROUP = N_HEADS // N_KV_HEADS
D_ATTN = N_HEADS * HEAD_DIM
D_KV = N_KV_HEADS * HEAD_DIM
ATTN_SCALE = HEAD_DIM ** -0.5
N_IDX_HEADS = 16
IDX_DIM = 64
INDEX_W_SCALE = (N_IDX_HEADS ** -0.5) * (IDX_DIM ** -0.5)
TOPK = 256
POOL_WINDOWS = (2, 4, 8, 16)
N_POOL_GROUPS = 4
D_POOL = D_MODEL // 2
POOL_GROUP = D_POOL // N_POOL_GROUPS
POOL_OUT_GROUP = D_MODEL // N_POOL_GROUPS
POOL_BUF = max(POOL_WINDOWS) - 1
D_FF = 4 * D_MODEL
ROPE_THETA = 10000.0
EPS = 1e-6

LANES = 128
SUBLANES = 8
T = SEQ + N_META
QB = 128
N_QBLK = -(-T // QB)
ROW_S = N_QBLK * QB
NS = DEC_BATCH * DEC_SEQ
R = 8704
VMEM_LIMIT = 56 * 1024 * 1024

NEG = -1e30


def _cparams(sem):
    return pltpu.CompilerParams(dimension_semantics=sem, vmem_limit_bytes=VMEM_LIMIT)


def _rms(x, g):
    return x * lax.rsqrt(jnp.mean(x * x, axis=-1, keepdims=True) + EPS) * g


def _swap_halves(x):
    lane = lax.broadcasted_iota(I32, x.shape, 1)
    return jnp.where(lane % HEAD_DIM < HEAD_DIM // 2,
                     pltpu.roll(x, LANES - HEAD_DIM // 2, 1),
                     pltpu.roll(x, HEAD_DIM // 2, 1))


def _rope_cols(x, cos, sin):
    outs = []
    for c in range(x.shape[1] // LANES):
        xc = x[:, c * LANES:(c + 1) * LANES]
        outs.append(xc * cos + _swap_halves(xc) * sin)
    return outs


PROJ_TM = 544
N_ROPE_A = D_ATTN + D_KV
PROJ_TN_A = N_ROPE_A
W_A_COLS = 2 * PROJ_TN_A
W_B_COLS = D_KV + D_POOL
W_C_COLS = 2 * D_MODEL
PROJ_TN_C = 2048
W_ROW_B = 0
W_ROW_A = W_B_COLS
W_ROW_C = 2 * PROJ_TN_C
W_ROWS = W_ROW_C + W_C_COLS
assert W_ROW_A % PROJ_TN_A == 0 and W_ROW_A + W_A_COLS <= W_ROW_C


def _norm_to_scratch(x_ref, g_ref, xn_ref):
    @pl.when(pl.program_id(1) == 0)
    def _():
        xn_ref[...] = _rms(x_ref[...], g_ref[...]).astype(BF16)


def _proj_dot(xn_ref, w_ref):
    return lax.dot_general(xn_ref[...], w_ref[...], (((1,), (1,)), ((), ())),
                           preferred_element_type=F32)


def _proj_a_kernel(x_ref, g_ref, w_ref, cos_ref, sin_ref, gk_ref,
                   q_ref, k_ref, kb_ref, qi_ref, ki_ref, kib_ref, wi_ref, xn_ref):
    _norm_to_scratch(x_ref, g_ref, xn_ref)
    j = pl.program_id(1)
    p = _proj_dot(xn_ref, w_ref)
    cos = cos_ref[...]
    sin = sin_ref[...]

    @pl.when(j == 0)
    def _():
        cols = _rope_cols(p, cos, sin)
        nq = D_ATTN // LANES
        for c in range(nq):
            q_ref[:, c * LANES:(c + 1) * LANES] = cols[c]
        for c in range(D_KV // LANES):
            k_ref[:, c * LANES:(c + 1) * LANES] = cols[nq + c]
            kb_ref[:, c * LANES:(c + 1) * LANES] = cols[nq + c].astype(BF16)

    @pl.when(j == 1)
    def _():
        nqi = N_IDX_HEADS * IDX_DIM // LANES
        cols = _rope_cols(p[:, :nqi * LANES], cos, sin)
        for c in range(nqi):
            qi_ref[:, c * LANES:(c + 1) * LANES] = cols[c]
        slab = p[:, nqi * LANES:(nqi + 1) * LANES]
        lane = lax.broadcasted_iota(I32, slab.shape, 1)
        is_ki = lane < IDX_DIM
        ms = jnp.sum(jnp.where(is_ki, slab * slab, 0.0), axis=-1, keepdims=True) / IDX_DIM
        kin = slab * lax.rsqrt(ms + EPS) * gk_ref[...]
        kin = kin * cos + _swap_halves(kin) * sin
        ki2 = jnp.where(is_ki, kin, pltpu.roll(kin, IDX_DIM, 1))
        ki_ref[...] = ki2
        kib_ref[...] = ki2.astype(BF16)
        wi_ref[...] = pltpu.roll(slab, LANES - IDX_DIM, 1) * INDEX_W_SCALE


def _proj_a(x_all, g, w_a, cos, sin, gk):
    n_i = R // PROJ_TM
    row = lambda i, j: (i, 0)
    outs = (
        jax.ShapeDtypeStruct((R, D_ATTN), F32),
        jax.ShapeDtypeStruct((R, D_KV), F32),
        jax.ShapeDtypeStruct((R, D_KV), BF16),
        jax.ShapeDtypeStruct((R, N_IDX_HEADS * IDX_DIM), F32),
        jax.ShapeDtypeStruct((R, LANES), F32),
        jax.ShapeDtypeStruct((R, LANES), BF16),
        jax.ShapeDtypeStruct((R, LANES), F32),
    )
    return pl.pallas_call(
        _proj_a_kernel,
        grid=(n_i, 2),
        in_specs=[
            pl.BlockSpec((PROJ_TM, D_MODEL), row),
            pl.BlockSpec((1, D_MODEL), lambda i, j: (0, 0)),
            pl.BlockSpec((PROJ_TN_A, D_MODEL), lambda i, j: (W_ROW_A // PROJ_TN_A + j, 0)),
            pl.BlockSpec((PROJ_TM, LANES), row),
            pl.BlockSpec((PROJ_TM, LANES), row),
            pl.BlockSpec((1, LANES), lambda i, j: (0, 0)),
        ],
        out_specs=[pl.BlockSpec((PROJ_TM, o.shape[1]), row) for o in outs],
        out_shape=outs,
        scratch_shapes=[pltpu.VMEM((PROJ_TM, D_MODEL), BF16)],
        compiler_params=_cparams(("arbitrary", "arbitrary")),
        name="proj_rope",
    )(x_all, g, w_a, cos, sin, gk)


def _proj_b_kernel(x_ref, g_ref, w_ref, v_ref, vb_ref, u_ref, xn_ref):
    _norm_to_scratch(x_ref, g_ref, xn_ref)
    p = _proj_dot(xn_ref, w_ref)
    v_ref[...] = p[:, :D_KV]
    vb_ref[...] = p[:, :D_KV].astype(BF16)
    u_ref[...] = p[:, D_KV:]


def _proj_b(x_all, g, w_b):
    n_i = R // PROJ_TM
    row = lambda i, j: (i, 0)
    outs = (
        jax.ShapeDtypeStruct((R, D_KV), F32),
        jax.ShapeDtypeStruct((R, D_KV), BF16),
        jax.ShapeDtypeStruct((R, D_POOL), F32),
    )
    return pl.pallas_call(
        _proj_b_kernel,
        grid=(n_i, 1),
        in_specs=[
            pl.BlockSpec((PROJ_TM, D_MODEL), row),
            pl.BlockSpec((1, D_MODEL), lambda i, j: (0, 0)),
            pl.BlockSpec((W_B_COLS, D_MODEL), lambda i, j: (W_ROW_B // W_B_COLS, 0)),
        ],
        out_specs=[pl.BlockSpec((PROJ_TM, o.shape[1]), row) for o in outs],
        out_shape=outs,
        scratch_shapes=[pltpu.VMEM((PROJ_TM, D_MODEL), BF16)],
        compiler_params=_cparams(("arbitrary", "arbitrary")),
        name="proj_vu",
    )(x_all, g, w_b)


def _proj_c_kernel(x_ref, g_ref, w_ref, o_ref, xn_ref):
    _norm_to_scratch(x_ref, g_ref, xn_ref)
    o_ref[...] = _proj_dot(xn_ref, w_ref)


def _proj_c(x_all, g, w_c):
    n_i = R // PROJ_TM
    return pl.pallas_call(
        _proj_c_kernel,
        grid=(n_i, W_C_COLS // PROJ_TN_C),
        in_specs=[
            pl.BlockSpec((PROJ_TM, D_MODEL), lambda i, j: (i, 0)),
            pl.BlockSpec((1, D_MODEL), lambda i, j: (0, 0)),
            pl.BlockSpec((PROJ_TN_C, D_MODEL), lambda i, j: (W_ROW_C // PROJ_TN_C + j, 0)),
        ],
        out_specs=pl.BlockSpec((PROJ_TM, PROJ_TN_C), lambda i, j: (i, j)),
        out_shape=jax.ShapeDtypeStruct((R, W_C_COLS), F32),
        scratch_shapes=[pltpu.VMEM((PROJ_TM, D_MODEL), BF16)],
        compiler_params=_cparams(("arbitrary", "arbitrary")),
        name="proj_gates",
    )(x_all, g, w_c)


POOL_TM = 512
HALO = 16


def _window_mean_minus_cur(ext_ref, rows, inv_cnt):
    outs = []
    for g, w in enumerate(POOL_WINDOWS):
        cols = slice(g * POOL_GROUP, (g + 1) * POOL_GROUP)
        cur = ext_ref[HALO:HALO + rows, cols]
        acc = cur
        for d in range(1, w):
            acc = acc + ext_ref[HALO - d:HALO - d + rows, cols]
        outs.append(acc * inv_cnt[g] - cur)
    return outs


def _pool_prompt_kernel(u_ref, halo_ref, o_ref, ext_ref):
    i = pl.program_id(0)
    ext_ref[HALO:, :] = u_ref[...]
    ext_ref[:HALO, :] = jnp.where(i == 0, 0.0, halo_ref[...])
    pos = i * POOL_TM + lax.broadcasted_iota(I32, (POOL_TM, 1), 0)
    inv_cnt = [1.0 / jnp.minimum(pos + 1, w).astype(F32) for w in POOL_WINDOWS]
    outs = _window_mean_minus_cur(ext_ref, POOL_TM, inv_cnt)
    for g in range(N_POOL_GROUPS):
        o_ref[:, g * POOL_GROUP:(g + 1) * POOL_GROUP] = outs[g]


def _pool_prompt(u):
    per = POOL_TM // HALO
    return pl.pallas_call(
        _pool_prompt_kernel,
        grid=(R // POOL_TM,),
        in_specs=[
            pl.BlockSpec((POOL_TM, D_POOL), lambda i: (i, 0)),
            pl.BlockSpec((HALO, D_POOL), lambda i: (jnp.maximum(i * per - 1, 0), 0)),
        ],
        out_specs=pl.BlockSpec((POOL_TM, D_POOL), lambda i: (i, 0)),
        out_shape=jax.ShapeDtypeStruct((R, D_POOL), F32),
        scratch_shapes=[pltpu.VMEM((POOL_TM + HALO, D_POOL), F32)],
        compiler_params=_cparams(("arbitrary",)),
        name="pool_prompt",
    )(u, u)


def _pool_sample_kernel(u_ref, st_ref, o_ref, ext_ref):
    for b in range(DEC_BATCH):
        ext_ref[:HALO, :] = st_ref[b]
        ext_ref[HALO:, :] = u_ref[b]
        inv_cnt = [1.0 / w for w in POOL_WINDOWS]
        outs = _window_mean_minus_cur(ext_ref, DEC_SEQ, inv_cnt)
        for g in range(N_POOL_GROUPS):
            o_ref[b, :, g * POOL_GROUP:(g + 1) * POOL_GROUP] = outs[g]


def _pool_sample(u_s, state16):
    return pl.pallas_call(
        _pool_sample_kernel,
        out_shape=jax.ShapeDtypeStruct((DEC_BATCH, DEC_SEQ, D_POOL), F32),
        scratch_shapes=[pltpu.VMEM((HALO + DEC_SEQ, D_POOL), F32)],
        compiler_params=pltpu.CompilerParams(vmem_limit_bytes=VMEM_LIMIT),
        name="pool_sample",
    )(u_s, state16)


MERGE_TM = 256


def _merge_kernel(x_ref, pooled_ref, attn_ref, gate_ref, wp_ref, ps_ref, wa_ref, wo_ref, gn_ref, o_ref):
    pooled = pooled_ref[...].astype(BF16)
    pool_out = jnp.concatenate(
        [jnp.dot(pooled[:, g * POOL_GROUP:(g + 1) * POOL_GROUP], wp_ref[g], preferred_element_type=F32)
         for g in range(N_POOL_GROUPS)], axis=1) * ps_ref[...]
    attn_out = jnp.dot(attn_ref[...], wa_ref[...], preferred_element_type=F32)
    gate = gate_ref[...]
    m = (jax.nn.sigmoid(gate[:, :D_MODEL]) * pool_out
         + jax.nn.sigmoid(gate[:, D_MODEL:]) * attn_out)
    mix = jnp.dot(m.astype(BF16), wo_ref[...], preferred_element_type=F32)
    o_ref[...] = x_ref[...] + _rms(mix, gn_ref[...])


def _merge(x_all, pooled, attn, gates, w_pool, pool_scale, w_attn_o, w_out, g_post):
    row = lambda i: (i, 0)
    const2 = lambda i: (0, 0)
    return pl.pallas_call(
        _merge_kernel,
        grid=(R // MERGE_TM,),
        in_specs=[
            pl.BlockSpec((MERGE_TM, D_MODEL), row),
            pl.BlockSpec((MERGE_TM, D_POOL), row),
            pl.BlockSpec((MERGE_TM, D_ATTN), row),
            pl.BlockSpec((MERGE_TM, 2 * D_MODEL), row),
            pl.BlockSpec((N_POOL_GROUPS, POOL_GROUP, POOL_OUT_GROUP), lambda i: (0, 0, 0)),
            pl.BlockSpec((1, D_MODEL), const2),
            pl.BlockSpec((D_ATTN, D_MODEL), const2),
            pl.BlockSpec((D_MODEL, D_MODEL), const2),
            pl.BlockSpec((1, D_MODEL), const2),
        ],
        out_specs=pl.BlockSpec((MERGE_TM, D_MODEL), row),
        out_shape=jax.ShapeDtypeStruct((R, D_MODEL), F32),
        compiler_params=_cparams(("arbitrary",)),
        name="merge",
    )(x_all, pooled, attn, gates, w_pool, pool_scale, w_attn_o, w_out, g_post)


FFN_TM = 544
FFN_TF = 1024


def _ffn_kernel(x_ref, gpre_ref, wu_ref, wd_ref, gpost_ref, o_ref, h_ref, acc_ref):
    j = pl.program_id(1)

    @pl.when(j == 0)
    def _():
        h_ref[...] = _rms(x_ref[...], gpre_ref[...]).astype(BF16)
        acc_ref[...] = jnp.zeros_like(acc_ref)

    a = jnp.maximum(jnp.dot(h_ref[...], wu_ref[...], preferred_element_type=F32), 0.0)
    acc_ref[...] += jnp.dot((a * a).astype(BF16), wd_ref[...], preferred_element_type=F32)

    @pl.when(j == pl.num_programs(1) - 1)
    def _():
        o_ref[...] = x_ref[...] + _rms(acc_ref[...], gpost_ref[...])


def _ffn(x1, g_pre, w_up, w_down, g_post):
    return pl.pallas_call(
        _ffn_kernel,
        grid=(R // FFN_TM, D_FF // FFN_TF),
        in_specs=[
            pl.BlockSpec((FFN_TM, D_MODEL), lambda i, j: (i, 0)),
            pl.BlockSpec((1, D_MODEL), lambda i, j: (0, 0)),
            pl.BlockSpec((D_MODEL, FFN_TF), lambda i, j: (0, j)),
            pl.BlockSpec((FFN_TF, D_MODEL), lambda i, j: (j, 0)),
            pl.BlockSpec((1, D_MODEL), lambda i, j: (0, 0)),
        ],
        out_specs=pl.BlockSpec((FFN_TM, D_MODEL), lambda i, j: (i, 0)),
        out_shape=jax.ShapeDtypeStruct((R, D_MODEL), F32),
        scratch_shapes=[pltpu.VMEM((FFN_TM, D_MODEL), BF16), pltpu.VMEM((FFN_TM, D_MODEL), F32)],
        compiler_params=_cparams(("arbitrary", "arbitrary")),
        name="ffn",
    )(x1, g_pre, w_up, w_down, g_post)


NEG_INF = float("-inf")
MAX_BISECT = 192
STEPS_PER_TRIP = 3


def _above(x):
    return x + (jnp.abs(x) * 2.0 ** -20 + 1e-30)


def _bisect_threshold(count_ge, smin, smax, active0):
    def cond(st):
        it, _, _, active = st
        return jnp.logical_and(it < MAX_BISECT, jnp.max(active) > 0)

    def step(lo, hi, active):
        mid = 0.5 * lo + 0.5 * hi
        cnt = count_ge(mid)
        on = active > 0
        ge = cnt >= TOPK
        still = jnp.logical_and(cnt != TOPK, jnp.logical_and(mid != lo, mid != hi))
        lo = jnp.where(jnp.logical_and(on, ge), mid, lo)
        hi = jnp.where(jnp.logical_and(on, jnp.logical_not(ge)), mid, hi)
        return lo, hi, jnp.where(jnp.logical_and(on, still), 1, 0).astype(I32)

    def body(st):
        it, lo, hi, active = st
        for _ in range(STEPS_PER_TRIP):
            lo, hi, active = step(lo, hi, active)
        return it + STEPS_PER_TRIP, lo, hi, active

    _, lo, _, _ = lax.while_loop(cond, body, (jnp.int32(0), smin, _above(smax), active0.astype(I32)))
    return lo


def _head_slab(x_ref, h):
    j = h // 2
    return x_ref[:, j * LANES:(j + 1) * LANES]


def _place_half(slab, src_half, dst_half):
    lane = lax.broadcasted_iota(I32, slab.shape, 1)
    x = slab if src_half == dst_half else pltpu.roll(slab, HEAD_DIM, 1)
    return jnp.where((lane >= HEAD_DIM) == (dst_half == 1), x, 0.0)


def _gather_heads(o, rows):
    outs = []
    for j in range(N_HEADS // 2):
        n = (2 * j) // GROUP
        tc, th = n // 2, n % 2
        a0 = o[(2 * j) * rows:(2 * j + 1) * rows, tc * LANES:(tc + 1) * LANES]
        a1 = o[(2 * j + 1) * rows:(2 * j + 2) * rows, tc * LANES:(tc + 1) * LANES]
        x0 = a0 if th == 0 else pltpu.roll(a0, HEAD_DIM, 1)
        x1 = a1 if th == 1 else pltpu.roll(a1, HEAD_DIM, 1)
        lane = lax.broadcasted_iota(I32, x0.shape, 1)
        outs.append(jnp.where(lane < HEAD_DIM, x0, x1))
    return outs


def _build_qbd(q_ref, qbd_ref, rows):
    for h in range(N_HEADS):
        n = h // GROUP
        tc, th = n // 2, n % 2
        placed = _place_half(_head_slab(q_ref, h) * ATTN_SCALE, h % 2, th).astype(qbd_ref.dtype)
        for c in range(D_KV // LANES):
            qbd_ref[h * rows:(h + 1) * rows, c * LANES:(c + 1) * LANES] = (
                placed if c == tc else jnp.zeros_like(placed))


def _softmax_step(a3, m_ref, l_ref, acc_ref, vc, rows, v_transposed):
    n = N_HEADS * rows
    ch = a3.shape[2]
    m_prev = m_ref[...].reshape(N_HEADS, rows, 1)
    m_new = jnp.maximum(m_prev, jnp.max(a3, axis=2, keepdims=True))
    p = jnp.exp(a3 - m_new)
    alpha = jnp.exp(m_prev - m_new)
    l_ref[...] = (alpha * l_ref[...].reshape(N_HEADS, rows, 1)
                  + jnp.sum(p, axis=2, keepdims=True)).reshape(n, 1)
    m_ref[...] = m_new.reshape(n, 1)
    pb = p.reshape(n, ch).astype(BF16)
    if v_transposed:
        pv = lax.dot_general(pb, vc, NT_DIMS, preferred_element_type=F32)
    else:
        pv = jnp.dot(pb, vc, preferred_element_type=F32)
    acc_ref[...] = alpha.reshape(n, 1) * acc_ref[...] + pv


ATT_CH = 512
CNT_ROWS = 8 * SUBLANES
N_ATT_CH = R // ATT_CH
NT_DIMS = (((1,), (1,)), ((), ()))
BOUND_MARGIN = 1.0 + 2.0 ** -5
MIN_ROW_SUM = 1e-30


def _pair_loop(n, body, carry):
    def two(j, carry):
        return body(2 * j + 1, body(2 * j, carry, 0), 1)

    carry = lax.fori_loop(0, n // 2, two, carry)
    return lax.cond(n % 2 == 1, lambda x: body(n - 1, x, 0), lambda x: x, carry)


def _inclusive_prefix_matrix(n):
    r_i = lax.broadcasted_iota(I32, (n, n), 0)
    c_i = lax.broadcasted_iota(I32, (n, n), 1)
    return jnp.where(r_i <= c_i, 1.0, 0.0).astype(BF16)


def _attn_prompt_kernel(qi_ref, wi_ref, q_ref, kib_ref, kb_ref, vb_ref, o_ref, lmin_ref,
                        s_ref, st_ref, qim_ref, wb_ref, qbd_ref, m_ref, l_ref, acc_ref, p_ref,
                        kmax_ref, tri_ref, *, bound_max):
    i = pl.program_id(0)

    @pl.when(i >= N_QBLK)
    def _():
        o_ref[...] = jnp.zeros_like(o_ref)
        lmin_ref[...] = jnp.ones_like(lmin_ref)

    @pl.when(i == 0)
    def _():
        tri_ref[...] = _inclusive_prefix_matrix(ATT_CH)

    if bound_max:
        @pl.when(i == 0)
        def _():
            r_i = lax.broadcasted_iota(I32, (D_KV, LANES), 0)
            c_i = lax.broadcasted_iota(I32, (D_KV, LANES), 1)
            sel = jnp.where(r_i // HEAD_DIM == c_i, 1.0, 0.0).astype(BF16)

            def body(c, mx):
                kc = kb_ref[pl.ds(pl.multiple_of(c * ATT_CH, ATT_CH), ATT_CH), :].astype(F32)
                n2 = jnp.dot((kc * kc).astype(BF16), sel, preferred_element_type=F32)
                return jnp.maximum(mx, n2)

            mx = lax.fori_loop(0, N_ATT_CH, body, jnp.zeros((ATT_CH, LANES), F32))
            kmax = jnp.max(mx, axis=0, keepdims=True)
            for n in range(N_KV_HEADS):
                kmax_ref[n] = jnp.broadcast_to(kmax[:, n:n + 1], (QB, LANES))

    @pl.when(i < N_QBLK)
    def _():
        n_ch = (i * QB) // ATT_CH + 1
        qrow = i * QB + lax.broadcasted_iota(I32, (QB, 1), 0)

        wi = wi_ref[...]
        for h in range(N_IDX_HEADS):
            qim_ref[h * QB:(h + 1) * QB, :] = _place_half(_head_slab(qi_ref, h), h % 2, h % 2).astype(BF16)
            wb_ref[h] = jnp.broadcast_to(wi[:, h:h + 1], (QB, LANES))
        _build_qbd(q_ref, qbd_ref, QB)

        def score_chunk(c, carry, parity):
            start = pl.multiple_of(c * ATT_CH, ATT_CH)
            kc = kib_ref[pl.ds(start, ATT_CH), :]
            acc = [jnp.zeros((QB, LANES), F32) for _ in range(ATT_CH // LANES)]
            hpd = 4
            for hg in range(N_IDX_HEADS // hpd):
                d = lax.dot_general(qim_ref[hg * hpd * QB:(hg + 1) * hpd * QB, :], kc, NT_DIMS,
                                    preferred_element_type=F32)
                for hh in range(hpd):
                    w = wb_ref[hg * hpd + hh]
                    for t in range(ATT_CH // LANES):
                        acc[t] = acc[t] + w * jnp.maximum(
                            d[hh * QB:(hh + 1) * QB, t * LANES:(t + 1) * LANES], 0.0)
            for t in range(ATT_CH // LANES):
                kpos = start + t * LANES + lax.broadcasted_iota(I32, (QB, LANES), 1)
                sc = jnp.where(kpos <= qrow, acc[t], NEG_INF)
                s_ref[c, :, t * LANES:(t + 1) * LANES] = sc
                st_ref[pl.ds(pl.multiple_of(start + t * LANES, LANES), LANES), :] = sc.T
            return carry

        _pair_loop(n_ch, score_chunk, 0)

        def key_major_chunk(c):
            return st_ref[pl.ds(pl.multiple_of(c * ATT_CH, ATT_CH), ATT_CH), :]

        def fold(x):
            return x.reshape(ATT_CH // CNT_ROWS, CNT_ROWS, QB)

        def count(t, strict):
            def body(c, cnt):
                sc = key_major_chunk(c)
                hit = jnp.where(sc > t if strict else sc >= t, 1.0, 0.0)
                return cnt + jnp.sum(fold(hit), axis=0)

            cnt = lax.fori_loop(0, n_ch, body, jnp.zeros((CNT_ROWS, QB), F32))
            return jnp.sum(cnt, axis=0, keepdims=True)

        def extremes(c, mm):
            sc = key_major_chunk(c)
            return (jnp.minimum(mm[0], jnp.min(fold(jnp.where(sc > NEG_INF, sc, -NEG_INF)), axis=0)),
                    jnp.maximum(mm[1], jnp.max(fold(sc), axis=0)))

        smin, smax = lax.fori_loop(0, n_ch, extremes, (jnp.full((CNT_ROWS, QB), -NEG_INF, F32),
                                                       jnp.full((CNT_ROWS, QB), NEG_INF, F32)))
        smin = jnp.min(smin, axis=0, keepdims=True)
        smax = jnp.max(smax, axis=0, keepdims=True)
        qrow_l = i * QB + lax.broadcasted_iota(I32, (1, QB), 1)
        lo = _bisect_threshold(functools.partial(count, strict=False), smin, smax,
                               jnp.logical_and(qrow_l + 1 > TOPK, smax > smin))
        lo_b = jnp.broadcast_to(lo, (QB, QB)).T

        excess = count(lo, strict=False) > TOPK
        tie_quota = jnp.where(excess, TOPK - count(lo, strict=True), TOPK)
        has_excess = jnp.max(jnp.where(excess, 1, 0)) > 0
        quota_b = jnp.broadcast_to(tie_quota, (QB, QB)).T

        m_ref[...] = jnp.full(m_ref.shape, NEG, F32)
        l_ref[...] = jnp.zeros_like(l_ref)
        acc_ref[...] = jnp.zeros_like(acc_ref)
        n_t = ATT_CH // LANES

        def plain_bias(c):
            return [jnp.where(s_ref[c, :, t * LANES:(t + 1) * LANES] >= lo_b, 0.0, NEG)
                    for t in range(n_t)]

        def quota_bias(c, seen):
            keys = [s_ref[c, :, t * LANES:(t + 1) * LANES] for t in range(n_t)]
            eq = [jnp.where(k == lo_b, 1.0, 0.0) for k in keys]
            incl = jnp.dot(jnp.concatenate(eq, axis=1).astype(BF16), tri_ref[...],
                           preferred_element_type=F32)
            bias = []
            for t in range(n_t):
                rank = seen + incl[:, t * LANES:(t + 1) * LANES] - eq[t]
                keep = jnp.logical_or(keys[t] > lo_b,
                                      jnp.logical_and(keys[t] == lo_b, rank < quota_b))
                bias.append(jnp.where(keep, 0.0, NEG))
            return bias, seen + jnp.broadcast_to(incl[:, ATT_CH - 1:ATT_CH], (QB, LANES))

        def masked_logits(c, bias):
            start = pl.multiple_of(c * ATT_CH, ATT_CH)
            a = lax.dot_general(qbd_ref[...], kb_ref[pl.ds(start, ATT_CH), :], NT_DIMS,
                                preferred_element_type=F32)
            return start, [[a[h * QB:(h + 1) * QB, t * LANES:(t + 1) * LANES] + bias[t]
                            for t in range(n_t)] for h in range(N_HEADS)]

        def max_chunk(c, carry):
            _, a = masked_logits(c, plain_bias(c))
            for h in range(N_HEADS):
                rows = slice(h * QB, (h + 1) * QB)
                m = m_ref[rows, :]
                for t in range(n_t):
                    m = jnp.maximum(m, a[h][t])
                m_ref[rows, :] = m
            return carry

        if bound_max:
            for h in range(N_HEADS):
                rows = slice(h * QB, (h + 1) * QB)
                qf = qbd_ref[rows, :].astype(F32)
                qn2 = jnp.sum(qf * qf, axis=1, keepdims=True)
                m_ref[rows, :] = jnp.sqrt(jnp.broadcast_to(qn2, (QB, LANES))
                                          * kmax_ref[h // GROUP]) * BOUND_MARGIN
        else:
            lax.fori_loop(0, n_ch, max_chunk, 0)
            m_ref[...] = jnp.broadcast_to(jnp.max(m_ref[...], axis=1, keepdims=True), m_ref.shape)

        def value_chunk(c, seen, parity, with_quota):
            pb_ref = p_ref.at[parity]
            if with_quota:
                bias, seen = quota_bias(c, seen)
            else:
                bias = plain_bias(c)
            start, a = masked_logits(c, bias)
            for h in range(N_HEADS):
                rows = slice(h * QB, (h + 1) * QB)
                m = m_ref[rows, :]
                l = l_ref[rows, :]
                for t in range(n_t):
                    p = jnp.exp(a[h][t] - m)
                    l = l + p
                    pb_ref[rows, t * LANES:(t + 1) * LANES] = p.astype(BF16)
                l_ref[rows, :] = l
            acc_ref[...] += jnp.dot(pb_ref[...], vb_ref[pl.ds(start, ATT_CH), :],
                                    preferred_element_type=F32)
            return seen

        seen0 = jnp.zeros((QB, LANES), F32)

        @pl.when(has_excess)
        def _():
            lax.fori_loop(0, n_ch, functools.partial(value_chunk, parity=0, with_quota=True), seen0)

        @pl.when(jnp.logical_not(has_excess))
        def _():
            _pair_loop(n_ch, functools.partial(value_chunk, with_quota=False), seen0)

        l = jnp.sum(l_ref[...], axis=1, keepdims=True)
        lmin_ref[...] = jnp.broadcast_to(jnp.min(l, axis=0, keepdims=True), lmin_ref.shape)
        o = acc_ref[...] / l
        for j, slab in enumerate(_gather_heads(o, QB)):
            o_ref[:, j * LANES:(j + 1) * LANES] = slab.astype(BF16)


def _attn_prompt(qi, wi, q, kib, kb, vb, bound_max):
    row = lambda i: (i, 0)
    full = lambda i: (0, 0)
    nrow = N_HEADS * QB
    return pl.pallas_call(
        functools.partial(_attn_prompt_kernel, bound_max=bound_max),
        grid=(R // QB,),
        in_specs=[
            pl.BlockSpec((QB, N_IDX_HEADS * IDX_DIM), row),
            pl.BlockSpec((QB, LANES), row),
            pl.BlockSpec((QB, D_ATTN), row),
            pl.BlockSpec((R, LANES), full),
            pl.BlockSpec((R, D_KV), full),
            pl.BlockSpec((R, D_KV), full),
        ],
        out_specs=[pl.BlockSpec((QB, D_ATTN), row),
                   pl.BlockSpec((1, SUBLANES, LANES), lambda i: (i, 0, 0))],
        out_shape=(jax.ShapeDtypeStruct((R, D_ATTN), BF16),
                   jax.ShapeDtypeStruct((R // QB, SUBLANES, LANES), F32)),
        scratch_shapes=[
            pltpu.VMEM((N_ATT_CH, QB, ATT_CH), F32),
            pltpu.VMEM((R, QB), F32),
            pltpu.VMEM((nrow, LANES), BF16),
            pltpu.VMEM((N_IDX_HEADS, QB, LANES), F32),
            pltpu.VMEM((nrow, D_KV), BF16),
            pltpu.VMEM((nrow, LANES), F32),
            pltpu.VMEM((nrow, LANES), F32),
            pltpu.VMEM((nrow, D_KV), F32),
            pltpu.VMEM((2, nrow, ATT_CH), BF16),
            pltpu.VMEM((N_KV_HEADS, QB, LANES), F32),
            pltpu.VMEM((ATT_CH, ATT_CH), BF16),
        ],
        compiler_params=_cparams(("arbitrary",)),
        name="attn_prompt_bound" if bound_max else "attn_prompt",
    )(qi, wi, q, kib, kb, vb)


S_CH = 4096
S_NCH = PAST_LEN // S_CH
S_PPC = S_CH // PAGE_SIZE
DMA_UNROLL = 8
S_TIE = 512
S_ROWS = N_HEADS * DEC_SEQ


def _page_lanes(p):
    return pl.ds(pl.multiple_of(p * PAGE_SIZE, PAGE_SIZE), PAGE_SIZE)


def _kidx_page_copy(pt_ref, kidx_hbm, kbuf, sem, b, p, slot):
    return pltpu.make_async_copy(
        kidx_hbm.at[pt_ref[b, p]], kbuf.at[slot, :, _page_lanes(p)], sem.at[slot])


def _sample_score_kernel(pt_ref, qi_ref, wi_ref, kin_ref, kidx_hbm, s_ref, sn_ref,
                         kbuf, sem, knp_ref):
    b = pl.program_id(0)
    nb = pl.num_programs(0)
    slot = b % 2

    def start_batch(bb, sl):
        def body(p, c):
            _kidx_page_copy(pt_ref, kidx_hbm, kbuf, sem, bb, p, sl).start()
            return c
        lax.fori_loop(0, N_PAGES, body, 0, unroll=DMA_UNROLL)

    @pl.when(b == 0)
    def _():
        start_batch(0, 0)

    @pl.when(b + 1 < nb)
    def _():
        start_batch(b + 1, 1 - slot)

    def wait_body(p, c):
        _kidx_page_copy(pt_ref, kidx_hbm, kbuf, sem, b, p, slot).wait()
        return c
    lax.fori_loop(0, N_PAGES, wait_body, 0, unroll=DMA_UNROLL)

    qi = qi_ref[0]
    qis = jnp.concatenate(
        [qi[:, h * IDX_DIM:(h + 1) * IDX_DIM] for h in range(N_IDX_HEADS)], axis=0).astype(BF16)
    wi = wi_ref[0]
    wb = [jnp.broadcast_to(wi[:, h:h + 1], (DEC_SEQ, LANES)) for h in range(N_IDX_HEADS)]

    def head_sum(d, width):
        outs = []
        for t in range(width // LANES):
            acc = jnp.zeros((DEC_SEQ, LANES), F32)
            for h in range(N_IDX_HEADS):
                acc = acc + wb[h] * jnp.maximum(
                    d[h * DEC_SEQ:(h + 1) * DEC_SEQ, t * LANES:(t + 1) * LANES], 0.0)
            outs.append(acc)
        return outs

    def score_chunk(c, carry):
        kc = kbuf[slot, :, pl.ds(pl.multiple_of(c * S_CH, S_CH), S_CH)].astype(BF16)
        d = jnp.dot(qis, kc, preferred_element_type=F32)
        for t, acc in enumerate(head_sum(d, S_CH)):
            s_ref[c, :, t * LANES:(t + 1) * LANES] = acc
        return carry
    lax.fori_loop(0, S_NCH, score_chunk, 0)

    knp_ref[...] = jnp.zeros_like(knp_ref)
    knp_ref[:DEC_SEQ, :] = kin_ref[0][:, :IDX_DIM]
    d = lax.dot_general(qis, knp_ref[...].astype(BF16), NT_DIMS, preferred_element_type=F32)
    kpos = lax.broadcasted_iota(I32, (DEC_SEQ, LANES), 1)
    srow = lax.broadcasted_iota(I32, (DEC_SEQ, LANES), 0)
    sn_ref[...] = jnp.where(kpos <= srow, head_sum(d, LANES)[0], NEG_INF)


def _sample_scores(page_table, qi_s, wi_s, ki_s, cache_kidx):
    blk = lambda b, pt: (b, 0, 0)
    grid_spec = pltpu.PrefetchScalarGridSpec(
        num_scalar_prefetch=1,
        grid=(DEC_BATCH,),
        in_specs=[
            pl.BlockSpec((1, DEC_SEQ, N_IDX_HEADS * IDX_DIM), blk),
            pl.BlockSpec((1, DEC_SEQ, LANES), blk),
            pl.BlockSpec((1, DEC_SEQ, LANES), blk),
            pl.BlockSpec(memory_space=pl.ANY),
        ],
        out_specs=[
            pl.BlockSpec((S_NCH, DEC_SEQ, S_CH), lambda b, pt: (0, b, 0)),
            pl.BlockSpec((DEC_SEQ, LANES), lambda b, pt: (b, 0)),
        ],
        scratch_shapes=[
            pltpu.VMEM((2, IDX_DIM, PAST_LEN), F32),
            pltpu.SemaphoreType.DMA((2,)),
            pltpu.VMEM((LANES, IDX_DIM), F32),
        ],
    )
    return pl.pallas_call(
        _sample_score_kernel,
        grid_spec=grid_spec,
        out_shape=(
            jax.ShapeDtypeStruct((S_NCH, NS, S_CH), F32),
            jax.ShapeDtypeStruct((NS, LANES), F32),
        ),
        compiler_params=_cparams(("arbitrary",)),
        name="sample_scores",
    )(page_table, qi_s, wi_s, ki_s, cache_kidx)


def _sample_threshold_kernel(s_ref, sn_ref, bp_ref, bn_ref):
    def lane_fold(x, op):
        out = x[:, :LANES]
        for t in range(1, S_CH // LANES):
            out = op(out, x[:, t * LANES:(t + 1) * LANES])
        return out

    def count(t, strict):
        def hits(sc):
            return jnp.where(sc > t if strict else sc >= t, 1.0, 0.0)

        cnt = lax.fori_loop(0, S_NCH, lambda c, cnt: cnt + lane_fold(hits(s_ref[c]), jnp.add),
                            hits(sn_ref[...]))
        return jnp.sum(cnt, axis=1, keepdims=True)

    def extremes(c, mm):
        return (jnp.minimum(mm[0], lane_fold(s_ref[c], jnp.minimum)),
                jnp.maximum(mm[1], lane_fold(s_ref[c], jnp.maximum)))

    new = sn_ref[...]
    smin, smax = lax.fori_loop(0, S_NCH, extremes, (jnp.where(new > NEG_INF, new, -NEG_INF), new))
    smin = jnp.min(smin, axis=1, keepdims=True)
    smax = jnp.max(smax, axis=1, keepdims=True)
    lo = _bisect_threshold(functools.partial(count, strict=False), smin, smax, smax > smin)

    quota = jnp.where(count(lo, strict=False) > TOPK, TOPK - count(lo, strict=True), TOPK)
    tri = _inclusive_prefix_matrix(S_TIE)

    def mask_tile(keys, seen, tri_t):
        eq = jnp.where(keys == lo, 1.0, 0.0)
        incl = jnp.dot(eq.astype(BF16), tri_t, preferred_element_type=F32)
        keep = jnp.logical_or(keys > lo, jnp.logical_and(keys == lo, seen + incl - eq < quota))
        return jnp.where(keep, 0.0, NEG), seen + incl[:, keys.shape[1] - 1:]

    def chunk(c, seen):
        for t in range(S_CH // S_TIE):
            cols = slice(t * S_TIE, (t + 1) * S_TIE)
            bp_ref[c, :, cols], seen = mask_tile(s_ref[c, :, cols], seen, tri)
        return seen

    seen = lax.fori_loop(0, S_NCH, chunk, jnp.zeros((NS, 1), F32))
    bn_ref[...], _ = mask_tile(sn_ref[...], seen, tri[:LANES, :LANES])


def _sample_threshold(keys_past, keys_new):
    vmem = pl.BlockSpec(memory_space=pltpu.VMEM)
    return pl.pallas_call(
        _sample_threshold_kernel,
        in_specs=[vmem, vmem],
        out_specs=[vmem, vmem],
        out_shape=(
            jax.ShapeDtypeStruct((S_NCH, NS, S_CH), F32),
            jax.ShapeDtypeStruct((NS, LANES), F32),
        ),
        compiler_params=pltpu.CompilerParams(vmem_limit_bytes=VMEM_LIMIT),
        name="sample_threshold",
    )(keys_past, keys_new)


def _kv_page_copies(pt_ref, ck_hbm, cv_hbm, kbuf, vbuf, sem, b, c, p, slot):
    page = pt_ref[b, c * S_PPC + p]
    dst = _page_lanes(p)
    return (pltpu.make_async_copy(ck_hbm.at[page], kbuf.at[slot, :, dst], sem.at[0, slot]),
            pltpu.make_async_copy(cv_hbm.at[page], vbuf.at[slot, :, dst], sem.at[1, slot]))


def _sample_attn_kernel(pt_ref, q_ref, bp_ref, bn_ref, kn_ref, vn_ref, ck_hbm, cv_hbm, o_ref,
                        kbuf, vbuf, sem, qbd_ref, m_ref, l_ref, acc_ref, knp_ref, vnp_ref):
    b = pl.program_id(0)
    c = pl.program_id(1)
    step = b * S_NCH + c
    n_steps = pl.num_programs(0) * S_NCH
    slot = step % 2

    def start_chunk(st, sl):
        bb = st // S_NCH
        cc = st % S_NCH

        def body(p, carry):
            for cp in _kv_page_copies(pt_ref, ck_hbm, cv_hbm, kbuf, vbuf, sem, bb, cc, p, sl):
                cp.start()
            return carry
        lax.fori_loop(0, S_PPC, body, 0, unroll=DMA_UNROLL)

    @pl.when(step == 0)
    def _():
        start_chunk(0, 0)

    @pl.when(step + 1 < n_steps)
    def _():
        start_chunk(step + 1, 1 - slot)

    def wait_body(p, carry):
        for cp in _kv_page_copies(pt_ref, ck_hbm, cv_hbm, kbuf, vbuf, sem, b, c, p, slot):
            cp.wait()
        return carry
    lax.fori_loop(0, S_PPC, wait_body, 0, unroll=DMA_UNROLL)

    @pl.when(c == 0)
    def _():
        _build_qbd(q_ref.at[0], qbd_ref, DEC_SEQ)
        m_ref[...] = jnp.full(m_ref.shape, NEG, F32)
        l_ref[...] = jnp.zeros_like(l_ref)
        acc_ref[...] = jnp.zeros_like(acc_ref)

    def update(kc, vc, bias, feature_major):
        qbd = qbd_ref[...].astype(BF16)
        if feature_major:
            a = jnp.dot(qbd, kc, preferred_element_type=F32)
        else:
            a = lax.dot_general(qbd, kc, NT_DIMS, preferred_element_type=F32)
        a3 = a.reshape(N_HEADS, DEC_SEQ, a.shape[1]) + bias[None]
        _softmax_step(a3, m_ref, l_ref, acc_ref, vc, DEC_SEQ, feature_major)

    update(kbuf[slot].astype(BF16), vbuf[slot].astype(BF16), bp_ref[0], True)

    @pl.when(c == S_NCH - 1)
    def _():
        knp_ref[...] = jnp.zeros_like(knp_ref)
        vnp_ref[...] = jnp.zeros_like(vnp_ref)
        knp_ref[:DEC_SEQ, :] = kn_ref[0]
        vnp_ref[:DEC_SEQ, :] = vn_ref[0]
        update(knp_ref[...].astype(BF16), vnp_ref[...].astype(BF16), bn_ref[...], False)
        o = acc_ref[...] / l_ref[...]
        for j, slab in enumerate(_gather_heads(o, DEC_SEQ)):
            o_ref[0, :, j * LANES:(j + 1) * LANES] = slab


def _sample_attn(page_table, q_s, bias_past, bias_new, k_s, v_s, cache_k, cache_v):
    blk = lambda b, c, pt: (b, 0, 0)
    grid_spec = pltpu.PrefetchScalarGridSpec(
        num_scalar_prefetch=1,
        grid=(DEC_BATCH, S_NCH),
        in_specs=[
            pl.BlockSpec((1, DEC_SEQ, D_ATTN), blk),
            pl.BlockSpec((1, DEC_SEQ, S_CH), lambda b, c, pt: (c, b, 0)),
            pl.BlockSpec((DEC_SEQ, LANES), lambda b, c, pt: (b, 0)),
            pl.BlockSpec((1, DEC_SEQ, D_KV), blk),
            pl.BlockSpec((1, DEC_SEQ, D_KV), blk),
            pl.BlockSpec(memory_space=pl.ANY),
            pl.BlockSpec(memory_space=pl.ANY),
        ],
        out_specs=pl.BlockSpec((1, DEC_SEQ, D_ATTN), blk),
        scratch_shapes=[
            pltpu.VMEM((2, D_KV, S_CH), F32),
            pltpu.VMEM((2, D_KV, S_CH), F32),
            pltpu.SemaphoreType.DMA((2, 2)),
            pltpu.VMEM((S_ROWS, D_KV), F32),
            pltpu.VMEM((S_ROWS, 1), F32),
            pltpu.VMEM((S_ROWS, 1), F32),
            pltpu.VMEM((S_ROWS, D_KV), F32),
            pltpu.VMEM((LANES, D_KV), F32),
            pltpu.VMEM((LANES, D_KV), F32),
        ],
    )
    return pl.pallas_call(
        _sample_attn_kernel,
        grid_spec=grid_spec,
        out_shape=jax.ShapeDtypeStruct((DEC_BATCH, DEC_SEQ, D_ATTN), F32),
        compiler_params=_cparams(("arbitrary", "arbitrary")),
        name="sample_attn",
    )(page_table, q_s, bias_past, bias_new, k_s, v_s, cache_k, cache_v)


def _rope_tables():
    pos = np.zeros((R,), np.float32)
    pos[:T] = np.arange(T)
    pos[ROW_S:ROW_S + NS] = np.tile(PAST_LEN + np.arange(DEC_SEQ), DEC_BATCH)
    half = HEAD_DIM // 2
    inv_freq = ROPE_THETA ** (-(jnp.arange(half, dtype=F32) * 2.0 / HEAD_DIM))
    ang = jnp.asarray(pos)[:, None] * inv_freq
    cos = jnp.tile(jnp.cos(ang), (1, LANES // half))
    sin = jnp.sin(ang)
    sin = jnp.tile(jnp.concatenate([-sin, sin], axis=1), (1, LANES // HEAD_DIM))
    return cos, sin


def kernel(x_prompt, x_sample, cache_k, cache_v, cache_kidx, state_pool, page_table, meta_tokens,
           norm_mix_pre, w_in, idx_k_norm, w_pool, pool_scale, w_attn_o, w_out, norm_mix_post,
           norm_ffn_pre, w_up, w_down, norm_ffn_post):
    x_all = jnp.concatenate([
        meta_tokens, x_prompt[0], jnp.zeros((ROW_S - T, D_MODEL), F32),
        x_sample.reshape(NS, D_MODEL), jnp.zeros((R - ROW_S - NS, D_MODEL), F32)], axis=0)

    wt = jnp.transpose(w_in[0])
    o_u, o_q, o_k, o_v, o_qi, o_ki, o_wi, o_gp = np.cumsum((0, D_POOL, D_ATTN, D_KV, D_KV,
                                                            N_IDX_HEADS * IDX_DIM, IDX_DIM, N_IDX_HEADS))
    n_pad = W_ROW_C - W_B_COLS - (o_v - o_q) - (o_gp - o_qi)
    w_all = jnp.concatenate([
        wt[o_v:o_qi], wt[o_u:o_q], wt[o_q:o_v], wt[o_qi:o_gp],
        jnp.zeros((n_pad, D_MODEL), F32), wt[o_gp:]], axis=0).astype(BF16)
    assert w_all.shape == (W_ROWS, D_MODEL)

    g_pre = norm_mix_pre[0][None]
    cos, sin = _rope_tables()
    gk = jnp.tile(idx_k_norm[0], LANES // IDX_DIM)[None]

    q, k, kb, qi, ki, kib, wi = _proj_a(x_all, g_pre, w_all, cos, sin, gk)
    v, vb, u = _proj_b(x_all, g_pre, w_all)
    gates = _proj_c(x_all, g_pre, w_all)

    def sample_rows(a):
        return a[ROW_S:ROW_S + NS].reshape(DEC_BATCH, DEC_SEQ, a.shape[1])

    pooled = _pool_prompt(u)
    u_s = sample_rows(u)
    state = state_pool[0]
    state16 = jnp.pad(state, ((0, 0), (HALO - POOL_BUF, 0), (0, 0)))
    pooled_s = _pool_sample(u_s, state16)
    pooled = lax.dynamic_update_slice(pooled, pooled_s.reshape(NS, D_POOL), (ROW_S, 0))

    attn, lmin = _attn_prompt(qi, wi, q, kib, kb, vb, bound_max=True)
    attn = lax.cond(jnp.min(lmin) >= MIN_ROW_SUM, lambda: attn,
                    lambda: _attn_prompt(qi, wi, q, kib, kb, vb, bound_max=False)[0])
    k_s, v_s = sample_rows(k), sample_rows(v)
    n_pool = cache_k.shape[1]
    kidx_fm = jnp.transpose(cache_kidx[0], (0, 2, 1))
    k_fm = jnp.transpose(cache_k[0], (0, 2, 3, 1)).reshape(n_pool, D_KV, PAGE_SIZE)
    v_fm = jnp.transpose(cache_v[0], (0, 2, 3, 1)).reshape(n_pool, D_KV, PAGE_SIZE)
    keys_past, keys_new = _sample_scores(page_table, sample_rows(qi), sample_rows(wi), sample_rows(ki),
                                         kidx_fm)
    bias_past, bias_new = _sample_threshold(keys_past, keys_new)
    attn_s = _sample_attn(page_table, sample_rows(q), bias_past, bias_new, k_s, v_s, k_fm, v_fm)
    attn = lax.dynamic_update_slice(attn, attn_s.reshape(NS, D_ATTN).astype(BF16), (ROW_S, 0))

    x1 = _merge(x_all, pooled, attn, gates, w_pool[0].astype(BF16), pool_scale[0][None],
                w_attn_o[0].astype(BF16), w_out[0].astype(BF16), norm_mix_post[0][None])
    y = _ffn(x1, norm_ffn_pre[0][None], w_up[0].astype(BF16), w_down[0].astype(BF16),
             norm_ffn_post[0][None])

    y_prompt = y[N_META:T][None]
    y_sample = y[ROW_S:ROW_S + NS].reshape(DEC_BATCH, DEC_SEQ, D_MODEL)
    kv_shape = (1, 1, T, N_KV_HEADS, HEAD_DIM)
    kv_s_shape = (1, DEC_BATCH, DEC_SEQ, N_KV_HEADS, HEAD_DIM)
    return (
        y_prompt, y_sample,
        k[:T].reshape(kv_shape), v[:T].reshape(kv_shape), ki[:T, :IDX_DIM][None, None],
        u[T - POOL_BUF:T][None, None],
        k_s.reshape(kv_s_shape), v_s.reshape(kv_s_shape), sample_rows(ki)[None, :, :, :IDX_DIM],
        jnp.concatenate([state[:, DEC_SEQ:], u_s], axis=1)[None],
    )
```

```python
import functools

import jax
import jax.numpy as jnp
import numpy as np
from jax import lax
from jax.experimental import pallas as pl
from jax.experimental.pallas import tpu as pltpu

F32 = jnp.float32
BF16 = jnp.bfloat16
I32 = jnp.int32

D_MODEL = 2048
SEQ = 8192
DEC_BATCH = 32
DEC_SEQ = 8
PAST_LEN = 16384
PAGE_SIZE = 128
N_PAGES = PAST_LEN // PAGE_SIZE
N_META = 16
N_HEADS = 16
N_KV_HEADS = 4
HEAD_DIM = 64
GROUP = N_HEADS // N_KV_HEADS
D_ATTN = N_HEADS * HEAD_DIM
D_KV = N_KV_HEADS * HEAD_DIM
ATTN_SCALE = HEAD_DIM ** -0.5
N_IDX_HEADS = 16
IDX_DIM = 64
INDEX_W_SCALE = (N_IDX_HEADS ** -0.5) * (IDX_DIM ** -0.5)
TOPK = 256
POOL_WINDOWS = (2, 4, 8, 16)
N_POOL_GROUPS = 4
D_POOL = D_MODEL // 2
POOL_GROUP = D_POOL // N_POOL_GROUPS
POOL_OUT_GROUP = D_MODEL // N_POOL_GROUPS
POOL_BUF = max(POOL_WINDOWS) - 1
D_FF = 4 * D_MODEL
ROPE_THETA = 10000.0
EPS = 1e-6

LANES = 128
SUBLANES = 8
T = SEQ + N_META
QB = 128
N_QBLK = -(-T // QB)
ROW_S = N_QBLK * QB
NS = DEC_BATCH * DEC_SEQ
R = 8704
VMEM_LIMIT = 56 * 1024 * 1024

NEG = -1e30


def _cparams(sem):
    return pltpu.CompilerParams(dimension_semantics=sem, vmem_limit_bytes=VMEM_LIMIT)


def _rms(x, g):
    return x * lax.rsqrt(jnp.mean(x * x, axis=-1, keepdims=True) + EPS) * g


def _swap_halves(x):
    lane = lax.broadcasted_iota(I32, x.shape, 1)
    return jnp.where(lane % HEAD_DIM < HEAD_DIM // 2,
                     pltpu.roll(x, LANES - HEAD_DIM // 2, 1),
                     pltpu.roll(x, HEAD_DIM // 2, 1))


def _rope_cols(x, cos, sin):
    outs = []
    for c in range(x.shape[1] // LANES):
        xc = x[:, c * LANES:(c + 1) * LANES]
        outs.append(xc * cos + _swap_halves(xc) * sin)
    return outs


PROJ_TM = 544
N_ROPE_A = D_ATTN + D_KV
PROJ_TN_A = N_ROPE_A
W_A_COLS = 2 * PROJ_TN_A
W_B_COLS = D_KV + D_POOL
W_C_COLS = 2 * D_MODEL
PROJ_TN_C = 1024
PROJ_TM_C = 1088
W_ROW_B = 0
W_ROW_A = W_B_COLS
W_ROW_C = 4 * PROJ_TN_C
W_ROWS = W_ROW_C + W_C_COLS
assert W_ROW_A % PROJ_TN_A == 0 and W_ROW_A + W_A_COLS <= W_ROW_C


def _norm_to_scratch(x_ref, g_ref, xn_ref):
    @pl.when(pl.program_id(1) == 0)
    def _():
        xn_ref[...] = _rms(x_ref[...], g_ref[...]).astype(BF16)


def _proj_dot(xn_ref, w_ref):
    return lax.dot_general(xn_ref[...], w_ref[...], (((1,), (1,)), ((), ())),
                           preferred_element_type=F32)


def _proj_a_kernel(x_ref, g_ref, w_ref, cos_ref, sin_ref, gk_ref,
                   q_ref, k_ref, kb_ref, qi_ref, ki_ref, kib_ref, wi_ref, xn_ref):
    _norm_to_scratch(x_ref, g_ref, xn_ref)
    j = pl.program_id(1)
    p = _proj_dot(xn_ref, w_ref)
    cos = cos_ref[...]
    sin = sin_ref[...]

    @pl.when(j == 0)
    def _():
        cols = _rope_cols(p, cos, sin)
        nq = D_ATTN // LANES
        for c in range(nq):
            q_ref[:, c * LANES:(c + 1) * LANES] = cols[c]
        for c in range(D_KV // LANES):
            k_ref[:, c * LANES:(c + 1) * LANES] = cols[nq + c]
            kb_ref[:, c * LANES:(c + 1) * LANES] = cols[nq + c].astype(BF16)

    @pl.when(j == 1)
    def _():
        nqi = N_IDX_HEADS * IDX_DIM // LANES
        cols = _rope_cols(p[:, :nqi * LANES], cos, sin)
        for c in range(nqi):
            qi_ref[:, c * LANES:(c + 1) * LANES] = cols[c]
        slab = p[:, nqi * LANES:(nqi + 1) * LANES]
        lane = lax.broadcasted_iota(I32, slab.shape, 1)
        is_ki = lane < IDX_DIM
        ms = jnp.sum(jnp.where(is_ki, slab * slab, 0.0), axis=-1, keepdims=True) / IDX_DIM
        kin = slab * lax.rsqrt(ms + EPS) * gk_ref[...]
        kin = kin * cos + _swap_halves(kin) * sin
        ki2 = jnp.where(is_ki, kin, pltpu.roll(kin, IDX_DIM, 1))
        ki_ref[...] = ki2
        kib_ref[...] = ki2.astype(BF16)
        wi_ref[...] = pltpu.roll(slab, LANES - IDX_DIM, 1) * INDEX_W_SCALE


def _proj_a(x_all, g, w_a, cos, sin, gk):
    n_i = R // PROJ_TM
    row = lambda i, j: (i, 0)
    outs = (
        jax.ShapeDtypeStruct((R, D_ATTN), F32),
        jax.ShapeDtypeStruct((R, D_KV), F32),
        jax.ShapeDtypeStruct((R, D_KV), BF16),
        jax.ShapeDtypeStruct((R, N_IDX_HEADS * IDX_DIM), F32),
        jax.ShapeDtypeStruct((R, LANES), F32),
        jax.ShapeDtypeStruct((R, LANES), BF16),
        jax.ShapeDtypeStruct((R, LANES), F32),
    )
    return pl.pallas_call(
        _proj_a_kernel,
        grid=(n_i, 2),
        in_specs=[
            pl.BlockSpec((PROJ_TM, D_MODEL), row),
            pl.BlockSpec((1, D_MODEL), lambda i, j: (0, 0)),
            pl.BlockSpec((PROJ_TN_A, D_MODEL), lambda i, j: (W_ROW_A // PROJ_TN_A + j, 0)),
            pl.BlockSpec((PROJ_TM, LANES), row),
            pl.BlockSpec((PROJ_TM, LANES), row),
            pl.BlockSpec((1, LANES), lambda i, j: (0, 0)),
        ],
        out_specs=[pl.BlockSpec((PROJ_TM, o.shape[1]), row) for o in outs],
        out_shape=outs,
        scratch_shapes=[pltpu.VMEM((PROJ_TM, D_MODEL), BF16)],
        compiler_params=_cparams(("arbitrary", "arbitrary")),
        name="proj_rope",
    )(x_all, g, w_a, cos, sin, gk)


def _proj_b_kernel(x_ref, g_ref, w_ref, v_ref, vb_ref, u_ref, xn_ref):
    _norm_to_scratch(x_ref, g_ref, xn_ref)
    p = _proj_dot(xn_ref, w_ref)
    v_ref[...] = p[:, :D_KV]
    vb_ref[...] = p[:, :D_KV].astype(BF16)
    u_ref[...] = p[:, D_KV:]


def _proj_b(x_all, g, w_b):
    n_i = R // PROJ_TM
    row = lambda i, j: (i, 0)
    outs = (
        jax.ShapeDtypeStruct((R, D_KV), F32),
        jax.ShapeDtypeStruct((R, D_KV), BF16),
        jax.ShapeDtypeStruct((R, D_POOL), F32),
    )
    return pl.pallas_call(
        _proj_b_kernel,
        grid=(n_i, 1),
        in_specs=[
            pl.BlockSpec((PROJ_TM, D_MODEL), row),
            pl.BlockSpec((1, D_MODEL), lambda i, j: (0, 0)),
            pl.BlockSpec((W_B_COLS, D_MODEL), lambda i, j: (W_ROW_B // W_B_COLS, 0)),
        ],
        out_specs=[pl.BlockSpec((PROJ_TM, o.shape[1]), row) for o in outs],
        out_shape=outs,
        scratch_shapes=[pltpu.VMEM((PROJ_TM, D_MODEL), BF16)],
        compiler_params=_cparams(("arbitrary", "arbitrary")),
        name="proj_vu",
    )(x_all, g, w_b)


def _proj_c_kernel(x_ref, g_ref, w_ref, o_ref, xn_ref):
    _norm_to_scratch(x_ref, g_ref, xn_ref)
    o_ref[...] = _proj_dot(xn_ref, w_ref)


def _proj_c(x_all, g, w_c):
    n_i = R // PROJ_TM_C
    return pl.pallas_call(
        _proj_c_kernel,
        grid=(n_i, W_C_COLS // PROJ_TN_C),
        in_specs=[
            pl.BlockSpec((PROJ_TM_C, D_MODEL), lambda i, j: (i, 0)),
            pl.BlockSpec((1, D_MODEL), lambda i, j: (0, 0)),
            pl.BlockSpec((PROJ_TN_C, D_MODEL), lambda i, j: (W_ROW_C // PROJ_TN_C + j, 0)),
        ],
        out_specs=pl.BlockSpec((PROJ_TM_C, PROJ_TN_C), lambda i, j: (i, j)),
        out_shape=jax.ShapeDtypeStruct((R, W_C_COLS), F32),
        scratch_shapes=[pltpu.VMEM((PROJ_TM_C, D_MODEL), BF16)],
        compiler_params=_cparams(("arbitrary", "arbitrary")),
        name="proj_gates",
    )(x_all, g, w_c)


POOL_TM = 512
HALO = 16


def _window_mean_minus_cur(ext_ref, rows, inv_cnt):
    outs = []
    for g, w in enumerate(POOL_WINDOWS):
        cols = slice(g * POOL_GROUP, (g + 1) * POOL_GROUP)
        cur = ext_ref[HALO:HALO + rows, cols]
        acc = cur
        for d in range(1, w):
            acc = acc + ext_ref[HALO - d:HALO - d + rows, cols]
        outs.append(acc * inv_cnt[g] - cur)
    return outs


def _pool_prompt_kernel(u_ref, halo_ref, o_ref, ext_ref):
    i = pl.program_id(0)
    ext_ref[HALO:, :] = u_ref[...]
    ext_ref[:HALO, :] = jnp.where(i == 0, 0.0, halo_ref[...])
    pos = i * POOL_TM + lax.broadcasted_iota(I32, (POOL_TM, 1), 0)
    inv_cnt = [1.0 / jnp.minimum(pos + 1, w).astype(F32) for w in POOL_WINDOWS]
    outs = _window_mean_minus_cur(ext_ref, POOL_TM, inv_cnt)
    for g in range(N_POOL_GROUPS):
        o_ref[:, g * POOL_GROUP:(g + 1) * POOL_GROUP] = outs[g]


def _pool_prompt(u):
    per = POOL_TM // HALO
    return pl.pallas_call(
        _pool_prompt_kernel,
        grid=(R // POOL_TM,),
        in_specs=[
            pl.BlockSpec((POOL_TM, D_POOL), lambda i: (i, 0)),
            pl.BlockSpec((HALO, D_POOL), lambda i: (jnp.maximum(i * per - 1, 0), 0)),
        ],
        out_specs=pl.BlockSpec((POOL_TM, D_POOL), lambda i: (i, 0)),
        out_shape=jax.ShapeDtypeStruct((R, D_POOL), F32),
        scratch_shapes=[pltpu.VMEM((POOL_TM + HALO, D_POOL), F32)],
        compiler_params=_cparams(("arbitrary",)),
        name="pool_prompt",
    )(u, u)


def _pool_sample_kernel(u_ref, st_ref, o_ref, ext_ref):
    for b in range(DEC_BATCH):
        ext_ref[:HALO, :] = st_ref[b]
        ext_ref[HALO:, :] = u_ref[b]
        inv_cnt = [1.0 / w for w in POOL_WINDOWS]
        outs = _window_mean_minus_cur(ext_ref, DEC_SEQ, inv_cnt)
        for g in range(N_POOL_GROUPS):
            o_ref[b, :, g * POOL_GROUP:(g + 1) * POOL_GROUP] = outs[g]


def _pool_sample(u_s, state16):
    return pl.pallas_call(
        _pool_sample_kernel,
        out_shape=jax.ShapeDtypeStruct((DEC_BATCH, DEC_SEQ, D_POOL), F32),
        scratch_shapes=[pltpu.VMEM((HALO + DEC_SEQ, D_POOL), F32)],
        compiler_params=pltpu.CompilerParams(vmem_limit_bytes=VMEM_LIMIT),
        name="pool_sample",
    )(u_s, state16)


MERGE_TM = 256


def _merge_kernel(x_ref, pooled_ref, attn_ref, gate_ref, wp_ref, ps_ref, wa_ref, wo_ref, gn_ref, o_ref):
    pooled = pooled_ref[...].astype(BF16)
    pool_out = jnp.concatenate(
        [jnp.dot(pooled[:, g * POOL_GROUP:(g + 1) * POOL_GROUP], wp_ref[g], preferred_element_type=F32)
         for g in range(N_POOL_GROUPS)], axis=1) * ps_ref[...]
    attn_out = jnp.dot(attn_ref[...], wa_ref[...], preferred_element_type=F32)
    gate = gate_ref[...]
    m = (jax.nn.sigmoid(gate[:, :D_MODEL]) * pool_out
         + jax.nn.sigmoid(gate[:, D_MODEL:]) * attn_out)
    mix = jnp.dot(m.astype(BF16), wo_ref[...], preferred_element_type=F32)
    o_ref[...] = x_ref[...] + _rms(mix, gn_ref[...])


def _merge(x_all, pooled, attn, gates, w_pool, pool_scale, w_attn_o, w_out, g_post):
    row = lambda i: (i, 0)
    const2 = lambda i: (0, 0)
    return pl.pallas_call(
        _merge_kernel,
        grid=(R // MERGE_TM,),
        in_specs=[
            pl.BlockSpec((MERGE_TM, D_MODEL), row),
            pl.BlockSpec((MERGE_TM, D_POOL), row),
            pl.BlockSpec((MERGE_TM, D_ATTN), row),
            pl.BlockSpec((MERGE_TM, 2 * D_MODEL), row),
            pl.BlockSpec((N_POOL_GROUPS, POOL_GROUP, POOL_OUT_GROUP), lambda i: (0, 0, 0)),
            pl.BlockSpec((1, D_MODEL), const2),
            pl.BlockSpec((D_ATTN, D_MODEL), const2),
            pl.BlockSpec((D_MODEL, D_MODEL), const2),
            pl.BlockSpec((1, D_MODEL), const2),
        ],
        out_specs=pl.BlockSpec((MERGE_TM, D_MODEL), row),
        out_shape=jax.ShapeDtypeStruct((R, D_MODEL), F32),
        compiler_params=_cparams(("arbitrary",)),
        name="merge",
    )(x_all, pooled, attn, gates, w_pool, pool_scale, w_attn_o, w_out, g_post)


FFN_TM = 544
FFN_TF = 1024


def _ffn_kernel(x_ref, gpre_ref, wu_ref, wd_ref, gpost_ref, o_ref, h_ref, acc_ref):
    j = pl.program_id(1)

    @pl.when(j == 0)
    def _():
        h_ref[...] = _rms(x_ref[...], gpre_ref[...]).astype(BF16)
        acc_ref[...] = jnp.zeros_like(acc_ref)

    a = jnp.maximum(jnp.dot(h_ref[...], wu_ref[...], preferred_element_type=F32), 0.0)
    acc_ref[...] += jnp.dot((a * a).astype(BF16), wd_ref[...], preferred_element_type=F32)

    @pl.when(j == pl.num_programs(1) - 1)
    def _():
        o_ref[...] = x_ref[...] + _rms(acc_ref[...], gpost_ref[...])


def _ffn(x1, g_pre, w_up, w_down, g_post):
    return pl.pallas_call(
        _ffn_kernel,
        grid=(R // FFN_TM, D_FF // FFN_TF),
        in_specs=[
            pl.BlockSpec((FFN_TM, D_MODEL), lambda i, j: (i, 0)),
            pl.BlockSpec((1, D_MODEL), lambda i, j: (0, 0)),
            pl.BlockSpec((D_MODEL, FFN_TF), lambda i, j: (0, j)),
            pl.BlockSpec((FFN_TF, D_MODEL), lambda i, j: (j, 0)),
            pl.BlockSpec((1, D_MODEL), lambda i, j: (0, 0)),
        ],
        out_specs=pl.BlockSpec((FFN_TM, D_MODEL), lambda i, j: (i, 0)),
        out_shape=jax.ShapeDtypeStruct((R, D_MODEL), F32),
        scratch_shapes=[pltpu.VMEM((FFN_TM, D_MODEL), BF16), pltpu.VMEM((FFN_TM, D_MODEL), F32)],
        compiler_params=_cparams(("arbitrary", "arbitrary")),
        name="ffn",
    )(x1, g_pre, w_up, w_down, g_post)


NEG_INF = float("-inf")
MAX_BISECT = 192
STEPS_PER_TRIP = 3


def _above(x):
    return x + (jnp.abs(x) * 2.0 ** -20 + 1e-30)


def _bisect_threshold(count_ge, smin, smax, active0):
    def cond(st):
        it, _, _, active = st
        return jnp.logical_and(it < MAX_BISECT, jnp.max(active) > 0)

    def step(lo, hi, active):
        mid = 0.5 * lo + 0.5 * hi
        cnt = count_ge(mid)
        on = active > 0
        ge = cnt >= TOPK
        still = jnp.logical_and(cnt != TOPK, jnp.logical_and(mid != lo, mid != hi))
        lo = jnp.where(jnp.logical_and(on, ge), mid, lo)
        hi = jnp.where(jnp.logical_and(on, jnp.logical_not(ge)), mid, hi)
        return lo, hi, jnp.where(jnp.logical_and(on, still), 1, 0).astype(I32)

    def body(st):
        it, lo, hi, active = st
        for _ in range(STEPS_PER_TRIP):
            lo, hi, active = step(lo, hi, active)
        return it + STEPS_PER_TRIP, lo, hi, active

    _, lo, _, _ = lax.while_loop(cond, body, (jnp.int32(0), smin, _above(smax), active0.astype(I32)))
    return lo


def _head_slab(x_ref, h):
    j = h // 2
    return x_ref[:, j * LANES:(j + 1) * LANES]


def _place_half(slab, src_half, dst_half):
    lane = lax.broadcasted_iota(I32, slab.shape, 1)
    x = slab if src_half == dst_half else pltpu.roll(slab, HEAD_DIM, 1)
    return jnp.where((lane >= HEAD_DIM) == (dst_half == 1), x, 0.0)


def _gather_heads(o, rows):
    outs = []
    for j in range(N_HEADS // 2):
        n = (2 * j) // GROUP
        tc, th = n // 2, n % 2
        a0 = o[(2 * j) * rows:(2 * j + 1) * rows, tc * LANES:(tc + 1) * LANES]
        a1 = o[(2 * j + 1) * rows:(2 * j + 2) * rows, tc * LANES:(tc + 1) * LANES]
        x0 = a0 if th == 0 else pltpu.roll(a0, HEAD_DIM, 1)
        x1 = a1 if th == 1 else pltpu.roll(a1, HEAD_DIM, 1)
        lane = lax.broadcasted_iota(I32, x0.shape, 1)
        outs.append(jnp.where(lane < HEAD_DIM, x0, x1))
    return outs


def _build_qbd(q_ref, qbd_ref, rows):
    for h in range(N_HEADS):
        n = h // GROUP
        tc, th = n // 2, n % 2
        placed = _place_half(_head_slab(q_ref, h) * ATTN_SCALE, h % 2, th).astype(qbd_ref.dtype)
        for c in range(D_KV // LANES):
            qbd_ref[h * rows:(h + 1) * rows, c * LANES:(c + 1) * LANES] = (
                placed if c == tc else jnp.zeros_like(placed))


def _softmax_step(a3, m_ref, l_ref, acc_ref, vc, rows, v_transposed):
    n = N_HEADS * rows
    ch = a3.shape[2]
    m_prev = m_ref[...].reshape(N_HEADS, rows, 1)
    m_new = jnp.maximum(m_prev, jnp.max(a3, axis=2, keepdims=True))
    p = jnp.exp(a3 - m_new)
    alpha = jnp.exp(m_prev - m_new)
    l_ref[...] = (alpha * l_ref[...].reshape(N_HEADS, rows, 1)
                  + jnp.sum(p, axis=2, keepdims=True)).reshape(n, 1)
    m_ref[...] = m_new.reshape(n, 1)
    pb = p.reshape(n, ch).astype(BF16)
    if v_transposed:
        pv = lax.dot_general(pb, vc, NT_DIMS, preferred_element_type=F32)
    else:
        pv = jnp.dot(pb, vc, preferred_element_type=F32)
    acc_ref[...] = alpha.reshape(n, 1) * acc_ref[...] + pv


ATT_CH = 512
CNT_ROWS = 8 * SUBLANES
N_ATT_CH = R // ATT_CH
NT_DIMS = (((1,), (1,)), ((), ()))
BOUND_MARGIN = 1.0 + 2.0 ** -5
MIN_ROW_SUM = 1e-30


def _pair_loop(n, body, carry):
    def two(j, carry):
        return body(2 * j + 1, body(2 * j, carry, 0), 1)

    carry = lax.fori_loop(0, n // 2, two, carry)
    return lax.cond(n % 2 == 1, lambda x: body(n - 1, x, 0), lambda x: x, carry)


def _inclusive_prefix_matrix(n):
    r_i = lax.broadcasted_iota(I32, (n, n), 0)
    c_i = lax.broadcasted_iota(I32, (n, n), 1)
    return jnp.where(r_i <= c_i, 1.0, 0.0).astype(BF16)


def _attn_prompt_kernel(qi_ref, wi_ref, q_ref, kib_ref, kb_ref, vb_ref, o_ref, lmin_ref,
                        s_ref, st_ref, qim_ref, wb_ref, qbd_ref, m_ref, l_ref, acc_ref, p_ref,
                        kmax_ref, tri_ref, *, bound_max):
    i = pl.program_id(0)

    @pl.when(i >= N_QBLK)
    def _():
        o_ref[...] = jnp.zeros_like(o_ref)
        lmin_ref[...] = jnp.ones_like(lmin_ref)

    @pl.when(i == 0)
    def _():
        tri_ref[...] = _inclusive_prefix_matrix(ATT_CH)

    if bound_max:
        @pl.when(i == 0)
        def _():
            r_i = lax.broadcasted_iota(I32, (D_KV, LANES), 0)
            c_i = lax.broadcasted_iota(I32, (D_KV, LANES), 1)
            sel = jnp.where(r_i // HEAD_DIM == c_i, 1.0, 0.0).astype(BF16)

            def body(c, mx):
                kc = kb_ref[pl.ds(pl.multiple_of(c * ATT_CH, ATT_CH), ATT_CH), :].astype(F32)
                n2 = jnp.dot((kc * kc).astype(BF16), sel, preferred_element_type=F32)
                return jnp.maximum(mx, n2)

            mx = lax.fori_loop(0, N_ATT_CH, body, jnp.zeros((ATT_CH, LANES), F32))
            kmax = jnp.max(mx, axis=0, keepdims=True)
            for n in range(N_KV_HEADS):
                kmax_ref[n] = jnp.broadcast_to(kmax[:, n:n + 1], (QB, LANES))

    @pl.when(i < N_QBLK)
    def _():
        n_ch = (i * QB) // ATT_CH + 1
        qrow = i * QB + lax.broadcasted_iota(I32, (QB, 1), 0)

        wi = wi_ref[...]
        for h in range(N_IDX_HEADS):
            qim_ref[h * QB:(h + 1) * QB, :] = _place_half(_head_slab(qi_ref, h), h % 2, h % 2).astype(BF16)
            wb_ref[h] = jnp.broadcast_to(wi[:, h:h + 1], (QB, LANES))
        _build_qbd(q_ref, qbd_ref, QB)

        def score_chunk(c, carry, parity):
            start = pl.multiple_of(c * ATT_CH, ATT_CH)
            kc = kib_ref[pl.ds(start, ATT_CH), :]
            acc = [jnp.zeros((QB, LANES), F32) for _ in range(ATT_CH // LANES)]
            hpd = 4
            for hg in range(N_IDX_HEADS // hpd):
                d = lax.dot_general(qim_ref[hg * hpd * QB:(hg + 1) * hpd * QB, :], kc, NT_DIMS,
                                    preferred_element_type=F32)
                for hh in range(hpd):
                    w = wb_ref[hg * hpd + hh]
                    for t in range(ATT_CH // LANES):
                        acc[t] = acc[t] + w * jnp.maximum(
                            d[hh * QB:(hh + 1) * QB, t * LANES:(t + 1) * LANES], 0.0)
            for t in range(ATT_CH // LANES):
                kpos = start + t * LANES + lax.broadcasted_iota(I32, (QB, LANES), 1)
                sc = jnp.where(kpos <= qrow, acc[t], NEG_INF)
                s_ref[c, :, t * LANES:(t + 1) * LANES] = sc
                st_ref[pl.ds(pl.multiple_of(start + t * LANES, LANES), LANES), :] = sc.T
            return carry

        _pair_loop(n_ch, score_chunk, 0)

        def key_major_chunk(c):
            return st_ref[pl.ds(pl.multiple_of(c * ATT_CH, ATT_CH), ATT_CH), :]

        def fold(x):
            return x.reshape(ATT_CH // CNT_ROWS, CNT_ROWS, QB)

        def count(t, strict):
            def body(c, cnt):
                sc = key_major_chunk(c)
                hit = jnp.where(sc > t if strict else sc >= t, 1.0, 0.0)
                return cnt + jnp.sum(fold(hit), axis=0)

            cnt = lax.fori_loop(0, n_ch, body, jnp.zeros((CNT_ROWS, QB), F32))
            return jnp.sum(cnt, axis=0, keepdims=True)

        def extremes(c, mm):
            sc = key_major_chunk(c)
            return (jnp.minimum(mm[0], jnp.min(fold(jnp.where(sc > NEG_INF, sc, -NEG_INF)), axis=0)),
                    jnp.maximum(mm[1], jnp.max(fold(sc), axis=0)))

        smin, smax = lax.fori_loop(0, n_ch, extremes, (jnp.full((CNT_ROWS, QB), -NEG_INF, F32),
                                                       jnp.full((CNT_ROWS, QB), NEG_INF, F32)))
        smin = jnp.min(smin, axis=0, keepdims=True)
        smax = jnp.max(smax, axis=0, keepdims=True)
        qrow_l = i * QB + lax.broadcasted_iota(I32, (1, QB), 1)
        lo = _bisect_threshold(functools.partial(count, strict=False), smin, smax,
                               jnp.logical_and(qrow_l + 1 > TOPK, smax > smin))
        lo_b = jnp.broadcast_to(lo, (QB, QB)).T

        excess = count(lo, strict=False) > TOPK
        tie_quota = jnp.where(excess, TOPK - count(lo, strict=True), TOPK)
        has_excess = jnp.max(jnp.where(excess, 1, 0)) > 0
        quota_b = jnp.broadcast_to(tie_quota, (QB, QB)).T

        m_ref[...] = jnp.full(m_ref.shape, NEG, F32)
        l_ref[...] = jnp.zeros_like(l_ref)
        acc_ref[...] = jnp.zeros_like(acc_ref)
        n_t = ATT_CH // LANES

        def plain_bias(c):
            return [jnp.where(s_ref[c, :, t * LANES:(t + 1) * LANES] >= lo_b, 0.0, NEG)
                    for t in range(n_t)]

        def quota_bias(c, seen):
            keys = [s_ref[c, :, t * LANES:(t + 1) * LANES] for t in range(n_t)]
            eq = [jnp.where(k == lo_b, 1.0, 0.0) for k in keys]
            incl = jnp.dot(jnp.concatenate(eq, axis=1).astype(BF16), tri_ref[...],
                           preferred_element_type=F32)
            bias = []
            for t in range(n_t):
                rank = seen + incl[:, t * LANES:(t + 1) * LANES] - eq[t]
                keep = jnp.logical_or(keys[t] > lo_b,
                                      jnp.logical_and(keys[t] == lo_b, rank < quota_b))
                bias.append(jnp.where(keep, 0.0, NEG))
            return bias, seen + jnp.broadcast_to(incl[:, ATT_CH - 1:ATT_CH], (QB, LANES))

        def masked_logits(c, bias):
            start = pl.multiple_of(c * ATT_CH, ATT_CH)
            a = lax.dot_general(qbd_ref[...], kb_ref[pl.ds(start, ATT_CH), :], NT_DIMS,
                                preferred_element_type=F32)
            return start, [[a[h * QB:(h + 1) * QB, t * LANES:(t + 1) * LANES] + bias[t]
                            for t in range(n_t)] for h in range(N_HEADS)]

        def max_chunk(c, carry):
            _, a = masked_logits(c, plain_bias(c))
            for h in range(N_HEADS):
                rows = slice(h * QB, (h + 1) * QB)
                m = m_ref[rows, :]
                for t in range(n_t):
                    m = jnp.maximum(m, a[h][t])
                m_ref[rows, :] = m
            return carry

        if bound_max:
            for h in range(N_HEADS):
                rows = slice(h * QB, (h + 1) * QB)
                qf = qbd_ref[rows, :].astype(F32)
                qn2 = jnp.sum(qf * qf, axis=1, keepdims=True)
                m_ref[rows, :] = jnp.sqrt(jnp.broadcast_to(qn2, (QB, LANES))
                                          * kmax_ref[h // GROUP]) * BOUND_MARGIN
        else:
            lax.fori_loop(0, n_ch, max_chunk, 0)
            m_ref[...] = jnp.broadcast_to(jnp.max(m_ref[...], axis=1, keepdims=True), m_ref.shape)

        def value_chunk(c, seen, parity, with_quota):
            pb_ref = p_ref.at[parity]
            if with_quota:
                bias, seen = quota_bias(c, seen)
            else:
                bias = plain_bias(c)
            start, a = masked_logits(c, bias)
            for h in range(N_HEADS):
                rows = slice(h * QB, (h + 1) * QB)
                m = m_ref[rows, :]
                l = l_ref[rows, :]
                for t in range(n_t):
                    p = jnp.exp(a[h][t] - m)
                    l = l + p
                    pb_ref[rows, t * LANES:(t + 1) * LANES] = p.astype(BF16)
                l_ref[rows, :] = l
            acc_ref[...] += jnp.dot(pb_ref[...], vb_ref[pl.ds(start, ATT_CH), :],
                                    preferred_element_type=F32)
            return seen

        seen0 = jnp.zeros((QB, LANES), F32)

        @pl.when(has_excess)
        def _():
            lax.fori_loop(0, n_ch, functools.partial(value_chunk, parity=0, with_quota=True), seen0)

        @pl.when(jnp.logical_not(has_excess))
        def _():
            _pair_loop(n_ch, functools.partial(value_chunk, with_quota=False), seen0)

        l = jnp.sum(l_ref[...], axis=1, keepdims=True)
        lmin_ref[...] = jnp.broadcast_to(jnp.min(l, axis=0, keepdims=True), lmin_ref.shape)
        o = acc_ref[...] / l
        for j, slab in enumerate(_gather_heads(o, QB)):
            o_ref[:, j * LANES:(j + 1) * LANES] = slab.astype(BF16)


def _attn_prompt(qi, wi, q, kib, kb, vb, bound_max):
    row = lambda i: (i, 0)
    full = lambda i: (0, 0)
    nrow = N_HEADS * QB
    return pl.pallas_call(
        functools.partial(_attn_prompt_kernel, bound_max=bound_max),
        grid=(R // QB,),
        in_specs=[
            pl.BlockSpec((QB, N_IDX_HEADS * IDX_DIM), row),
            pl.BlockSpec((QB, LANES), row),
            pl.BlockSpec((QB, D_ATTN), row),
            pl.BlockSpec((R, LANES), full),
            pl.BlockSpec((R, D_KV), full),
            pl.BlockSpec((R, D_KV), full),
        ],
        out_specs=[pl.BlockSpec((QB, D_ATTN), row),
                   pl.BlockSpec((1, SUBLANES, LANES), lambda i: (i, 0, 0))],
        out_shape=(jax.ShapeDtypeStruct((R, D_ATTN), BF16),
                   jax.ShapeDtypeStruct((R // QB, SUBLANES, LANES), F32)),
        scratch_shapes=[
            pltpu.VMEM((N_ATT_CH, QB, ATT_CH), F32),
            pltpu.VMEM((R, QB), F32),
            pltpu.VMEM((nrow, LANES), BF16),
            pltpu.VMEM((N_IDX_HEADS, QB, LANES), F32),
            pltpu.VMEM((nrow, D_KV), BF16),
            pltpu.VMEM((nrow, LANES), F32),
            pltpu.VMEM((nrow, LANES), F32),
            pltpu.VMEM((nrow, D_KV), F32),
            pltpu.VMEM((2, nrow, ATT_CH), BF16),
            pltpu.VMEM((N_KV_HEADS, QB, LANES), F32),
            pltpu.VMEM((ATT_CH, ATT_CH), BF16),
        ],
        compiler_params=_cparams(("arbitrary",)),
        name="attn_prompt_bound" if bound_max else "attn_prompt",
    )(qi, wi, q, kib, kb, vb)


S_CH = 8192
S_NCH = PAST_LEN // S_CH
S_PPC = S_CH // PAGE_SIZE
DMA_UNROLL = 8
S_TIE = 512
S_ROWS = N_HEADS * DEC_SEQ


def _page_lanes(p):
    return pl.ds(pl.multiple_of(p * PAGE_SIZE, PAGE_SIZE), PAGE_SIZE)


def _kidx_page_copy(pt_ref, kidx_hbm, kbuf, sem, b, p, slot):
    return pltpu.make_async_copy(
        kidx_hbm.at[pt_ref[b, p]], kbuf.at[slot, :, _page_lanes(p)], sem.at[slot])


def _sample_score_kernel(pt_ref, qi_ref, wi_ref, kin_ref, kidx_hbm, s_ref, sn_ref,
                         kbuf, sem, knp_ref):
    b = pl.program_id(0)
    nb = pl.num_programs(0)
    slot = b % 2

    def start_batch(bb, sl):
        def body(p, c):
            _kidx_page_copy(pt_ref, kidx_hbm, kbuf, sem, bb, p, sl).start()
            return c
        lax.fori_loop(0, N_PAGES, body, 0, unroll=DMA_UNROLL)

    @pl.when(b == 0)
    def _():
        start_batch(0, 0)

    @pl.when(b + 1 < nb)
    def _():
        start_batch(b + 1, 1 - slot)

    def wait_body(p, c):
        _kidx_page_copy(pt_ref, kidx_hbm, kbuf, sem, b, p, slot).wait()
        return c
    lax.fori_loop(0, N_PAGES, wait_body, 0, unroll=DMA_UNROLL)

    qi = qi_ref[0]
    qis = jnp.concatenate(
        [qi[:, h * IDX_DIM:(h + 1) * IDX_DIM] for h in range(N_IDX_HEADS)], axis=0).astype(BF16)
    wi = wi_ref[0]
    wb = [jnp.broadcast_to(wi[:, h:h + 1], (DEC_SEQ, LANES)) for h in range(N_IDX_HEADS)]

    def head_sum(d, width):
        outs = []
        for t in range(width // LANES):
            acc = jnp.zeros((DEC_SEQ, LANES), F32)
            for h in range(N_IDX_HEADS):
                acc = acc + wb[h] * jnp.maximum(
                    d[h * DEC_SEQ:(h + 1) * DEC_SEQ, t * LANES:(t + 1) * LANES], 0.0)
            outs.append(acc)
        return outs

    def score_chunk(c, carry):
        kc = kbuf[slot, :, pl.ds(pl.multiple_of(c * S_CH, S_CH), S_CH)].astype(BF16)
        d = jnp.dot(qis, kc, preferred_element_type=F32)
        for t, acc in enumerate(head_sum(d, S_CH)):
            s_ref[c, :, t * LANES:(t + 1) * LANES] = acc
        return carry
    lax.fori_loop(0, S_NCH, score_chunk, 0)

    knp_ref[...] = jnp.zeros_like(knp_ref)
    knp_ref[:DEC_SEQ, :] = kin_ref[0][:, :IDX_DIM]
    d = lax.dot_general(qis, knp_ref[...].astype(BF16), NT_DIMS, preferred_element_type=F32)
    kpos = lax.broadcasted_iota(I32, (DEC_SEQ, LANES), 1)
    srow = lax.broadcasted_iota(I32, (DEC_SEQ, LANES), 0)
    sn_ref[...] = jnp.where(kpos <= srow, head_sum(d, LANES)[0], NEG_INF)


def _sample_scores(page_table, qi_s, wi_s, ki_s, cache_kidx):
    blk = lambda b, pt: (b, 0, 0)
    grid_spec = pltpu.PrefetchScalarGridSpec(
        num_scalar_prefetch=1,
        grid=(DEC_BATCH,),
        in_specs=[
            pl.BlockSpec((1, DEC_SEQ, N_IDX_HEADS * IDX_DIM), blk),
            pl.BlockSpec((1, DEC_SEQ, LANES), blk),
            pl.BlockSpec((1, DEC_SEQ, LANES), blk),
            pl.BlockSpec(memory_space=pl.ANY),
        ],
        out_specs=[
            pl.BlockSpec((S_NCH, DEC_SEQ, S_CH), lambda b, pt: (0, b, 0)),
            pl.BlockSpec((DEC_SEQ, LANES), lambda b, pt: (b, 0)),
        ],
        scratch_shapes=[
            pltpu.VMEM((2, IDX_DIM, PAST_LEN), F32),
            pltpu.SemaphoreType.DMA((2,)),
            pltpu.VMEM((LANES, IDX_DIM), F32),
        ],
    )
    return pl.pallas_call(
        _sample_score_kernel,
        grid_spec=grid_spec,
        out_shape=(
            jax.ShapeDtypeStruct((S_NCH, NS, S_CH), F32),
            jax.ShapeDtypeStruct((NS, LANES), F32),
        ),
        compiler_params=_cparams(("arbitrary",)),
        name="sample_scores",
    )(page_table, qi_s, wi_s, ki_s, cache_kidx)


def _sample_threshold_kernel(s_ref, sn_ref, bp_ref, bn_ref):
    def lane_fold(x, op):
        out = x[:, :LANES]
        for t in range(1, S_CH // LANES):
            out = op(out, x[:, t * LANES:(t + 1) * LANES])
        return out

    def count(t, strict):
        def hits(sc):
            return jnp.where(sc > t if strict else sc >= t, 1.0, 0.0)

        cnt = lax.fori_loop(0, S_NCH, lambda c, cnt: cnt + lane_fold(hits(s_ref[c]), jnp.add),
                            hits(sn_ref[...]))
        return jnp.sum(cnt, axis=1, keepdims=True)

    def extremes(c, mm):
        return (jnp.minimum(mm[0], lane_fold(s_ref[c], jnp.minimum)),
                jnp.maximum(mm[1], lane_fold(s_ref[c], jnp.maximum)))

    new = sn_ref[...]
    smin, smax = lax.fori_loop(0, S_NCH, extremes, (jnp.where(new > NEG_INF, new, -NEG_INF), new))
    smin = jnp.min(smin, axis=1, keepdims=True)
    smax = jnp.max(smax, axis=1, keepdims=True)
    lo = _bisect_threshold(functools.partial(count, strict=False), smin, smax, smax > smin)

    quota = jnp.where(count(lo, strict=False) > TOPK, TOPK - count(lo, strict=True), TOPK)
    tri = _inclusive_prefix_matrix(S_TIE)

    def mask_tile(keys, seen, tri_t):
        eq = jnp.where(keys == lo, 1.0, 0.0)
        incl = jnp.dot(eq.astype(BF16), tri_t, preferred_element_type=F32)
        keep = jnp.logical_or(keys > lo, jnp.logical_and(keys == lo, seen + incl - eq < quota))
        return jnp.where(keep, 0.0, NEG), seen + incl[:, keys.shape[1] - 1:]

    def chunk(c, seen):
        for t in range(S_CH // S_TIE):
            cols = slice(t * S_TIE, (t + 1) * S_TIE)
            bp_ref[c, :, cols], seen = mask_tile(s_ref[c, :, cols], seen, tri)
        return seen

    seen = lax.fori_loop(0, S_NCH, chunk, jnp.zeros((NS, 1), F32))
    bn_ref[...], _ = mask_tile(sn_ref[...], seen, tri[:LANES, :LANES])


def _sample_threshold(keys_past, keys_new):
    vmem = pl.BlockSpec(memory_space=pltpu.VMEM)
    return pl.pallas_call(
        _sample_threshold_kernel,
        in_specs=[vmem, vmem],
        out_specs=[vmem, vmem],
        out_shape=(
            jax.ShapeDtypeStruct((S_NCH, NS, S_CH), F32),
            jax.ShapeDtypeStruct((NS, LANES), F32),
        ),
        compiler_params=pltpu.CompilerParams(vmem_limit_bytes=VMEM_LIMIT),
        name="sample_threshold",
    )(keys_past, keys_new)


def _kv_page_copies(pt_ref, ck_hbm, cv_hbm, kbuf, vbuf, sem, b, c, p, slot):
    page = pt_ref[b, c * S_PPC + p]
    dst = _page_lanes(p)
    return (pltpu.make_async_copy(ck_hbm.at[page], kbuf.at[slot, :, dst], sem.at[0, slot]),
            pltpu.make_async_copy(cv_hbm.at[page], vbuf.at[slot, :, dst], sem.at[1, slot]))


def _sample_attn_kernel(pt_ref, q_ref, bp_ref, bn_ref, kn_ref, vn_ref, ck_hbm, cv_hbm, o_ref,
                        kbuf, vbuf, sem, qbd_ref, m_ref, l_ref, acc_ref, knp_ref, vnp_ref):
    b = pl.program_id(0)
    c = pl.program_id(1)
    step = b * S_NCH + c
    n_steps = pl.num_programs(0) * S_NCH
    slot = step % 2

    def start_chunk(st, sl):
        bb = st // S_NCH
        cc = st % S_NCH

        def body(p, carry):
            for cp in _kv_page_copies(pt_ref, ck_hbm, cv_hbm, kbuf, vbuf, sem, bb, cc, p, sl):
                cp.start()
            return carry
        lax.fori_loop(0, S_PPC, body, 0, unroll=DMA_UNROLL)

    @pl.when(step == 0)
    def _():
        start_chunk(0, 0)

    @pl.when(step + 1 < n_steps)
    def _():
        start_chunk(step + 1, 1 - slot)

    def wait_body(p, carry):
        for cp in _kv_page_copies(pt_ref, ck_hbm, cv_hbm, kbuf, vbuf, sem, b, c, p, slot):
            cp.wait()
        return carry
    lax.fori_loop(0, S_PPC, wait_body, 0, unroll=DMA_UNROLL)

    @pl.when(c == 0)
    def _():
        _build_qbd(q_ref.at[0], qbd_ref, DEC_SEQ)
        m_ref[...] = jnp.full(m_ref.shape, NEG, F32)
        l_ref[...] = jnp.zeros_like(l_ref)
        acc_ref[...] = jnp.zeros_like(acc_ref)

    def update(kc, vc, bias, feature_major):
        qbd = qbd_ref[...].astype(BF16)
        if feature_major:
            a = jnp.dot(qbd, kc, preferred_element_type=F32)
        else:
            a = lax.dot_general(qbd, kc, NT_DIMS, preferred_element_type=F32)
        a3 = a.reshape(N_HEADS, DEC_SEQ, a.shape[1]) + bias[None]
        _softmax_step(a3, m_ref, l_ref, acc_ref, vc, DEC_SEQ, feature_major)

    update(kbuf[slot].astype(BF16), vbuf[slot].astype(BF16), bp_ref[0], True)

    @pl.when(c == S_NCH - 1)
    def _():
        knp_ref[...] = jnp.zeros_like(knp_ref)
        vnp_ref[...] = jnp.zeros_like(vnp_ref)
        knp_ref[:DEC_SEQ, :] = kn_ref[0]
        vnp_ref[:DEC_SEQ, :] = vn_ref[0]
        update(knp_ref[...].astype(BF16), vnp_ref[...].astype(BF16), bn_ref[...], False)
        o = acc_ref[...] / l_ref[...]
        for j, slab in enumerate(_gather_heads(o, DEC_SEQ)):
            o_ref[0, :, j * LANES:(j + 1) * LANES] = slab


def _sample_attn(page_table, q_s, bias_past, bias_new, k_s, v_s, cache_k, cache_v):
    blk = lambda b, c, pt: (b, 0, 0)
    grid_spec = pltpu.PrefetchScalarGridSpec(
        num_scalar_prefetch=1,
        grid=(DEC_BATCH, S_NCH),
        in_specs=[
            pl.BlockSpec((1, DEC_SEQ, D_ATTN), blk),
            pl.BlockSpec((1, DEC_SEQ, S_CH), lambda b, c, pt: (c, b, 0)),
            pl.BlockSpec((DEC_SEQ, LANES), lambda b, c, pt: (b, 0)),
            pl.BlockSpec((1, DEC_SEQ, D_KV), blk),
            pl.BlockSpec((1, DEC_SEQ, D_KV), blk),
            pl.BlockSpec(memory_space=pl.ANY),
            pl.BlockSpec(memory_space=pl.ANY),
        ],
        out_specs=pl.BlockSpec((1, DEC_SEQ, D_ATTN), blk),
        scratch_shapes=[
            pltpu.VMEM((2, D_KV, S_CH), F32),
            pltpu.VMEM((2, D_KV, S_CH), F32),
            pltpu.SemaphoreType.DMA((2, 2)),
            pltpu.VMEM((S_ROWS, D_KV), F32),
            pltpu.VMEM((S_ROWS, 1), F32),
            pltpu.VMEM((S_ROWS, 1), F32),
            pltpu.VMEM((S_ROWS, D_KV), F32),
            pltpu.VMEM((LANES, D_KV), F32),
            pltpu.VMEM((LANES, D_KV), F32),
        ],
    )
    return pl.pallas_call(
        _sample_attn_kernel,
        grid_spec=grid_spec,
        out_shape=jax.ShapeDtypeStruct((DEC_BATCH, DEC_SEQ, D_ATTN), F32),
        compiler_params=_cparams(("arbitrary", "arbitrary")),
        name="sample_attn",
    )(page_table, q_s, bias_past, bias_new, k_s, v_s, cache_k, cache_v)


def _rope_tables():
    pos = np.zeros((R,), np.float32)
    pos[:T] = np.arange(T)
    pos[ROW_S:ROW_S + NS] = np.tile(PAST_LEN + np.arange(DEC_SEQ), DEC_BATCH)
    half = HEAD_DIM // 2
    inv_freq = ROPE_THETA ** (-(jnp.arange(half, dtype=F32) * 2.0 / HEAD_DIM))
    ang = jnp.asarray(pos)[:, None] * inv_freq
    cos = jnp.tile(jnp.cos(ang), (1, LANES // half))
    sin = jnp.sin(ang)
    sin = jnp.tile(jnp.concatenate([-sin, sin], axis=1), (1, LANES // HEAD_DIM))
    return cos, sin


def kernel(x_prompt, x_sample, cache_k, cache_v, cache_kidx, state_pool, page_table, meta_tokens,
           norm_mix_pre, w_in, idx_k_norm, w_pool, pool_scale, w_attn_o, w_out, norm_mix_post,
           norm_ffn_pre, w_up, w_down, norm_ffn_post):
    x_all = jnp.concatenate([
        meta_tokens, x_prompt[0], jnp.zeros((ROW_S - T, D_MODEL), F32),
        x_sample.reshape(NS, D_MODEL), jnp.zeros((R - ROW_S - NS, D_MODEL), F32)], axis=0)

    wt = jnp.transpose(w_in[0])
    o_u, o_q, o_k, o_v, o_qi, o_ki, o_wi, o_gp = np.cumsum((0, D_POOL, D_ATTN, D_KV, D_KV,
                                                            N_IDX_HEADS * IDX_DIM, IDX_DIM, N_IDX_HEADS))
    n_pad = W_ROW_C - W_B_COLS - (o_v - o_q) - (o_gp - o_qi)
    w_all = jnp.concatenate([
        wt[o_v:o_qi], wt[o_u:o_q], wt[o_q:o_v], wt[o_qi:o_gp],
        jnp.zeros((n_pad, D_MODEL), F32), wt[o_gp:]], axis=0).astype(BF16)
    assert w_all.shape == (W_ROWS, D_MODEL)

    g_pre = norm_mix_pre[0][None]
    cos, sin = _rope_tables()
    gk = jnp.tile(idx_k_norm[0], LANES // IDX_DIM)[None]

    q, k, kb, qi, ki, kib, wi = _proj_a(x_all, g_pre, w_all, cos, sin, gk)
    v, vb, u = _proj_b(x_all, g_pre, w_all)
    gates = _proj_c(x_all, g_pre, w_all)

    def sample_rows(a):
        return a[ROW_S:ROW_S + NS].reshape(DEC_BATCH, DEC_SEQ, a.shape[1])

    pooled = _pool_prompt(u)
    u_s = sample_rows(u)
    state = state_pool[0]
    state16 = jnp.pad(state, ((0, 0), (HALO - POOL_BUF, 0), (0, 0)))
    pooled_s = _pool_sample(u_s, state16)
    pooled = lax.dynamic_update_slice(pooled, pooled_s.reshape(NS, D_POOL), (ROW_S, 0))

    attn, lmin = _attn_prompt(qi, wi, q, kib, kb, vb, bound_max=True)
    attn = lax.cond(jnp.min(lmin) >= MIN_ROW_SUM, lambda: attn,
                    lambda: _attn_prompt(qi, wi, q, kib, kb, vb, bound_max=False)[0])
    k_s, v_s = sample_rows(k), sample_rows(v)
    n_pool = cache_k.shape[1]
    kidx_fm = jnp.transpose(cache_kidx[0], (0, 2, 1))
    k_fm = jnp.transpose(cache_k[0], (0, 2, 3, 1)).reshape(n_pool, D_KV, PAGE_SIZE)
    v_fm = jnp.transpose(cache_v[0], (0, 2, 3, 1)).reshape(n_pool, D_KV, PAGE_SIZE)
    keys_past, keys_new = _sample_scores(page_table, sample_rows(qi), sample_rows(wi), sample_rows(ki),
                                         kidx_fm)
    bias_past, bias_new = _sample_threshold(keys_past, keys_new)
    attn_s = _sample_attn(page_table, sample_rows(q), bias_past, bias_new, k_s, v_s, k_fm, v_fm)
    attn = lax.dynamic_update_slice(attn, attn_s.reshape(NS, D_ATTN).astype(BF16), (ROW_S, 0))

    x1 = _merge(x_all, pooled, attn, gates, w_pool[0].astype(BF16), pool_scale[0][None],
                w_attn_o[0].astype(BF16), w_out[0].astype(BF16), norm_mix_post[0][None])
    y = _ffn(x1, norm_ffn_pre[0][None], w_up[0].astype(BF16), w_down[0].astype(BF16),
             norm_ffn_post[0][None])

    y_prompt = y[N_META:T][None]
    y_sample = y[ROW_S:ROW_S + NS].reshape(DEC_BATCH, DEC_SEQ, D_MODEL)
    kv_shape = (1, 1, T, N_KV_HEADS, HEAD_DIM)
    kv_s_shape = (1, DEC_BATCH, DEC_SEQ, N_KV_HEADS, HEAD_DIM)
    return (
        y_prompt, y_sample,
        k[:T].reshape(kv_shape), v[:T].reshape(kv_shape), ki[:T, :IDX_DIM][None, None],
        u[T - POOL_BUF:T][None, None],
        k_s.reshape(kv_s_shape), v_s.reshape(kv_s_shape), sample_rows(ki)[None, :, :, :IDX_DIM],
        jnp.concatenate([state[:, DEC_SEQ:], u_s], axis=1)[None],
    )
```

```python
import functools

import jax
import jax.numpy as jnp
import numpy as np
from jax import lax
from jax.experimental import pallas as pl
from jax.experimental.pallas import tpu as pltpu

F32 = jnp.float32
BF16 = jnp.bfloat16
I32 = jnp.int32

D_MODEL = 2048
SEQ = 8192
DEC_BATCH = 32
DEC_SEQ = 8
PAST_LEN = 16384
PAGE_SIZE = 128
N_PAGES = PAST_LEN // PAGE_SIZE
N_META = 16
N_HEADS = 16
N_KV_HEADS = 4
HEAD_DIM = 64
GROUP = N_HEADS // N_KV_HEADS
D_ATTN = N_HEADS * HEAD_DIM
D_KV = N_KV_HEADS * HEAD_DIM
ATTN_SCALE = HEAD_DIM ** -0.5
N_IDX_HEADS = 16
IDX_DIM = 64
INDEX_W_SCALE = (N_IDX_HEADS ** -0.5) * (IDX_DIM ** -0.5)
TOPK = 256
POOL_WINDOWS = (2, 4, 8, 16)
N_POOL_GROUPS = 4
D_POOL = D_MODEL // 2
POOL_GROUP = D_POOL // N_POOL_GROUPS
POOL_OUT_GROUP = D_MODEL // N_POOL_GROUPS
POOL_BUF = max(POOL_WINDOWS) - 1
D_FF = 4 * D_MODEL
ROPE_THETA = 10000.0
EPS = 1e-6

LANES = 128
SUBLANES = 8
T = SEQ + N_META
QB = 128
N_QBLK = -(-T // QB)
ROW_S = N_QBLK * QB
NS = DEC_BATCH * DEC_SEQ
R = 8704
VMEM_LIMIT = 56 * 1024 * 1024

NEG = -1e30


def _cparams(sem):
    return pltpu.CompilerParams(dimension_semantics=sem, vmem_limit_bytes=VMEM_LIMIT)


def _rms(x, g):
    return x * lax.rsqrt(jnp.mean(x * x, axis=-1, keepdims=True) + EPS) * g


def _swap_halves(x):
    lane = lax.broadcasted_iota(I32, x.shape, 1)
    return jnp.where(lane % HEAD_DIM < HEAD_DIM // 2,
                     pltpu.roll(x, LANES - HEAD_DIM // 2, 1),
                     pltpu.roll(x, HEAD_DIM // 2, 1))


def _rope_cols(x, cos, sin):
    outs = []
    for c in range(x.shape[1] // LANES):
        xc = x[:, c * LANES:(c + 1) * LANES]
        outs.append(xc * cos + _swap_halves(xc) * sin)
    return outs


PROJ_TM = 544
N_ROPE_A = D_ATTN + D_KV
PROJ_TN_A = N_ROPE_A
W_A_COLS = 2 * PROJ_TN_A
W_B_COLS = D_KV + D_POOL
W_C_COLS = 2 * D_MODEL
PROJ_TN_C = 1024
PROJ_TM_C = 1088
W_ROW_B = 0
W_ROW_A = W_B_COLS
W_ROW_C = 4 * PROJ_TN_C
W_ROWS = W_ROW_C + W_C_COLS
assert W_ROW_A % PROJ_TN_A == 0 and W_ROW_A + W_A_COLS <= W_ROW_C


def _norm_to_scratch(x_ref, g_ref, xn_ref):
    @pl.when(pl.program_id(1) == 0)
    def _():
        xn_ref[...] = _rms(x_ref[...], g_ref[...]).astype(BF16)


def _proj_dot(xn_ref, w_ref):
    return lax.dot_general(xn_ref[...], w_ref[...], (((1,), (1,)), ((), ())),
                           preferred_element_type=F32)


PROJ_PIECE = 256


def _proj_a_kernel(x_ref, g_ref, w0_ref, w1_ref, cos_ref, sin_ref, gk_ref,
                   q_ref, k_ref, kb_ref, qi_ref, ki_ref, kib_ref, wi_ref, xn_ref):
    xn_ref[...] = _rms(x_ref[...], g_ref[...]).astype(BF16)
    cos = cos_ref[...]
    sin = sin_ref[...]
    n_q, n_k, n_qi = D_ATTN // LANES, D_KV // LANES, N_IDX_HEADS * IDX_DIM // LANES

    def rope(slab):
        return slab * cos + _swap_halves(slab) * sin

    def emit(g, slab):
        if g < n_q:
            q_ref[:, g * LANES:(g + 1) * LANES] = rope(slab)
        elif g < n_q + n_k:
            c = g - n_q
            r = rope(slab)
            k_ref[:, c * LANES:(c + 1) * LANES] = r
            kb_ref[:, c * LANES:(c + 1) * LANES] = r.astype(BF16)
        elif g < n_q + n_k + n_qi:
            c = g - n_q - n_k
            qi_ref[:, c * LANES:(c + 1) * LANES] = rope(slab)
        elif g == n_q + n_k + n_qi:
            lane = lax.broadcasted_iota(I32, slab.shape, 1)
            is_ki = lane < IDX_DIM
            ms = jnp.sum(jnp.where(is_ki, slab * slab, 0.0), axis=-1, keepdims=True) / IDX_DIM
            kin = rope(slab * lax.rsqrt(ms + EPS) * gk_ref[...])
            ki2 = jnp.where(is_ki, kin, pltpu.roll(kin, IDX_DIM, 1))
            ki_ref[...] = ki2
            kib_ref[...] = ki2.astype(BF16)
            wi_ref[...] = pltpu.roll(slab, LANES - IDX_DIM, 1) * INDEX_W_SCALE

    per_block = PROJ_TN_A // PROJ_PIECE
    for p in range(W_A_COLS // PROJ_PIECE):
        w_ref = (w0_ref, w1_ref)[p // per_block]
        r0 = (p % per_block) * PROJ_PIECE
        d = lax.dot_general(xn_ref[...], w_ref[r0:r0 + PROJ_PIECE, :], (((1,), (1,)), ((), ())),
                            preferred_element_type=F32)
        for e in range(PROJ_PIECE // LANES):
            emit(p * (PROJ_PIECE // LANES) + e, d[:, e * LANES:(e + 1) * LANES])


def _proj_a(x_all, g, w_a, cos, sin, gk):
    n_i = R // PROJ_TM
    row = lambda i: (i, 0)
    const = lambda i: (0, 0)
    first = W_ROW_A // PROJ_TN_A
    outs = (
        jax.ShapeDtypeStruct((R, D_ATTN), F32),
        jax.ShapeDtypeStruct((R, D_KV), F32),
        jax.ShapeDtypeStruct((R, D_KV), BF16),
        jax.ShapeDtypeStruct((R, N_IDX_HEADS * IDX_DIM), F32),
        jax.ShapeDtypeStruct((R, LANES), F32),
        jax.ShapeDtypeStruct((R, LANES), BF16),
        jax.ShapeDtypeStruct((R, LANES), F32),
    )
    return pl.pallas_call(
        _proj_a_kernel,
        grid=(n_i,),
        in_specs=[
            pl.BlockSpec((PROJ_TM, D_MODEL), row),
            pl.BlockSpec((1, D_MODEL), const),
            pl.BlockSpec((PROJ_TN_A, D_MODEL), lambda i: (first, 0)),
            pl.BlockSpec((PROJ_TN_A, D_MODEL), lambda i: (first + 1, 0)),
            pl.BlockSpec((PROJ_TM, LANES), row),
            pl.BlockSpec((PROJ_TM, LANES), row),
            pl.BlockSpec((1, LANES), const),
        ],
        out_specs=[pl.BlockSpec((PROJ_TM, o.shape[1]), row) for o in outs],
        out_shape=outs,
        scratch_shapes=[pltpu.VMEM((PROJ_TM, D_MODEL), BF16)],
        compiler_params=_cparams(("arbitrary",)),
        name="proj_rope",
    )(x_all, g, w_a, w_a, cos, sin, gk)


def _proj_b_kernel(x_ref, g_ref, w_ref, v_ref, vb_ref, u_ref, xn_ref):
    _norm_to_scratch(x_ref, g_ref, xn_ref)
    p = _proj_dot(xn_ref, w_ref)
    v_ref[...] = p[:, :D_KV]
    vb_ref[...] = p[:, :D_KV].astype(BF16)
    u_ref[...] = p[:, D_KV:]


def _proj_b(x_all, g, w_b):
    n_i = R // PROJ_TM
    row = lambda i, j: (i, 0)
    outs = (
        jax.ShapeDtypeStruct((R, D_KV), F32),
        jax.ShapeDtypeStruct((R, D_KV), BF16),
        jax.ShapeDtypeStruct((R, D_POOL), F32),
    )
    return pl.pallas_call(
        _proj_b_kernel,
        grid=(n_i, 1),
        in_specs=[
            pl.BlockSpec((PROJ_TM, D_MODEL), row),
            pl.BlockSpec((1, D_MODEL), lambda i, j: (0, 0)),
            pl.BlockSpec((W_B_COLS, D_MODEL), lambda i, j: (W_ROW_B // W_B_COLS, 0)),
        ],
        out_specs=[pl.BlockSpec((PROJ_TM, o.shape[1]), row) for o in outs],
        out_shape=outs,
        scratch_shapes=[pltpu.VMEM((PROJ_TM, D_MODEL), BF16)],
        compiler_params=_cparams(("arbitrary", "arbitrary")),
        name="proj_vu",
    )(x_all, g, w_b)


def _proj_c_kernel(x_ref, g_ref, w_ref, o_ref, xn_ref):
    _norm_to_scratch(x_ref, g_ref, xn_ref)
    o_ref[...] = _proj_dot(xn_ref, w_ref)


def _proj_c(x_all, g, w_c):
    n_i = R // PROJ_TM_C
    return pl.pallas_call(
        _proj_c_kernel,
        grid=(n_i, W_C_COLS // PROJ_TN_C),
        in_specs=[
            pl.BlockSpec((PROJ_TM_C, D_MODEL), lambda i, j: (i, 0)),
            pl.BlockSpec((1, D_MODEL), lambda i, j: (0, 0)),
            pl.BlockSpec((PROJ_TN_C, D_MODEL), lambda i, j: (W_ROW_C // PROJ_TN_C + j, 0)),
        ],
        out_specs=pl.BlockSpec((PROJ_TM_C, PROJ_TN_C), lambda i, j: (i, j)),
        out_shape=jax.ShapeDtypeStruct((R, W_C_COLS), F32),
        scratch_shapes=[pltpu.VMEM((PROJ_TM_C, D_MODEL), BF16)],
        compiler_params=_cparams(("arbitrary", "arbitrary")),
        name="proj_gates",
    )(x_all, g, w_c)


POOL_TM = 512
HALO = 16


def _window_mean_minus_cur(ext_ref, rows, inv_cnt):
    outs = []
    for g, w in enumerate(POOL_WINDOWS):
        cols = slice(g * POOL_GROUP, (g + 1) * POOL_GROUP)
        cur = ext_ref[HALO:HALO + rows, cols]
        acc = cur
        for d in range(1, w):
            acc = acc + ext_ref[HALO - d:HALO - d + rows, cols]
        outs.append(acc * inv_cnt[g] - cur)
    return outs


def _pool_prompt_kernel(u_ref, halo_ref, o_ref, ext_ref):
    i = pl.program_id(0)
    ext_ref[HALO:, :] = u_ref[...]
    ext_ref[:HALO, :] = jnp.where(i == 0, 0.0, halo_ref[...])
    pos = i * POOL_TM + lax.broadcasted_iota(I32, (POOL_TM, 1), 0)
    inv_cnt = [1.0 / jnp.minimum(pos + 1, w).astype(F32) for w in POOL_WINDOWS]
    outs = _window_mean_minus_cur(ext_ref, POOL_TM, inv_cnt)
    for g in range(N_POOL_GROUPS):
        o_ref[:, g * POOL_GROUP:(g + 1) * POOL_GROUP] = outs[g]


def _pool_prompt(u):
    per = POOL_TM // HALO
    return pl.pallas_call(
        _pool_prompt_kernel,
        grid=(R // POOL_TM,),
        in_specs=[
            pl.BlockSpec((POOL_TM, D_POOL), lambda i: (i, 0)),
            pl.BlockSpec((HALO, D_POOL), lambda i: (jnp.maximum(i * per - 1, 0), 0)),
        ],
        out_specs=pl.BlockSpec((POOL_TM, D_POOL), lambda i: (i, 0)),
        out_shape=jax.ShapeDtypeStruct((R, D_POOL), F32),
        scratch_shapes=[pltpu.VMEM((POOL_TM + HALO, D_POOL), F32)],
        compiler_params=_cparams(("arbitrary",)),
        name="pool_prompt",
    )(u, u)


def _pool_sample_kernel(u_ref, st_ref, o_ref, ext_ref):
    for b in range(DEC_BATCH):
        ext_ref[:HALO, :] = st_ref[b]
        ext_ref[HALO:, :] = u_ref[b]
        inv_cnt = [1.0 / w for w in POOL_WINDOWS]
        outs = _window_mean_minus_cur(ext_ref, DEC_SEQ, inv_cnt)
        for g in range(N_POOL_GROUPS):
            o_ref[b, :, g * POOL_GROUP:(g + 1) * POOL_GROUP] = outs[g]


def _pool_sample(u_s, state16):
    return pl.pallas_call(
        _pool_sample_kernel,
        out_shape=jax.ShapeDtypeStruct((DEC_BATCH, DEC_SEQ, D_POOL), F32),
        scratch_shapes=[pltpu.VMEM((HALO + DEC_SEQ, D_POOL), F32)],
        compiler_params=pltpu.CompilerParams(vmem_limit_bytes=VMEM_LIMIT),
        name="pool_sample",
    )(u_s, state16)


MERGE_TM = 256


def _merge_kernel(x_ref, pooled_ref, attn_ref, gate_ref, wp_ref, ps_ref, wa_ref, wo_ref, gn_ref, o_ref):
    pooled = pooled_ref[...].astype(BF16)
    pool_out = jnp.concatenate(
        [jnp.dot(pooled[:, g * POOL_GROUP:(g + 1) * POOL_GROUP], wp_ref[g], preferred_element_type=F32)
         for g in range(N_POOL_GROUPS)], axis=1) * ps_ref[...]
    attn_out = jnp.dot(attn_ref[...], wa_ref[...], preferred_element_type=F32)
    gate = gate_ref[...]
    m = (jax.nn.sigmoid(gate[:, :D_MODEL]) * pool_out
         + jax.nn.sigmoid(gate[:, D_MODEL:]) * attn_out)
    mix = jnp.dot(m.astype(BF16), wo_ref[...], preferred_element_type=F32)
    o_ref[...] = x_ref[...] + _rms(mix, gn_ref[...])


def _merge(x_all, pooled, attn, gates, w_pool, pool_scale, w_attn_o, w_out, g_post):
    row = lambda i: (i, 0)
    const2 = lambda i: (0, 0)
    return pl.pallas_call(
        _merge_kernel,
        grid=(R // MERGE_TM,),
        in_specs=[
            pl.BlockSpec((MERGE_TM, D_MODEL), row),
            pl.BlockSpec((MERGE_TM, D_POOL), row),
            pl.BlockSpec((MERGE_TM, D_ATTN), row),
            pl.BlockSpec((MERGE_TM, 2 * D_MODEL), row),
            pl.BlockSpec((N_POOL_GROUPS, POOL_GROUP, POOL_OUT_GROUP), lambda i: (0, 0, 0)),
            pl.BlockSpec((1, D_MODEL), const2),
            pl.BlockSpec((D_ATTN, D_MODEL), const2),
            pl.BlockSpec((D_MODEL, D_MODEL), const2),
            pl.BlockSpec((1, D_MODEL), const2),
        ],
        out_specs=pl.BlockSpec((MERGE_TM, D_MODEL), row),
        out_shape=jax.ShapeDtypeStruct((R, D_MODEL), F32),
        compiler_params=_cparams(("arbitrary",)),
        name="merge",
    )(x_all, pooled, attn, gates, w_pool, pool_scale, w_attn_o, w_out, g_post)


FFN_TM = 544
FFN_TF = 1024


def _ffn_kernel(x_ref, gpre_ref, wu_ref, wd_ref, gpost_ref, o_ref, h_ref, acc_ref):
    j = pl.program_id(1)

    @pl.when(j == 0)
    def _():
        h_ref[...] = _rms(x_ref[...], gpre_ref[...]).astype(BF16)
        acc_ref[...] = jnp.zeros_like(acc_ref)

    a = jnp.maximum(jnp.dot(h_ref[...], wu_ref[...], preferred_element_type=F32), 0.0)
    acc_ref[...] += jnp.dot((a * a).astype(BF16), wd_ref[...], preferred_element_type=F32)

    @pl.when(j == pl.num_programs(1) - 1)
    def _():
        o_ref[...] = x_ref[...] + _rms(acc_ref[...], gpost_ref[...])


def _ffn(x1, g_pre, w_up, w_down, g_post):
    return pl.pallas_call(
        _ffn_kernel,
        grid=(R // FFN_TM, D_FF // FFN_TF),
        in_specs=[
            pl.BlockSpec((FFN_TM, D_MODEL), lambda i, j: (i, 0)),
            pl.BlockSpec((1, D_MODEL), lambda i, j: (0, 0)),
            pl.BlockSpec((D_MODEL, FFN_TF), lambda i, j: (0, j)),
            pl.BlockSpec((FFN_TF, D_MODEL), lambda i, j: (j, 0)),
            pl.BlockSpec((1, D_MODEL), lambda i, j: (0, 0)),
        ],
        out_specs=pl.BlockSpec((FFN_TM, D_MODEL), lambda i, j: (i, 0)),
        out_shape=jax.ShapeDtypeStruct((R, D_MODEL), F32),
        scratch_shapes=[pltpu.VMEM((FFN_TM, D_MODEL), BF16), pltpu.VMEM((FFN_TM, D_MODEL), F32)],
        compiler_params=_cparams(("arbitrary", "arbitrary")),
        name="ffn",
    )(x1, g_pre, w_up, w_down, g_post)


NEG_INF = float("-inf")
MAX_BISECT = 192
STEPS_PER_TRIP = 3


def _above(x):
    return x + (jnp.abs(x) * 2.0 ** -20 + 1e-30)


def _bisect_threshold(count_ge, smin, smax, active0):
    def cond(st):
        it, _, _, active = st
        return jnp.logical_and(it < MAX_BISECT, jnp.max(active) > 0)

    def step(lo, hi, active):
        mid = 0.5 * lo + 0.5 * hi
        cnt = count_ge(mid)
        on = active > 0
        ge = cnt >= TOPK
        still = jnp.logical_and(cnt != TOPK, jnp.logical_and(mid != lo, mid != hi))
        lo = jnp.where(jnp.logical_and(on, ge), mid, lo)
        hi = jnp.where(jnp.logical_and(on, jnp.logical_not(ge)), mid, hi)
        return lo, hi, jnp.where(jnp.logical_and(on, still), 1, 0).astype(I32)

    def body(st):
        it, lo, hi, active = st
        for _ in range(STEPS_PER_TRIP):
            lo, hi, active = step(lo, hi, active)
        return it + STEPS_PER_TRIP, lo, hi, active

    _, lo, _, _ = lax.while_loop(cond, body, (jnp.int32(0), smin, _above(smax), active0.astype(I32)))
    return lo


def _head_slab(x_ref, h):
    j = h // 2
    return x_ref[:, j * LANES:(j + 1) * LANES]


def _place_half(slab, src_half, dst_half):
    lane = lax.broadcasted_iota(I32, slab.shape, 1)
    x = slab if src_half == dst_half else pltpu.roll(slab, HEAD_DIM, 1)
    return jnp.where((lane >= HEAD_DIM) == (dst_half == 1), x, 0.0)


def _gather_heads(o, rows):
    outs = []
    for j in range(N_HEADS // 2):
        n = (2 * j) // GROUP
        tc, th = n // 2, n % 2
        a0 = o[(2 * j) * rows:(2 * j + 1) * rows, tc * LANES:(tc + 1) * LANES]
        a1 = o[(2 * j + 1) * rows:(2 * j + 2) * rows, tc * LANES:(tc + 1) * LANES]
        x0 = a0 if th == 0 else pltpu.roll(a0, HEAD_DIM, 1)
        x1 = a1 if th == 1 else pltpu.roll(a1, HEAD_DIM, 1)
        lane = lax.broadcasted_iota(I32, x0.shape, 1)
        outs.append(jnp.where(lane < HEAD_DIM, x0, x1))
    return outs


def _build_qbd(q_ref, qbd_ref, rows):
    for h in range(N_HEADS):
        n = h // GROUP
        tc, th = n // 2, n % 2
        placed = _place_half(_head_slab(q_ref, h) * ATTN_SCALE, h % 2, th).astype(qbd_ref.dtype)
        for c in range(D_KV // LANES):
            qbd_ref[h * rows:(h + 1) * rows, c * LANES:(c + 1) * LANES] = (
                placed if c == tc else jnp.zeros_like(placed))


def _softmax_step(a3, m_ref, l_ref, acc_ref, vc, rows, v_transposed):
    n = N_HEADS * rows
    ch = a3.shape[2]
    m_prev = m_ref[...].reshape(N_HEADS, rows, 1)
    m_new = jnp.maximum(m_prev, jnp.max(a3, axis=2, keepdims=True))
    p = jnp.exp(a3 - m_new)
    alpha = jnp.exp(m_prev - m_new)
    l_ref[...] = (alpha * l_ref[...].reshape(N_HEADS, rows, 1)
                  + jnp.sum(p, axis=2, keepdims=True)).reshape(n, 1)
    m_ref[...] = m_new.reshape(n, 1)
    pb = p.reshape(n, ch).astype(BF16)
    if v_transposed:
        pv = lax.dot_general(pb, vc, NT_DIMS, preferred_element_type=F32)
    else:
        pv = jnp.dot(pb, vc, preferred_element_type=F32)
    acc_ref[...] = alpha.reshape(n, 1) * acc_ref[...] + pv


ATT_CH = 512
CNT_ROWS = 8 * SUBLANES
N_ATT_CH = R // ATT_CH
NT_DIMS = (((1,), (1,)), ((), ()))
BOUND_MARGIN = 1.0 + 2.0 ** -5
MIN_ROW_SUM = 1e-30


def _pair_loop(n, body, carry):
    def two(j, carry):
        return body(2 * j + 1, body(2 * j, carry, 0), 1)

    carry = lax.fori_loop(0, n // 2, two, carry)
    return lax.cond(n % 2 == 1, lambda x: body(n - 1, x, 0), lambda x: x, carry)


def _inclusive_prefix_matrix(n):
    r_i = lax.broadcasted_iota(I32, (n, n), 0)
    c_i = lax.broadcasted_iota(I32, (n, n), 1)
    return jnp.where(r_i <= c_i, 1.0, 0.0).astype(BF16)


def _attn_prompt_kernel(qi_ref, wi_ref, q_ref, kib_ref, kb_ref, vb_ref, o_ref, lmin_ref,
                        s_ref, st_ref, qim_ref, wb_ref, qbd_ref, m_ref, l_ref, acc_ref, p_ref,
                        kmax_ref, tri_ref, *, bound_max):
    i = pl.program_id(0)

    @pl.when(i >= N_QBLK)
    def _():
        o_ref[...] = jnp.zeros_like(o_ref)
        lmin_ref[...] = jnp.ones_like(lmin_ref)

    @pl.when(i == 0)
    def _():
        tri_ref[...] = _inclusive_prefix_matrix(ATT_CH)

    if bound_max:
        @pl.when(i == 0)
        def _():
            r_i = lax.broadcasted_iota(I32, (D_KV, LANES), 0)
            c_i = lax.broadcasted_iota(I32, (D_KV, LANES), 1)
            sel = jnp.where(r_i // HEAD_DIM == c_i, 1.0, 0.0).astype(BF16)

            def body(c, mx):
                kc = kb_ref[pl.ds(pl.multiple_of(c * ATT_CH, ATT_CH), ATT_CH), :].astype(F32)
                n2 = jnp.dot((kc * kc).astype(BF16), sel, preferred_element_type=F32)
                return jnp.maximum(mx, n2)

            mx = lax.fori_loop(0, N_ATT_CH, body, jnp.zeros((ATT_CH, LANES), F32))
            kmax = jnp.max(mx, axis=0, keepdims=True)
            for n in range(N_KV_HEADS):
                kmax_ref[n] = jnp.broadcast_to(kmax[:, n:n + 1], (QB, LANES))

    @pl.when(i < N_QBLK)
    def _():
        n_ch = (i * QB) // ATT_CH + 1
        qrow = i * QB + lax.broadcasted_iota(I32, (QB, 1), 0)

        wi = wi_ref[...]
        for h in range(N_IDX_HEADS):
            qim_ref[h * QB:(h + 1) * QB, :] = _place_half(_head_slab(qi_ref, h), h % 2, h % 2).astype(BF16)
            wb_ref[h] = jnp.broadcast_to(wi[:, h:h + 1], (QB, LANES))
        _build_qbd(q_ref, qbd_ref, QB)

        def score_chunk(c, carry, parity):
            start = pl.multiple_of(c * ATT_CH, ATT_CH)
            kc = kib_ref[pl.ds(start, ATT_CH), :]
            acc = [jnp.zeros((QB, LANES), F32) for _ in range(ATT_CH // LANES)]
            hpd = 4
            for hg in range(N_IDX_HEADS // hpd):
                d = lax.dot_general(qim_ref[hg * hpd * QB:(hg + 1) * hpd * QB, :], kc, NT_DIMS,
                                    preferred_element_type=F32)
                for hh in range(hpd):
                    w = wb_ref[hg * hpd + hh]
                    for t in range(ATT_CH // LANES):
                        acc[t] = acc[t] + w * jnp.maximum(
                            d[hh * QB:(hh + 1) * QB, t * LANES:(t + 1) * LANES], 0.0)
            for t in range(ATT_CH // LANES):
                kpos = start + t * LANES + lax.broadcasted_iota(I32, (QB, LANES), 1)
                sc = jnp.where(kpos <= qrow, acc[t], NEG_INF)
                s_ref[c, :, t * LANES:(t + 1) * LANES] = sc
                st_ref[pl.ds(pl.multiple_of(start + t * LANES, LANES), LANES), :] = sc.T
            return carry

        _pair_loop(n_ch, score_chunk, 0)

        def key_major_chunk(c):
            return st_ref[pl.ds(pl.multiple_of(c * ATT_CH, ATT_CH), ATT_CH), :]

        def fold(x):
            return x.reshape(ATT_CH // CNT_ROWS, CNT_ROWS, QB)

        def count(t, strict):
            def body(c, cnt):
                sc = key_major_chunk(c)
                hit = jnp.where(sc > t if strict else sc >= t, 1.0, 0.0)
                return cnt + jnp.sum(fold(hit), axis=0)

            cnt = lax.fori_loop(0, n_ch, body, jnp.zeros((CNT_ROWS, QB), F32))
            return jnp.sum(cnt, axis=0, keepdims=True)

        def extremes(c, mm):
            sc = key_major_chunk(c)
            return (jnp.minimum(mm[0], jnp.min(fold(jnp.where(sc > NEG_INF, sc, -NEG_INF)), axis=0)),
                    jnp.maximum(mm[1], jnp.max(fold(sc), axis=0)))

        smin, smax = lax.fori_loop(0, n_ch, extremes, (jnp.full((CNT_ROWS, QB), -NEG_INF, F32),
                                                       jnp.full((CNT_ROWS, QB), NEG_INF, F32)))
        smin = jnp.min(smin, axis=0, keepdims=True)
        smax = jnp.max(smax, axis=0, keepdims=True)
        qrow_l = i * QB + lax.broadcasted_iota(I32, (1, QB), 1)
        lo = _bisect_threshold(functools.partial(count, strict=False), smin, smax,
                               jnp.logical_and(qrow_l + 1 > TOPK, smax > smin))
        lo_b = jnp.broadcast_to(lo, (QB, QB)).T

        excess = count(lo, strict=False) > TOPK
        tie_quota = jnp.where(excess, TOPK - count(lo, strict=True), TOPK)
        has_excess = jnp.max(jnp.where(excess, 1, 0)) > 0
        quota_b = jnp.broadcast_to(tie_quota, (QB, QB)).T

        m_ref[...] = jnp.full(m_ref.shape, NEG, F32)
        l_ref[...] = jnp.zeros_like(l_ref)
        acc_ref[...] = jnp.zeros_like(acc_ref)
        n_t = ATT_CH // LANES

        def plain_bias(c):
            return [jnp.where(s_ref[c, :, t * LANES:(t + 1) * LANES] >= lo_b, 0.0, NEG)
                    for t in range(n_t)]

        def quota_bias(c, seen):
            keys = [s_ref[c, :, t * LANES:(t + 1) * LANES] for t in range(n_t)]
            eq = [jnp.where(k == lo_b, 1.0, 0.0) for k in keys]
            incl = jnp.dot(jnp.concatenate(eq, axis=1).astype(BF16), tri_ref[...],
                           preferred_element_type=F32)
            bias = []
            for t in range(n_t):
                rank = seen + incl[:, t * LANES:(t + 1) * LANES] - eq[t]
                keep = jnp.logical_or(keys[t] > lo_b,
                                      jnp.logical_and(keys[t] == lo_b, rank < quota_b))
                bias.append(jnp.where(keep, 0.0, NEG))
            return bias, seen + jnp.broadcast_to(incl[:, ATT_CH - 1:ATT_CH], (QB, LANES))

        def masked_logits(c, bias):
            start = pl.multiple_of(c * ATT_CH, ATT_CH)
            a = lax.dot_general(qbd_ref[...], kb_ref[pl.ds(start, ATT_CH), :], NT_DIMS,
                                preferred_element_type=F32)
            return start, [[a[h * QB:(h + 1) * QB, t * LANES:(t + 1) * LANES] + bias[t]
                            for t in range(n_t)] for h in range(N_HEADS)]

        def max_chunk(c, carry):
            _, a = masked_logits(c, plain_bias(c))
            for h in range(N_HEADS):
                rows = slice(h * QB, (h + 1) * QB)
                m = m_ref[rows, :]
                for t in range(n_t):
                    m = jnp.maximum(m, a[h][t])
                m_ref[rows, :] = m
            return carry

        if bound_max:
            for h in range(N_HEADS):
                rows = slice(h * QB, (h + 1) * QB)
                qf = qbd_ref[rows, :].astype(F32)
                qn2 = jnp.sum(qf * qf, axis=1, keepdims=True)
                m_ref[rows, :] = jnp.sqrt(jnp.broadcast_to(qn2, (QB, LANES))
                                          * kmax_ref[h // GROUP]) * BOUND_MARGIN
        else:
            lax.fori_loop(0, n_ch, max_chunk, 0)
            m_ref[...] = jnp.broadcast_to(jnp.max(m_ref[...], axis=1, keepdims=True), m_ref.shape)

        def value_chunk(c, seen, parity, with_quota):
            pb_ref = p_ref.at[parity]
            if with_quota:
                bias, seen = quota_bias(c, seen)
            else:
                bias = plain_bias(c)
            start, a = masked_logits(c, bias)
            for h in range(N_HEADS):
                rows = slice(h * QB, (h + 1) * QB)
                m = m_ref[rows, :]
                l = l_ref[rows, :]
                for t in range(n_t):
                    p = jnp.exp(a[h][t] - m)
                    l = l + p
                    pb_ref[rows, t * LANES:(t + 1) * LANES] = p.astype(BF16)
                l_ref[rows, :] = l
            acc_ref[...] += jnp.dot(pb_ref[...], vb_ref[pl.ds(start, ATT_CH), :],
                                    preferred_element_type=F32)
            return seen

        seen0 = jnp.zeros((QB, LANES), F32)

        @pl.when(has_excess)
        def _():
            lax.fori_loop(0, n_ch, functools.partial(value_chunk, parity=0, with_quota=True), seen0)

        @pl.when(jnp.logical_not(has_excess))
        def _():
            _pair_loop(n_ch, functools.partial(value_chunk, with_quota=False), seen0)

        l = jnp.sum(l_ref[...], axis=1, keepdims=True)
        lmin_ref[...] = jnp.broadcast_to(jnp.min(l, axis=0, keepdims=True), lmin_ref.shape)
        o = acc_ref[...] / l
        for j, slab in enumerate(_gather_heads(o, QB)):
            o_ref[:, j * LANES:(j + 1) * LANES] = slab.astype(BF16)


def _attn_prompt(qi, wi, q, kib, kb, vb, bound_max):
    row = lambda i: (i, 0)
    full = lambda i: (0, 0)
    nrow = N_HEADS * QB
    return pl.pallas_call(
        functools.partial(_attn_prompt_kernel, bound_max=bound_max),
        grid=(R // QB,),
        in_specs=[
            pl.BlockSpec((QB, N_IDX_HEADS * IDX_DIM), row),
            pl.BlockSpec((QB, LANES), row),
            pl.BlockSpec((QB, D_ATTN), row),
            pl.BlockSpec((R, LANES), full),
            pl.BlockSpec((R, D_KV), full),
            pl.BlockSpec((R, D_KV), full),
        ],
        out_specs=[pl.BlockSpec((QB, D_ATTN), row),
                   pl.BlockSpec((1, SUBLANES, LANES), lambda i: (i, 0, 0))],
        out_shape=(jax.ShapeDtypeStruct((R, D_ATTN), BF16),
                   jax.ShapeDtypeStruct((R // QB, SUBLANES, LANES), F32)),
        scratch_shapes=[
            pltpu.VMEM((N_ATT_CH, QB, ATT_CH), F32),
            pltpu.VMEM((R, QB), F32),
            pltpu.VMEM((nrow, LANES), BF16),
            pltpu.VMEM((N_IDX_HEADS, QB, LANES), F32),
            pltpu.VMEM((nrow, D_KV), BF16),
            pltpu.VMEM((nrow, LANES), F32),
            pltpu.VMEM((nrow, LANES), F32),
            pltpu.VMEM((nrow, D_KV), F32),
            pltpu.VMEM((2, nrow, ATT_CH), BF16),
            pltpu.VMEM((N_KV_HEADS, QB, LANES), F32),
            pltpu.VMEM((ATT_CH, ATT_CH), BF16),
        ],
        compiler_params=_cparams(("arbitrary",)),
        name="attn_prompt_bound" if bound_max else "attn_prompt",
    )(qi, wi, q, kib, kb, vb)


S_CH = 8192
S_NCH = PAST_LEN // S_CH
S_PPC = S_CH // PAGE_SIZE
DMA_UNROLL = 8
S_TIE = 512
S_ROWS = N_HEADS * DEC_SEQ


def _page_lanes(p):
    return pl.ds(pl.multiple_of(p * PAGE_SIZE, PAGE_SIZE), PAGE_SIZE)


def _kidx_page_copy(pt_ref, kidx_hbm, kbuf, sem, b, p, slot):
    return pltpu.make_async_copy(
        kidx_hbm.at[pt_ref[b, p]], kbuf.at[slot, :, _page_lanes(p)], sem.at[slot])


def _sample_score_kernel(pt_ref, qi_ref, wi_ref, kin_ref, kidx_hbm, s_ref, sn_ref,
                         kbuf, sem, knp_ref):
    b = pl.program_id(0)
    nb = pl.num_programs(0)
    slot = b % 2

    def start_batch(bb, sl):
        def body(p, c):
            _kidx_page_copy(pt_ref, kidx_hbm, kbuf, sem, bb, p, sl).start()
            return c
        lax.fori_loop(0, N_PAGES, body, 0, unroll=DMA_UNROLL)

    @pl.when(b == 0)
    def _():
        start_batch(0, 0)

    @pl.when(b + 1 < nb)
    def _():
        start_batch(b + 1, 1 - slot)

    def wait_body(p, c):
        _kidx_page_copy(pt_ref, kidx_hbm, kbuf, sem, b, p, slot).wait()
        return c
    lax.fori_loop(0, N_PAGES, wait_body, 0, unroll=DMA_UNROLL)

    qi = qi_ref[0]
    qis = jnp.concatenate(
        [qi[:, h * IDX_DIM:(h + 1) * IDX_DIM] for h in range(N_IDX_HEADS)], axis=0).astype(BF16)
    wi = wi_ref[0]
    wb = [jnp.broadcast_to(wi[:, h:h + 1], (DEC_SEQ, LANES)) for h in range(N_IDX_HEADS)]

    def head_sum(d, width):
        outs = []
        for t in range(width // LANES):
            acc = jnp.zeros((DEC_SEQ, LANES), F32)
            for h in range(N_IDX_HEADS):
                acc = acc + wb[h] * jnp.maximum(
                    d[h * DEC_SEQ:(h + 1) * DEC_SEQ, t * LANES:(t + 1) * LANES], 0.0)
            outs.append(acc)
        return outs

    def score_chunk(c, carry):
        kc = kbuf[slot, :, pl.ds(pl.multiple_of(c * S_CH, S_CH), S_CH)].astype(BF16)
        d = jnp.dot(qis, kc, preferred_element_type=F32)
        for t, acc in enumerate(head_sum(d, S_CH)):
            s_ref[c, :, t * LANES:(t + 1) * LANES] = acc
        return carry
    lax.fori_loop(0, S_NCH, score_chunk, 0)

    knp_ref[...] = jnp.zeros_like(knp_ref)
    knp_ref[:DEC_SEQ, :] = kin_ref[0][:, :IDX_DIM]
    d = lax.dot_general(qis, knp_ref[...].astype(BF16), NT_DIMS, preferred_element_type=F32)
    kpos = lax.broadcasted_iota(I32, (DEC_SEQ, LANES), 1)
    srow = lax.broadcasted_iota(I32, (DEC_SEQ, LANES), 0)
    sn_ref[...] = jnp.where(kpos <= srow, head_sum(d, LANES)[0], NEG_INF)


def _sample_scores(page_table, qi_s, wi_s, ki_s, cache_kidx):
    blk = lambda b, pt: (b, 0, 0)
    grid_spec = pltpu.PrefetchScalarGridSpec(
        num_scalar_prefetch=1,
        grid=(DEC_BATCH,),
        in_specs=[
            pl.BlockSpec((1, DEC_SEQ, N_IDX_HEADS * IDX_DIM), blk),
            pl.BlockSpec((1, DEC_SEQ, LANES), blk),
            pl.BlockSpec((1, DEC_SEQ, LANES), blk),
            pl.BlockSpec(memory_space=pl.ANY),
        ],
        out_specs=[
            pl.BlockSpec((S_NCH, DEC_SEQ, S_CH), lambda b, pt: (0, b, 0)),
            pl.BlockSpec((DEC_SEQ, LANES), lambda b, pt: (b, 0)),
        ],
        scratch_shapes=[
            pltpu.VMEM((2, IDX_DIM, PAST_LEN), F32),
            pltpu.SemaphoreType.DMA((2,)),
            pltpu.VMEM((LANES, IDX_DIM), F32),
        ],
    )
    return pl.pallas_call(
        _sample_score_kernel,
        grid_spec=grid_spec,
        out_shape=(
            jax.ShapeDtypeStruct((S_NCH, NS, S_CH), F32),
            jax.ShapeDtypeStruct((NS, LANES), F32),
        ),
        compiler_params=_cparams(("arbitrary",)),
        name="sample_scores",
    )(page_table, qi_s, wi_s, ki_s, cache_kidx)


def _sample_threshold_kernel(s_ref, sn_ref, bp_ref, bn_ref):
    def lane_fold(x, op):
        out = x[:, :LANES]
        for t in range(1, S_CH // LANES):
            out = op(out, x[:, t * LANES:(t + 1) * LANES])
        return out

    def count(t, strict):
        def hits(sc):
            return jnp.where(sc > t if strict else sc >= t, 1.0, 0.0)

        cnt = lax.fori_loop(0, S_NCH, lambda c, cnt: cnt + lane_fold(hits(s_ref[c]), jnp.add),
                            hits(sn_ref[...]))
        return jnp.sum(cnt, axis=1, keepdims=True)

    def extremes(c, mm):
        return (jnp.minimum(mm[0], lane_fold(s_ref[c], jnp.minimum)),
                jnp.maximum(mm[1], lane_fold(s_ref[c], jnp.maximum)))

    new = sn_ref[...]
    smin, smax = lax.fori_loop(0, S_NCH, extremes, (jnp.where(new > NEG_INF, new, -NEG_INF), new))
    smin = jnp.min(smin, axis=1, keepdims=True)
    smax = jnp.max(smax, axis=1, keepdims=True)
    lo = _bisect_threshold(functools.partial(count, strict=False), smin, smax, smax > smin)

    quota = jnp.where(count(lo, strict=False) > TOPK, TOPK - count(lo, strict=True), TOPK)
    tri = _inclusive_prefix_matrix(S_TIE)

    def mask_tile(keys, seen, tri_t):
        eq = jnp.where(keys == lo, 1.0, 0.0)
        incl = jnp.dot(eq.astype(BF16), tri_t, preferred_element_type=F32)
        keep = jnp.logical_or(keys > lo, jnp.logical_and(keys == lo, seen + incl - eq < quota))
        return jnp.where(keep, 0.0, NEG), seen + incl[:, keys.shape[1] - 1:]

    def chunk(c, seen):
        for t in range(S_CH // S_TIE):
            cols = slice(t * S_TIE, (t + 1) * S_TIE)
            bp_ref[c, :, cols], seen = mask_tile(s_ref[c, :, cols], seen, tri)
        return seen

    seen = lax.fori_loop(0, S_NCH, chunk, jnp.zeros((NS, 1), F32))
    bn_ref[...], _ = mask_tile(sn_ref[...], seen, tri[:LANES, :LANES])


def _sample_threshold(keys_past, keys_new):
    vmem = pl.BlockSpec(memory_space=pltpu.VMEM)
    return pl.pallas_call(
        _sample_threshold_kernel,
        in_specs=[vmem, vmem],
        out_specs=[vmem, vmem],
        out_shape=(
            jax.ShapeDtypeStruct((S_NCH, NS, S_CH), F32),
            jax.ShapeDtypeStruct((NS, LANES), F32),
        ),
        compiler_params=pltpu.CompilerParams(vmem_limit_bytes=VMEM_LIMIT),
        name="sample_threshold",
    )(keys_past, keys_new)


def _kv_page_copies(pt_ref, ck_hbm, cv_hbm, kbuf, vbuf, sem, b, c, p, slot):
    page = pt_ref[b, c * S_PPC + p]
    dst = _page_lanes(p)
    return (pltpu.make_async_copy(ck_hbm.at[page], kbuf.at[slot, :, dst], sem.at[0, slot]),
            pltpu.make_async_copy(cv_hbm.at[page], vbuf.at[slot, :, dst], sem.at[1, slot]))


def _sample_attn_kernel(pt_ref, q_ref, bp_ref, bn_ref, kn_ref, vn_ref, ck_hbm, cv_hbm, o_ref,
                        kbuf, vbuf, sem, qbd_ref, m_ref, l_ref, acc_ref, knp_ref, vnp_ref):
    b = pl.program_id(0)
    c = pl.program_id(1)
    step = b * S_NCH + c
    n_steps = pl.num_programs(0) * S_NCH
    slot = step % 2

    def start_chunk(st, sl):
        bb = st // S_NCH
        cc = st % S_NCH

        def body(p, carry):
            for cp in _kv_page_copies(pt_ref, ck_hbm, cv_hbm, kbuf, vbuf, sem, bb, cc, p, sl):
                cp.start()
            return carry
        lax.fori_loop(0, S_PPC, body, 0, unroll=DMA_UNROLL)

    @pl.when(step == 0)
    def _():
        start_chunk(0, 0)

    @pl.when(step + 1 < n_steps)
    def _():
        start_chunk(step + 1, 1 - slot)

    def wait_body(p, carry):
        for cp in _kv_page_copies(pt_ref, ck_hbm, cv_hbm, kbuf, vbuf, sem, b, c, p, slot):
            cp.wait()
        return carry
    lax.fori_loop(0, S_PPC, wait_body, 0, unroll=DMA_UNROLL)

    @pl.when(c == 0)
    def _():
        _build_qbd(q_ref.at[0], qbd_ref, DEC_SEQ)
        m_ref[...] = jnp.full(m_ref.shape, NEG, F32)
        l_ref[...] = jnp.zeros_like(l_ref)
        acc_ref[...] = jnp.zeros_like(acc_ref)

    def update(kc, vc, bias, feature_major):
        qbd = qbd_ref[...].astype(BF16)
        if feature_major:
            a = jnp.dot(qbd, kc, preferred_element_type=F32)
        else:
            a = lax.dot_general(qbd, kc, NT_DIMS, preferred_element_type=F32)
        a3 = a.reshape(N_HEADS, DEC_SEQ, a.shape[1]) + bias[None]
        _softmax_step(a3, m_ref, l_ref, acc_ref, vc, DEC_SEQ, feature_major)

    update(kbuf[slot].astype(BF16), vbuf[slot].astype(BF16), bp_ref[0], True)

    @pl.when(c == S_NCH - 1)
    def _():
        knp_ref[...] = jnp.zeros_like(knp_ref)
        vnp_ref[...] = jnp.zeros_like(vnp_ref)
        knp_ref[:DEC_SEQ, :] = kn_ref[0]
        vnp_ref[:DEC_SEQ, :] = vn_ref[0]
        update(knp_ref[...].astype(BF16), vnp_ref[...].astype(BF16), bn_ref[...], False)
        o = acc_ref[...] / l_ref[...]
        for j, slab in enumerate(_gather_heads(o, DEC_SEQ)):
            o_ref[0, :, j * LANES:(j + 1) * LANES] = slab


def _sample_attn(page_table, q_s, bias_past, bias_new, k_s, v_s, cache_k, cache_v):
    blk = lambda b, c, pt: (b, 0, 0)
    grid_spec = pltpu.PrefetchScalarGridSpec(
        num_scalar_prefetch=1,
        grid=(DEC_BATCH, S_NCH),
        in_specs=[
            pl.BlockSpec((1, DEC_SEQ, D_ATTN), blk),
            pl.BlockSpec((1, DEC_SEQ, S_CH), lambda b, c, pt: (c, b, 0)),
            pl.BlockSpec((DEC_SEQ, LANES), lambda b, c, pt: (b, 0)),
            pl.BlockSpec((1, DEC_SEQ, D_KV), blk),
            pl.BlockSpec((1, DEC_SEQ, D_KV), blk),
            pl.BlockSpec(memory_space=pl.ANY),
            pl.BlockSpec(memory_space=pl.ANY),
        ],
        out_specs=pl.BlockSpec((1, DEC_SEQ, D_ATTN), blk),
        scratch_shapes=[
            pltpu.VMEM((2, D_KV, S_CH), F32),
            pltpu.VMEM((2, D_KV, S_CH), F32),
            pltpu.SemaphoreType.DMA((2, 2)),
            pltpu.VMEM((S_ROWS, D_KV), F32),
            pltpu.VMEM((S_ROWS, 1), F32),
            pltpu.VMEM((S_ROWS, 1), F32),
            pltpu.VMEM((S_ROWS, D_KV), F32),
            pltpu.VMEM((LANES, D_KV), F32),
            pltpu.VMEM((LANES, D_KV), F32),
        ],
    )
    return pl.pallas_call(
        _sample_attn_kernel,
        grid_spec=grid_spec,
        out_shape=jax.ShapeDtypeStruct((DEC_BATCH, DEC_SEQ, D_ATTN), F32),
        compiler_params=_cparams(("arbitrary", "arbitrary")),
        name="sample_attn",
    )(page_table, q_s, bias_past, bias_new, k_s, v_s, cache_k, cache_v)


def _rope_tables():
    pos = np.zeros((R,), np.float32)
    pos[:T] = np.arange(T)
    pos[ROW_S:ROW_S + NS] = np.tile(PAST_LEN + np.arange(DEC_SEQ), DEC_BATCH)
    half = HEAD_DIM // 2
    inv_freq = ROPE_THETA ** (-(jnp.arange(half, dtype=F32) * 2.0 / HEAD_DIM))
    ang = jnp.asarray(pos)[:, None] * inv_freq
    cos = jnp.tile(jnp.cos(ang), (1, LANES // half))
    sin = jnp.sin(ang)
    sin = jnp.tile(jnp.concatenate([-sin, sin], axis=1), (1, LANES // HEAD_DIM))
    return cos, sin


def kernel(x_prompt, x_sample, cache_k, cache_v, cache_kidx, state_pool, page_table, meta_tokens,
           norm_mix_pre, w_in, idx_k_norm, w_pool, pool_scale, w_attn_o, w_out, norm_mix_post,
           norm_ffn_pre, w_up, w_down, norm_ffn_post):
    x_all = jnp.concatenate([
        meta_tokens, x_prompt[0], jnp.zeros((ROW_S - T, D_MODEL), F32),
        x_sample.reshape(NS, D_MODEL), jnp.zeros((R - ROW_S - NS, D_MODEL), F32)], axis=0)

    wt = jnp.transpose(w_in[0])
    o_u, o_q, o_k, o_v, o_qi, o_ki, o_wi, o_gp = np.cumsum((0, D_POOL, D_ATTN, D_KV, D_KV,
                                                            N_IDX_HEADS * IDX_DIM, IDX_DIM, N_IDX_HEADS))
    n_pad = W_ROW_C - W_B_COLS - (o_v - o_q) - (o_gp - o_qi)
    w_all = jnp.concatenate([
        wt[o_v:o_qi], wt[o_u:o_q], wt[o_q:o_v], wt[o_qi:o_gp],
        jnp.zeros((n_pad, D_MODEL), F32), wt[o_gp:]], axis=0).astype(BF16)
    assert w_all.shape == (W_ROWS, D_MODEL)

    g_pre = norm_mix_pre[0][None]
    cos, sin = _rope_tables()
    gk = jnp.tile(idx_k_norm[0], LANES // IDX_DIM)[None]

    q, k, kb, qi, ki, kib, wi = _proj_a(x_all, g_pre, w_all, cos, sin, gk)
    v, vb, u = _proj_b(x_all, g_pre, w_all)
    gates = _proj_c(x_all, g_pre, w_all)

    def sample_rows(a):
        return a[ROW_S:ROW_S + NS].reshape(DEC_BATCH, DEC_SEQ, a.shape[1])

    pooled = _pool_prompt(u)
    u_s = sample_rows(u)
    state = state_pool[0]
    state16 = jnp.pad(state, ((0, 0), (HALO - POOL_BUF, 0), (0, 0)))
    pooled_s = _pool_sample(u_s, state16)
    pooled = lax.dynamic_update_slice(pooled, pooled_s.reshape(NS, D_POOL), (ROW_S, 0))

    attn, lmin = _attn_prompt(qi, wi, q, kib, kb, vb, bound_max=True)
    attn = lax.cond(jnp.min(lmin) >= MIN_ROW_SUM, lambda: attn,
                    lambda: _attn_prompt(qi, wi, q, kib, kb, vb, bound_max=False)[0])
    k_s, v_s = sample_rows(k), sample_rows(v)
    n_pool = cache_k.shape[1]
    kidx_fm = jnp.transpose(cache_kidx[0], (0, 2, 1))
    k_fm = jnp.transpose(cache_k[0], (0, 2, 3, 1)).reshape(n_pool, D_KV, PAGE_SIZE)
    v_fm = jnp.transpose(cache_v[0], (0, 2, 3, 1)).reshape(n_pool, D_KV, PAGE_SIZE)
    keys_past, keys_new = _sample_scores(page_table, sample_rows(qi), sample_rows(wi), sample_rows(ki),
                                         kidx_fm)
    bias_past, bias_new = _sample_threshold(keys_past, keys_new)
    attn_s = _sample_attn(page_table, sample_rows(q), bias_past, bias_new, k_s, v_s, k_fm, v_fm)
    attn = lax.dynamic_update_slice(attn, attn_s.reshape(NS, D_ATTN).astype(BF16), (ROW_S, 0))

    x1 = _merge(x_all, pooled, attn, gates, w_pool[0].astype(BF16), pool_scale[0][None],
                w_attn_o[0].astype(BF16), w_out[0].astype(BF16), norm_mix_post[0][None])
    y = _ffn(x1, norm_ffn_pre[0][None], w_up[0].astype(BF16), w_down[0].astype(BF16),
             norm_ffn_post[0][None])

    y_prompt = y[N_META:T][None]
    y_sample = y[ROW_S:ROW_S + NS].reshape(DEC_BATCH, DEC_SEQ, D_MODEL)
    kv_shape = (1, 1, T, N_KV_HEADS, HEAD_DIM)
    kv_s_shape = (1, DEC_BATCH, DEC_SEQ, N_KV_HEADS, HEAD_DIM)
    return (
        y_prompt, y_sample,
        k[:T].reshape(kv_shape), v[:T].reshape(kv_shape), ki[:T, :IDX_DIM][None, None],
        u[T - POOL_BUF:T][None, None],
        k_s.reshape(kv_s_shape), v_s.reshape(kv_s_shape), sample_rows(ki)[None, :, :, :IDX_DIM],
        jnp.concatenate([state[:, DEC_SEQ:], u_s], axis=1)[None],
    )
```

```python
import functools

import jax
import jax.numpy as jnp
import numpy as np
from jax import lax
from jax.experimental import pallas as pl
from jax.experimental.pallas import tpu as pltpu

F32 = jnp.float32
BF16 = jnp.bfloat16
I32 = jnp.int32

D_MODEL = 2048
SEQ = 8192
DEC_BATCH = 32
DEC_SEQ = 8
PAST_LEN = 16384
PAGE_SIZE = 128
N_PAGES = PAST_LEN // PAGE_SIZE
N_META = 16
N_HEADS = 16
N_KV_HEADS = 4
HEAD_DIM = 64
GROUP = N_HEADS // N_KV_HEADS
D_ATTN = N_HEADS * HEAD_DIM
D_KV = N_KV_HEADS * HEAD_DIM
ATTN_SCALE = HEAD_DIM ** -0.5
N_IDX_HEADS = 16
IDX_DIM = 64
INDEX_W_SCALE = (N_IDX_HEADS ** -0.5) * (IDX_DIM ** -0.5)
TOPK = 256
POOL_WINDOWS = (2, 4, 8, 16)
N_POOL_GROUPS = 4
D_POOL = D_MODEL // 2
POOL_GROUP = D_POOL // N_POOL_GROUPS
POOL_OUT_GROUP = D_MODEL // N_POOL_GROUPS
POOL_BUF = max(POOL_WINDOWS) - 1
D_FF = 4 * D_MODEL
ROPE_THETA = 10000.0
EPS = 1e-6

LANES = 128
SUBLANES = 8
T = SEQ + N_META
QB = 128
N_QBLK = -(-T // QB)
ROW_S = N_QBLK * QB
NS = DEC_BATCH * DEC_SEQ
R = 8704
VMEM_LIMIT = 56 * 1024 * 1024

NEG = -1e30


def _cparams(sem):
    return pltpu.CompilerParams(dimension_semantics=sem, vmem_limit_bytes=VMEM_LIMIT)


def _rms(x, g):
    return x * lax.rsqrt(jnp.mean(x * x, axis=-1, keepdims=True) + EPS) * g


def _swap_halves(x):
    lane = lax.broadcasted_iota(I32, x.shape, 1)
    return jnp.where(lane % HEAD_DIM < HEAD_DIM // 2,
                     pltpu.roll(x, LANES - HEAD_DIM // 2, 1),
                     pltpu.roll(x, HEAD_DIM // 2, 1))


def _rope_cols(x, cos, sin):
    outs = []
    for c in range(x.shape[1] // LANES):
        xc = x[:, c * LANES:(c + 1) * LANES]
        outs.append(xc * cos + _swap_halves(xc) * sin)
    return outs


PROJ_TM = 544
N_ROPE_A = D_ATTN + D_KV
PROJ_TN_A = N_ROPE_A
W_A_COLS = 2 * PROJ_TN_A
W_B_COLS = D_KV + D_POOL
W_C_COLS = 2 * D_MODEL
PROJ_TN_C = 1024
PROJ_TM_C = 1088
W_ROW_B = 0
W_ROW_A = W_B_COLS
W_ROW_C = 4 * PROJ_TN_C
W_ROWS = W_ROW_C + W_C_COLS
assert W_ROW_A % PROJ_TN_A == 0 and W_ROW_A + W_A_COLS <= W_ROW_C


def _norm_to_scratch(x_ref, g_ref, xn_ref):
    @pl.when(pl.program_id(1) == 0)
    def _():
        xn_ref[...] = _rms(x_ref[...], g_ref[...]).astype(BF16)


def _proj_dot(xn_ref, w_ref):
    return lax.dot_general(xn_ref[...], w_ref[...], (((1,), (1,)), ((), ())),
                           preferred_element_type=F32)


PROJ_PIECE = 256


def _proj_a_kernel(x_ref, g_ref, w0_ref, w1_ref, cos_ref, sin_ref, gk_ref,
                   q_ref, k_ref, kb_ref, qi_ref, ki_ref, kib_ref, wi_ref, xn_ref):
    xn_ref[...] = _rms(x_ref[...], g_ref[...]).astype(BF16)
    cos = cos_ref[...]
    sin = sin_ref[...]
    n_q, n_k, n_qi = D_ATTN // LANES, D_KV // LANES, N_IDX_HEADS * IDX_DIM // LANES

    def rope(slab):
        return slab * cos + _swap_halves(slab) * sin

    def emit(g, slab):
        if g < n_q:
            q_ref[:, g * LANES:(g + 1) * LANES] = rope(slab)
        elif g < n_q + n_k:
            c = g - n_q
            r = rope(slab)
            k_ref[:, c * LANES:(c + 1) * LANES] = r
            kb_ref[:, c * LANES:(c + 1) * LANES] = r.astype(BF16)
        elif g < n_q + n_k + n_qi:
            c = g - n_q - n_k
            qi_ref[:, c * LANES:(c + 1) * LANES] = rope(slab)
        elif g == n_q + n_k + n_qi:
            lane = lax.broadcasted_iota(I32, slab.shape, 1)
            is_ki = lane < IDX_DIM
            ms = jnp.sum(jnp.where(is_ki, slab * slab, 0.0), axis=-1, keepdims=True) / IDX_DIM
            kin = rope(slab * lax.rsqrt(ms + EPS) * gk_ref[...])
            ki2 = jnp.where(is_ki, kin, pltpu.roll(kin, IDX_DIM, 1))
            ki_ref[...] = ki2
            kib_ref[...] = ki2.astype(BF16)
            wi_ref[...] = pltpu.roll(slab, LANES - IDX_DIM, 1) * INDEX_W_SCALE

    per_block = PROJ_TN_A // PROJ_PIECE
    for p in range(W_A_COLS // PROJ_PIECE):
        w_ref = (w0_ref, w1_ref)[p // per_block]
        r0 = (p % per_block) * PROJ_PIECE
        d = lax.dot_general(xn_ref[...], w_ref[r0:r0 + PROJ_PIECE, :], (((1,), (1,)), ((), ())),
                            preferred_element_type=F32)
        for e in range(PROJ_PIECE // LANES):
            emit(p * (PROJ_PIECE // LANES) + e, d[:, e * LANES:(e + 1) * LANES])


def _proj_a(x_all, g, w_a, cos, sin, gk):
    n_i = R // PROJ_TM
    row = lambda i: (i, 0)
    const = lambda i: (0, 0)
    first = W_ROW_A // PROJ_TN_A
    outs = (
        jax.ShapeDtypeStruct((R, D_ATTN), F32),
        jax.ShapeDtypeStruct((R, D_KV), F32),
        jax.ShapeDtypeStruct((R, D_KV), BF16),
        jax.ShapeDtypeStruct((R, N_IDX_HEADS * IDX_DIM), F32),
        jax.ShapeDtypeStruct((R, LANES), F32),
        jax.ShapeDtypeStruct((R, LANES), BF16),
        jax.ShapeDtypeStruct((R, LANES), F32),
    )
    return pl.pallas_call(
        _proj_a_kernel,
        grid=(n_i,),
        in_specs=[
            pl.BlockSpec((PROJ_TM, D_MODEL), row),
            pl.BlockSpec((1, D_MODEL), const),
            pl.BlockSpec((PROJ_TN_A, D_MODEL), lambda i: (first, 0)),
            pl.BlockSpec((PROJ_TN_A, D_MODEL), lambda i: (first + 1, 0)),
            pl.BlockSpec((PROJ_TM, LANES), row),
            pl.BlockSpec((PROJ_TM, LANES), row),
            pl.BlockSpec((1, LANES), const),
        ],
        out_specs=[pl.BlockSpec((PROJ_TM, o.shape[1]), row) for o in outs],
        out_shape=outs,
        scratch_shapes=[pltpu.VMEM((PROJ_TM, D_MODEL), BF16)],
        compiler_params=_cparams(("arbitrary",)),
        name="proj_rope",
    )(x_all, g, w_a, w_a, cos, sin, gk)


def _proj_b_kernel(x_ref, g_ref, w_ref, v_ref, vb_ref, u_ref, xn_ref):
    _norm_to_scratch(x_ref, g_ref, xn_ref)
    p = _proj_dot(xn_ref, w_ref)
    v_ref[...] = p[:, :D_KV]
    vb_ref[...] = p[:, :D_KV].astype(BF16)
    u_ref[...] = p[:, D_KV:]


def _proj_b(x_all, g, w_b):
    n_i = R // PROJ_TM
    row = lambda i, j: (i, 0)
    outs = (
        jax.ShapeDtypeStruct((R, D_KV), F32),
        jax.ShapeDtypeStruct((R, D_KV), BF16),
        jax.ShapeDtypeStruct((R, D_POOL), F32),
    )
    return pl.pallas_call(
        _proj_b_kernel,
        grid=(n_i, 1),
        in_specs=[
            pl.BlockSpec((PROJ_TM, D_MODEL), row),
            pl.BlockSpec((1, D_MODEL), lambda i, j: (0, 0)),
            pl.BlockSpec((W_B_COLS, D_MODEL), lambda i, j: (W_ROW_B // W_B_COLS, 0)),
        ],
        out_specs=[pl.BlockSpec((PROJ_TM, o.shape[1]), row) for o in outs],
        out_shape=outs,
        scratch_shapes=[pltpu.VMEM((PROJ_TM, D_MODEL), BF16)],
        compiler_params=_cparams(("arbitrary", "arbitrary")),
        name="proj_vu",
    )(x_all, g, w_b)


def _proj_c_kernel(x_ref, g_ref, w_ref, o_ref, xn_ref):
    _norm_to_scratch(x_ref, g_ref, xn_ref)
    o_ref[...] = _proj_dot(xn_ref, w_ref)


def _proj_c(x_all, g, w_c):
    n_i = R // PROJ_TM_C
    return pl.pallas_call(
        _proj_c_kernel,
        grid=(n_i, W_C_COLS // PROJ_TN_C),
        in_specs=[
            pl.BlockSpec((PROJ_TM_C, D_MODEL), lambda i, j: (i, 0)),
            pl.BlockSpec((1, D_MODEL), lambda i, j: (0, 0)),
            pl.BlockSpec((PROJ_TN_C, D_MODEL), lambda i, j: (W_ROW_C // PROJ_TN_C + j, 0)),
        ],
        out_specs=pl.BlockSpec((PROJ_TM_C, PROJ_TN_C), lambda i, j: (i, j)),
        out_shape=jax.ShapeDtypeStruct((R, W_C_COLS), F32),
        scratch_shapes=[pltpu.VMEM((PROJ_TM_C, D_MODEL), BF16)],
        compiler_params=_cparams(("arbitrary", "arbitrary")),
        name="proj_gates",
    )(x_all, g, w_c)


POOL_TM = 512
HALO = 16


def _window_mean_minus_cur(ext_ref, rows, inv_cnt):
    outs = []
    for g, w in enumerate(POOL_WINDOWS):
        cols = slice(g * POOL_GROUP, (g + 1) * POOL_GROUP)
        cur = ext_ref[HALO:HALO + rows, cols]
        acc = cur
        for d in range(1, w):
            acc = acc + ext_ref[HALO - d:HALO - d + rows, cols]
        outs.append(acc * inv_cnt[g] - cur)
    return outs


def _pool_prompt_kernel(u_ref, halo_ref, o_ref, ext_ref):
    i = pl.program_id(0)
    ext_ref[HALO:, :] = u_ref[...]
    ext_ref[:HALO, :] = jnp.where(i == 0, 0.0, halo_ref[...])
    pos = i * POOL_TM + lax.broadcasted_iota(I32, (POOL_TM, 1), 0)
    inv_cnt = [1.0 / jnp.minimum(pos + 1, w).astype(F32) for w in POOL_WINDOWS]
    outs = _window_mean_minus_cur(ext_ref, POOL_TM, inv_cnt)
    for g in range(N_POOL_GROUPS):
        o_ref[:, g * POOL_GROUP:(g + 1) * POOL_GROUP] = outs[g]


def _pool_prompt(u):
    per = POOL_TM // HALO
    return pl.pallas_call(
        _pool_prompt_kernel,
        grid=(R // POOL_TM,),
        in_specs=[
            pl.BlockSpec((POOL_TM, D_POOL), lambda i: (i, 0)),
            pl.BlockSpec((HALO, D_POOL), lambda i: (jnp.maximum(i * per - 1, 0), 0)),
        ],
        out_specs=pl.BlockSpec((POOL_TM, D_POOL), lambda i: (i, 0)),
        out_shape=jax.ShapeDtypeStruct((R, D_POOL), F32),
        scratch_shapes=[pltpu.VMEM((POOL_TM + HALO, D_POOL), F32)],
        compiler_params=_cparams(("arbitrary",)),
        name="pool_prompt",
    )(u, u)


def _pool_sample_kernel(u_ref, st_ref, o_ref, ext_ref):
    for b in range(DEC_BATCH):
        ext_ref[:HALO, :] = st_ref[b]
        ext_ref[HALO:, :] = u_ref[b]
        inv_cnt = [1.0 / w for w in POOL_WINDOWS]
        outs = _window_mean_minus_cur(ext_ref, DEC_SEQ, inv_cnt)
        for g in range(N_POOL_GROUPS):
            o_ref[b, :, g * POOL_GROUP:(g + 1) * POOL_GROUP] = outs[g]


def _pool_sample(u_s, state16):
    return pl.pallas_call(
        _pool_sample_kernel,
        out_shape=jax.ShapeDtypeStruct((DEC_BATCH, DEC_SEQ, D_POOL), F32),
        scratch_shapes=[pltpu.VMEM((HALO + DEC_SEQ, D_POOL), F32)],
        compiler_params=pltpu.CompilerParams(vmem_limit_bytes=VMEM_LIMIT),
        name="pool_sample",
    )(u_s, state16)


MERGE_TM = 256


def _merge_kernel(x_ref, pooled_ref, attn_ref, gate_ref, wp_ref, ps_ref, wa_ref, wo_ref, gn_ref, o_ref):
    pooled = pooled_ref[...].astype(BF16)
    pool_out = jnp.concatenate(
        [jnp.dot(pooled[:, g * POOL_GROUP:(g + 1) * POOL_GROUP], wp_ref[g], preferred_element_type=F32)
         for g in range(N_POOL_GROUPS)], axis=1) * ps_ref[...]
    attn_out = jnp.dot(attn_ref[...], wa_ref[...], preferred_element_type=F32)
    gate = gate_ref[...]
    m = (jax.nn.sigmoid(gate[:, :D_MODEL]) * pool_out
         + jax.nn.sigmoid(gate[:, D_MODEL:]) * attn_out)
    mix = jnp.dot(m.astype(BF16), wo_ref[...], preferred_element_type=F32)
    o_ref[...] = x_ref[...] + _rms(mix, gn_ref[...])


def _merge(x_all, pooled, attn, gates, w_pool, pool_scale, w_attn_o, w_out, g_post):
    row = lambda i: (i, 0)
    const2 = lambda i: (0, 0)
    return pl.pallas_call(
        _merge_kernel,
        grid=(R // MERGE_TM,),
        in_specs=[
            pl.BlockSpec((MERGE_TM, D_MODEL), row),
            pl.BlockSpec((MERGE_TM, D_POOL), row),
            pl.BlockSpec((MERGE_TM, D_ATTN), row),
            pl.BlockSpec((MERGE_TM, 2 * D_MODEL), row),
            pl.BlockSpec((N_POOL_GROUPS, POOL_GROUP, POOL_OUT_GROUP), lambda i: (0, 0, 0)),
            pl.BlockSpec((1, D_MODEL), const2),
            pl.BlockSpec((D_ATTN, D_MODEL), const2),
            pl.BlockSpec((D_MODEL, D_MODEL), const2),
            pl.BlockSpec((1, D_MODEL), const2),
        ],
        out_specs=pl.BlockSpec((MERGE_TM, D_MODEL), row),
        out_shape=jax.ShapeDtypeStruct((R, D_MODEL), F32),
        compiler_params=_cparams(("arbitrary",)),
        name="merge",
    )(x_all, pooled, attn, gates, w_pool, pool_scale, w_attn_o, w_out, g_post)


FFN_TM = 544
FFN_TF = 1024


def _ffn_kernel(x_ref, gpre_ref, wu_ref, wd_ref, gpost_ref, o_ref, h_ref, acc_ref):
    j = pl.program_id(1)

    @pl.when(j == 0)
    def _():
        h_ref[...] = _rms(x_ref[...], gpre_ref[...]).astype(BF16)
        acc_ref[...] = jnp.zeros_like(acc_ref)

    a = jnp.maximum(jnp.dot(h_ref[...], wu_ref[...], preferred_element_type=F32), 0.0)
    acc_ref[...] += jnp.dot((a * a).astype(BF16), wd_ref[...], preferred_element_type=F32)

    @pl.when(j == pl.num_programs(1) - 1)
    def _():
        o_ref[...] = x_ref[...] + _rms(acc_ref[...], gpost_ref[...])


def _ffn(x1, g_pre, w_up, w_down, g_post):
    return pl.pallas_call(
        _ffn_kernel,
        grid=(R // FFN_TM, D_FF // FFN_TF),
        in_specs=[
            pl.BlockSpec((FFN_TM, D_MODEL), lambda i, j: (i, 0)),
            pl.BlockSpec((1, D_MODEL), lambda i, j: (0, 0)),
            pl.BlockSpec((D_MODEL, FFN_TF), lambda i, j: (0, j)),
            pl.BlockSpec((FFN_TF, D_MODEL), lambda i, j: (j, 0)),
            pl.BlockSpec((1, D_MODEL), lambda i, j: (0, 0)),
        ],
        out_specs=pl.BlockSpec((FFN_TM, D_MODEL), lambda i, j: (i, 0)),
        out_shape=jax.ShapeDtypeStruct((R, D_MODEL), F32),
        scratch_shapes=[pltpu.VMEM((FFN_TM, D_MODEL), BF16), pltpu.VMEM((FFN_TM, D_MODEL), F32)],
        compiler_params=_cparams(("arbitrary", "arbitrary")),
        name="ffn",
    )(x1, g_pre, w_up, w_down, g_post)


NEG_INF = float("-inf")
MAX_BISECT = 192
STEPS_PER_TRIP = 4


def _above(x):
    return x + (jnp.abs(x) * 2.0 ** -20 + 1e-30)


def _bisect_threshold(count_ge, smin, smax, active0):
    def cond(st):
        it, _, _, active = st
        return jnp.logical_and(it < MAX_BISECT, jnp.max(active) > 0)

    def step(lo, hi, active):
        mid = 0.5 * lo + 0.5 * hi
        cnt = count_ge(mid)
        on = active > 0
        ge = cnt >= TOPK
        still = jnp.logical_and(cnt != TOPK, jnp.logical_and(mid != lo, mid != hi))
        lo = jnp.where(jnp.logical_and(on, ge), mid, lo)
        hi = jnp.where(jnp.logical_and(on, jnp.logical_not(ge)), mid, hi)
        return lo, hi, jnp.where(jnp.logical_and(on, still), 1, 0).astype(I32)

    def body(st):
        it, lo, hi, active = st
        for _ in range(STEPS_PER_TRIP):
            lo, hi, active = step(lo, hi, active)
        return it + STEPS_PER_TRIP, lo, hi, active

    _, lo, _, _ = lax.while_loop(cond, body, (jnp.int32(0), smin, _above(smax), active0.astype(I32)))
    return lo


def _head_slab(x_ref, h):
    j = h // 2
    return x_ref[:, j * LANES:(j + 1) * LANES]


def _place_half(slab, src_half, dst_half):
    lane = lax.broadcasted_iota(I32, slab.shape, 1)
    x = slab if src_half == dst_half else pltpu.roll(slab, HEAD_DIM, 1)
    return jnp.where((lane >= HEAD_DIM) == (dst_half == 1), x, 0.0)


def _gather_heads(o, rows):
    outs = []
    for j in range(N_HEADS // 2):
        n = (2 * j) // GROUP
        tc, th = n // 2, n % 2
        a0 = o[(2 * j) * rows:(2 * j + 1) * rows, tc * LANES:(tc + 1) * LANES]
        a1 = o[(2 * j + 1) * rows:(2 * j + 2) * rows, tc * LANES:(tc + 1) * LANES]
        x0 = a0 if th == 0 else pltpu.roll(a0, HEAD_DIM, 1)
        x1 = a1 if th == 1 else pltpu.roll(a1, HEAD_DIM, 1)
        lane = lax.broadcasted_iota(I32, x0.shape, 1)
        outs.append(jnp.where(lane < HEAD_DIM, x0, x1))
    return outs


def _build_qbd(q_ref, qbd_ref, rows):
    for h in range(N_HEADS):
        n = h // GROUP
        tc, th = n // 2, n % 2
        placed = _place_half(_head_slab(q_ref, h) * ATTN_SCALE, h % 2, th).astype(qbd_ref.dtype)
        for c in range(D_KV // LANES):
            qbd_ref[h * rows:(h + 1) * rows, c * LANES:(c + 1) * LANES] = (
                placed if c == tc else jnp.zeros_like(placed))


def _softmax_step(a3, m_ref, l_ref, acc_ref, vc, rows, v_transposed):
    n = N_HEADS * rows
    ch = a3.shape[2]
    m_prev = m_ref[...].reshape(N_HEADS, rows, 1)
    m_new = jnp.maximum(m_prev, jnp.max(a3, axis=2, keepdims=True))
    p = jnp.exp(a3 - m_new)
    alpha = jnp.exp(m_prev - m_new)
    l_ref[...] = (alpha * l_ref[...].reshape(N_HEADS, rows, 1)
                  + jnp.sum(p, axis=2, keepdims=True)).reshape(n, 1)
    m_ref[...] = m_new.reshape(n, 1)
    pb = p.reshape(n, ch).astype(BF16)
    if v_transposed:
        pv = lax.dot_general(pb, vc, NT_DIMS, preferred_element_type=F32)
    else:
        pv = jnp.dot(pb, vc, preferred_element_type=F32)
    acc_ref[...] = alpha.reshape(n, 1) * acc_ref[...] + pv


ATT_CH = 512
CNT_ROWS = 8 * SUBLANES
N_ATT_CH = R // ATT_CH
NT_DIMS = (((1,), (1,)), ((), ()))
BOUND_MARGIN = 1.0 + 2.0 ** -5
MIN_ROW_SUM = 1e-30


def _pair_loop(n, body, carry):
    def two(j, carry):
        return body(2 * j + 1, body(2 * j, carry, 0), 1)

    carry = lax.fori_loop(0, n // 2, two, carry)
    return lax.cond(n % 2 == 1, lambda x: body(n - 1, x, 0), lambda x: x, carry)


def _inclusive_prefix_matrix(n):
    r_i = lax.broadcasted_iota(I32, (n, n), 0)
    c_i = lax.broadcasted_iota(I32, (n, n), 1)
    return jnp.where(r_i <= c_i, 1.0, 0.0).astype(BF16)


def _attn_prompt_kernel(qi_ref, wi_ref, q_ref, kib_ref, kb_ref, vb_ref, o_ref, lmin_ref,
                        s_ref, st_ref, qim_ref, wb_ref, qbd_ref, m_ref, l_ref, acc_ref, p_ref,
                        kmax_ref, tri_ref, *, bound_max):
    i = pl.program_id(0)

    @pl.when(i >= N_QBLK)
    def _():
        o_ref[...] = jnp.zeros_like(o_ref)
        lmin_ref[...] = jnp.ones_like(lmin_ref)

    @pl.when(i == 0)
    def _():
        tri_ref[...] = _inclusive_prefix_matrix(ATT_CH)

    if bound_max:
        @pl.when(i == 0)
        def _():
            r_i = lax.broadcasted_iota(I32, (D_KV, LANES), 0)
            c_i = lax.broadcasted_iota(I32, (D_KV, LANES), 1)
            sel = jnp.where(r_i // HEAD_DIM == c_i, 1.0, 0.0).astype(BF16)

            def body(c, mx):
                kc = kb_ref[pl.ds(pl.multiple_of(c * ATT_CH, ATT_CH), ATT_CH), :].astype(F32)
                n2 = jnp.dot((kc * kc).astype(BF16), sel, preferred_element_type=F32)
                return jnp.maximum(mx, n2)

            mx = lax.fori_loop(0, N_ATT_CH, body, jnp.zeros((ATT_CH, LANES), F32))
            kmax = jnp.max(mx, axis=0, keepdims=True)
            for n in range(N_KV_HEADS):
                kmax_ref[n] = jnp.broadcast_to(kmax[:, n:n + 1], (QB, LANES))

    @pl.when(i < N_QBLK)
    def _():
        n_ch = (i * QB) // ATT_CH + 1
        qrow = i * QB + lax.broadcasted_iota(I32, (QB, 1), 0)

        wi = wi_ref[...]
        for h in range(N_IDX_HEADS):
            qim_ref[h * QB:(h + 1) * QB, :] = _place_half(_head_slab(qi_ref, h), h % 2, h % 2).astype(BF16)
            wb_ref[h] = jnp.broadcast_to(wi[:, h:h + 1], (QB, LANES))
        _build_qbd(q_ref, qbd_ref, QB)

        def score_chunk(c, carry, parity):
            start = pl.multiple_of(c * ATT_CH, ATT_CH)
            kc = kib_ref[pl.ds(start, ATT_CH), :]
            acc = [jnp.zeros((QB, LANES), F32) for _ in range(ATT_CH // LANES)]
            hpd = 4
            for hg in range(N_IDX_HEADS // hpd):
                d = lax.dot_general(qim_ref[hg * hpd * QB:(hg + 1) * hpd * QB, :], kc, NT_DIMS,
                                    preferred_element_type=F32)
                for hh in range(hpd):
                    w = wb_ref[hg * hpd + hh]
                    for t in range(ATT_CH // LANES):
                        acc[t] = acc[t] + w * jnp.maximum(
                            d[hh * QB:(hh + 1) * QB, t * LANES:(t + 1) * LANES], 0.0)
            for t in range(ATT_CH // LANES):
                kpos = start + t * LANES + lax.broadcasted_iota(I32, (QB, LANES), 1)
                sc = jnp.where(kpos <= qrow, acc[t], NEG_INF)
                s_ref[c, :, t * LANES:(t + 1) * LANES] = sc
                st_ref[pl.ds(pl.multiple_of(start + t * LANES, LANES), LANES), :] = sc.T
            return carry

        _pair_loop(n_ch, score_chunk, 0)

        def key_major_chunk(c):
            return st_ref[pl.ds(pl.multiple_of(c * ATT_CH, ATT_CH), ATT_CH), :]

        def fold(x):
            return x.reshape(ATT_CH // CNT_ROWS, CNT_ROWS, QB)

        def count(t, strict):
            def body(c, cnt):
                sc = key_major_chunk(c)
                hit = jnp.where(sc > t if strict else sc >= t, 1.0, 0.0)
                return cnt + jnp.sum(fold(hit), axis=0)

            cnt = lax.fori_loop(0, n_ch, body, jnp.zeros((CNT_ROWS, QB), F32))
            return jnp.sum(cnt, axis=0, keepdims=True)

        def extremes(c, mm):
            sc = key_major_chunk(c)
            return (jnp.minimum(mm[0], jnp.min(fold(jnp.where(sc > NEG_INF, sc, -NEG_INF)), axis=0)),
                    jnp.maximum(mm[1], jnp.max(fold(sc), axis=0)))

        smin, smax = lax.fori_loop(0, n_ch, extremes, (jnp.full((CNT_ROWS, QB), -NEG_INF, F32),
                                                       jnp.full((CNT_ROWS, QB), NEG_INF, F32)))
        smin = jnp.min(smin, axis=0, keepdims=True)
        smax = jnp.max(smax, axis=0, keepdims=True)
        qrow_l = i * QB + lax.broadcasted_iota(I32, (1, QB), 1)
        lo = _bisect_threshold(functools.partial(count, strict=False), smin, smax,
                               jnp.logical_and(qrow_l + 1 > TOPK, smax > smin))
        lo_b = jnp.broadcast_to(lo, (QB, QB)).T

        excess = count(lo, strict=False) > TOPK
        has_excess = jnp.max(jnp.where(excess, 1, 0)) > 0
        above = lax.cond(has_excess, lambda: count(lo, strict=True),
                         lambda: jnp.zeros((1, QB), F32))
        tie_quota = jnp.where(excess, TOPK - above, TOPK)
        quota_b = jnp.broadcast_to(tie_quota, (QB, QB)).T

        m_ref[...] = jnp.full(m_ref.shape, NEG, F32)
        l_ref[...] = jnp.zeros_like(l_ref)
        acc_ref[...] = jnp.zeros_like(acc_ref)
        n_t = ATT_CH // LANES

        def plain_bias(c):
            return [jnp.where(s_ref[c, :, t * LANES:(t + 1) * LANES] >= lo_b, 0.0, NEG)
                    for t in range(n_t)]

        def quota_bias(c, seen):
            keys = [s_ref[c, :, t * LANES:(t + 1) * LANES] for t in range(n_t)]
            eq = [jnp.where(k == lo_b, 1.0, 0.0) for k in keys]
            incl = jnp.dot(jnp.concatenate(eq, axis=1).astype(BF16), tri_ref[...],
                           preferred_element_type=F32)
            bias = []
            for t in range(n_t):
                rank = seen + incl[:, t * LANES:(t + 1) * LANES] - eq[t]
                keep = jnp.logical_or(keys[t] > lo_b,
                                      jnp.logical_and(keys[t] == lo_b, rank < quota_b))
                bias.append(jnp.where(keep, 0.0, NEG))
            return bias, seen + jnp.broadcast_to(incl[:, ATT_CH - 1:ATT_CH], (QB, LANES))

        def masked_logits(c, bias):
            start = pl.multiple_of(c * ATT_CH, ATT_CH)
            a = lax.dot_general(qbd_ref[...], kb_ref[pl.ds(start, ATT_CH), :], NT_DIMS,
                                preferred_element_type=F32)
            return start, [[a[h * QB:(h + 1) * QB, t * LANES:(t + 1) * LANES] + bias[t]
                            for t in range(n_t)] for h in range(N_HEADS)]

        def max_chunk(c, carry):
            _, a = masked_logits(c, plain_bias(c))
            for h in range(N_HEADS):
                rows = slice(h * QB, (h + 1) * QB)
                m = m_ref[rows, :]
                for t in range(n_t):
                    m = jnp.maximum(m, a[h][t])
                m_ref[rows, :] = m
            return carry

        if bound_max:
            for h in range(N_HEADS):
                rows = slice(h * QB, (h + 1) * QB)
                qf = qbd_ref[rows, :].astype(F32)
                qn2 = jnp.sum(qf * qf, axis=1, keepdims=True)
                m_ref[rows, :] = jnp.sqrt(jnp.broadcast_to(qn2, (QB, LANES))
                                          * kmax_ref[h // GROUP]) * BOUND_MARGIN
        else:
            lax.fori_loop(0, n_ch, max_chunk, 0)
            m_ref[...] = jnp.broadcast_to(jnp.max(m_ref[...], axis=1, keepdims=True), m_ref.shape)

        def value_chunk(c, seen, parity, with_quota):
            pb_ref = p_ref.at[parity]
            if with_quota:
                bias, seen = quota_bias(c, seen)
            else:
                bias = plain_bias(c)
            start, a = masked_logits(c, bias)
            for h in range(N_HEADS):
                rows = slice(h * QB, (h + 1) * QB)
                m = m_ref[rows, :]
                l = l_ref[rows, :]
                for t in range(n_t):
                    p = jnp.exp(a[h][t] - m)
                    l = l + p
                    pb_ref[rows, t * LANES:(t + 1) * LANES] = p.astype(BF16)
                l_ref[rows, :] = l
            acc_ref[...] += jnp.dot(pb_ref[...], vb_ref[pl.ds(start, ATT_CH), :],
                                    preferred_element_type=F32)
            return seen

        seen0 = jnp.zeros((QB, LANES), F32)

        @pl.when(has_excess)
        def _():
            lax.fori_loop(0, n_ch, functools.partial(value_chunk, parity=0, with_quota=True), seen0)

        @pl.when(jnp.logical_not(has_excess))
        def _():
            _pair_loop(n_ch, functools.partial(value_chunk, with_quota=False), seen0)

        l = jnp.sum(l_ref[...], axis=1, keepdims=True)
        lmin_ref[...] = jnp.broadcast_to(jnp.min(l, axis=0, keepdims=True), lmin_ref.shape)
        o = acc_ref[...] / l
        for j, slab in enumerate(_gather_heads(o, QB)):
            o_ref[:, j * LANES:(j + 1) * LANES] = slab.astype(BF16)


def _attn_prompt(qi, wi, q, kib, kb, vb, bound_max):
    row = lambda i: (i, 0)
    full = lambda i: (0, 0)
    nrow = N_HEADS * QB
    return pl.pallas_call(
        functools.partial(_attn_prompt_kernel, bound_max=bound_max),
        grid=(R // QB,),
        in_specs=[
            pl.BlockSpec((QB, N_IDX_HEADS * IDX_DIM), row),
            pl.BlockSpec((QB, LANES), row),
            pl.BlockSpec((QB, D_ATTN), row),
            pl.BlockSpec((R, LANES), full),
            pl.BlockSpec((R, D_KV), full),
            pl.BlockSpec((R, D_KV), full),
        ],
        out_specs=[pl.BlockSpec((QB, D_ATTN), row),
                   pl.BlockSpec((1, SUBLANES, LANES), lambda i: (i, 0, 0))],
        out_shape=(jax.ShapeDtypeStruct((R, D_ATTN), BF16),
                   jax.ShapeDtypeStruct((R // QB, SUBLANES, LANES), F32)),
        scratch_shapes=[
            pltpu.VMEM((N_ATT_CH, QB, ATT_CH), F32),
            pltpu.VMEM((R, QB), F32),
            pltpu.VMEM((nrow, LANES), BF16),
            pltpu.VMEM((N_IDX_HEADS, QB, LANES), F32),
            pltpu.VMEM((nrow, D_KV), BF16),
            pltpu.VMEM((nrow, LANES), F32),
            pltpu.VMEM((nrow, LANES), F32),
            pltpu.VMEM((nrow, D_KV), F32),
            pltpu.VMEM((2, nrow, ATT_CH), BF16),
            pltpu.VMEM((N_KV_HEADS, QB, LANES), F32),
            pltpu.VMEM((ATT_CH, ATT_CH), BF16),
        ],
        compiler_params=_cparams(("arbitrary",)),
        name="attn_prompt_bound" if bound_max else "attn_prompt",
    )(qi, wi, q, kib, kb, vb)


S_CH = 8192
S_NCH = PAST_LEN // S_CH
S_PPC = S_CH // PAGE_SIZE
DMA_UNROLL = 8
S_TIE = 512
S_ROWS = N_HEADS * DEC_SEQ


def _page_lanes(p):
    return pl.ds(pl.multiple_of(p * PAGE_SIZE, PAGE_SIZE), PAGE_SIZE)


def _kidx_page_copy(pt_ref, kidx_hbm, kbuf, sem, b, p, slot):
    return pltpu.make_async_copy(
        kidx_hbm.at[pt_ref[b, p]], kbuf.at[slot, :, _page_lanes(p)], sem.at[slot])


def _sample_score_kernel(pt_ref, qi_ref, wi_ref, kin_ref, kidx_hbm, s_ref, sn_ref,
                         kbuf, sem, knp_ref):
    b = pl.program_id(0)
    nb = pl.num_programs(0)
    slot = b % 2

    def start_batch(bb, sl):
        def body(p, c):
            _kidx_page_copy(pt_ref, kidx_hbm, kbuf, sem, bb, p, sl).start()
            return c
        lax.fori_loop(0, N_PAGES, body, 0, unroll=DMA_UNROLL)

    @pl.when(b == 0)
    def _():
        start_batch(0, 0)

    @pl.when(b + 1 < nb)
    def _():
        start_batch(b + 1, 1 - slot)

    def wait_body(p, c):
        _kidx_page_copy(pt_ref, kidx_hbm, kbuf, sem, b, p, slot).wait()
        return c
    lax.fori_loop(0, N_PAGES, wait_body, 0, unroll=DMA_UNROLL)

    qi = qi_ref[0]
    qis = jnp.concatenate(
        [qi[:, h * IDX_DIM:(h + 1) * IDX_DIM] for h in range(N_IDX_HEADS)], axis=0).astype(BF16)
    wi = wi_ref[0]
    wb = [jnp.broadcast_to(wi[:, h:h + 1], (DEC_SEQ, LANES)) for h in range(N_IDX_HEADS)]

    def head_sum(d, width):
        outs = []
        for t in range(width // LANES):
            acc = jnp.zeros((DEC_SEQ, LANES), F32)
            for h in range(N_IDX_HEADS):
                acc = acc + wb[h] * jnp.maximum(
                    d[h * DEC_SEQ:(h + 1) * DEC_SEQ, t * LANES:(t + 1) * LANES], 0.0)
            outs.append(acc)
        return outs

    def score_chunk(c, carry):
        kc = kbuf[slot, :, pl.ds(pl.multiple_of(c * S_CH, S_CH), S_CH)].astype(BF16)
        d = jnp.dot(qis, kc, preferred_element_type=F32)
        for t, acc in enumerate(head_sum(d, S_CH)):
            s_ref[c, :, t * LANES:(t + 1) * LANES] = acc
        return carry
    lax.fori_loop(0, S_NCH, score_chunk, 0)

    knp_ref[...] = jnp.zeros_like(knp_ref)
    knp_ref[:DEC_SEQ, :] = kin_ref[0][:, :IDX_DIM]
    d = lax.dot_general(qis, knp_ref[...].astype(BF16), NT_DIMS, preferred_element_type=F32)
    kpos = lax.broadcasted_iota(I32, (DEC_SEQ, LANES), 1)
    srow = lax.broadcasted_iota(I32, (DEC_SEQ, LANES), 0)
    sn_ref[...] = jnp.where(kpos <= srow, head_sum(d, LANES)[0], NEG_INF)


def _sample_scores(page_table, qi_s, wi_s, ki_s, cache_kidx):
    blk = lambda b, pt: (b, 0, 0)
    grid_spec = pltpu.PrefetchScalarGridSpec(
        num_scalar_prefetch=1,
        grid=(DEC_BATCH,),
        in_specs=[
            pl.BlockSpec((1, DEC_SEQ, N_IDX_HEADS * IDX_DIM), blk),
            pl.BlockSpec((1, DEC_SEQ, LANES), blk),
            pl.BlockSpec((1, DEC_SEQ, LANES), blk),
            pl.BlockSpec(memory_space=pl.ANY),
        ],
        out_specs=[
            pl.BlockSpec((S_NCH, DEC_SEQ, S_CH), lambda b, pt: (0, b, 0)),
            pl.BlockSpec((DEC_SEQ, LANES), lambda b, pt: (b, 0)),
        ],
        scratch_shapes=[
            pltpu.VMEM((2, IDX_DIM, PAST_LEN), F32),
            pltpu.SemaphoreType.DMA((2,)),
            pltpu.VMEM((LANES, IDX_DIM), F32),
        ],
    )
    return pl.pallas_call(
        _sample_score_kernel,
        grid_spec=grid_spec,
        out_shape=(
            jax.ShapeDtypeStruct((S_NCH, NS, S_CH), F32),
            jax.ShapeDtypeStruct((NS, LANES), F32),
        ),
        compiler_params=_cparams(("arbitrary",)),
        name="sample_scores",
    )(page_table, qi_s, wi_s, ki_s, cache_kidx)


def _sample_threshold_kernel(s_ref, sn_ref, bp_ref, bn_ref):
    def lane_fold(x, op):
        out = x[:, :LANES]
        for t in range(1, S_CH // LANES):
            out = op(out, x[:, t * LANES:(t + 1) * LANES])
        return out

    def count(t, strict):
        def hits(sc):
            return jnp.where(sc > t if strict else sc >= t, 1.0, 0.0)

        cnt = lax.fori_loop(0, S_NCH, lambda c, cnt: cnt + lane_fold(hits(s_ref[c]), jnp.add),
                            hits(sn_ref[...]))
        return jnp.sum(cnt, axis=1, keepdims=True)

    def extremes(c, mm):
        return (jnp.minimum(mm[0], lane_fold(s_ref[c], jnp.minimum)),
                jnp.maximum(mm[1], lane_fold(s_ref[c], jnp.maximum)))

    new = sn_ref[...]
    smin, smax = lax.fori_loop(0, S_NCH, extremes, (jnp.where(new > NEG_INF, new, -NEG_INF), new))
    smin = jnp.min(smin, axis=1, keepdims=True)
    smax = jnp.max(smax, axis=1, keepdims=True)
    lo = _bisect_threshold(functools.partial(count, strict=False), smin, smax, smax > smin)

    quota = jnp.where(count(lo, strict=False) > TOPK, TOPK - count(lo, strict=True), TOPK)
    tri = _inclusive_prefix_matrix(S_TIE)

    def mask_tile(keys, seen, tri_t):
        eq = jnp.where(keys == lo, 1.0, 0.0)
        incl = jnp.dot(eq.astype(BF16), tri_t, preferred_element_type=F32)
        keep = jnp.logical_or(keys > lo, jnp.logical_and(keys == lo, seen + incl - eq < quota))
        return jnp.where(keep, 0.0, NEG), seen + incl[:, keys.shape[1] - 1:]

    def chunk(c, seen):
        for t in range(S_CH // S_TIE):
            cols = slice(t * S_TIE, (t + 1) * S_TIE)
            bp_ref[c, :, cols], seen = mask_tile(s_ref[c, :, cols], seen, tri)
        return seen

    seen = lax.fori_loop(0, S_NCH, chunk, jnp.zeros((NS, 1), F32))
    bn_ref[...], _ = mask_tile(sn_ref[...], seen, tri[:LANES, :LANES])


def _sample_threshold(keys_past, keys_new):
    vmem = pl.BlockSpec(memory_space=pltpu.VMEM)
    return pl.pallas_call(
        _sample_threshold_kernel,
        in_specs=[vmem, vmem],
        out_specs=[vmem, vmem],
        out_shape=(
            jax.ShapeDtypeStruct((S_NCH, NS, S_CH), F32),
            jax.ShapeDtypeStruct((NS, LANES), F32),
        ),
        compiler_params=pltpu.CompilerParams(vmem_limit_bytes=VMEM_LIMIT),
        name="sample_threshold",
    )(keys_past, keys_new)


def _kv_page_copies(pt_ref, ck_hbm, cv_hbm, kbuf, vbuf, sem, b, c, p, slot):
    page = pt_ref[b, c * S_PPC + p]
    dst = _page_lanes(p)
    return (pltpu.make_async_copy(ck_hbm.at[page], kbuf.at[slot, :, dst], sem.at[0, slot]),
            pltpu.make_async_copy(cv_hbm.at[page], vbuf.at[slot, :, dst], sem.at[1, slot]))


def _sample_attn_kernel(pt_ref, q_ref, bp_ref, bn_ref, kn_ref, vn_ref, ck_hbm, cv_hbm, o_ref,
                        kbuf, vbuf, sem, qbd_ref, m_ref, l_ref, acc_ref, knp_ref, vnp_ref):
    b = pl.program_id(0)
    c = pl.program_id(1)
    step = b * S_NCH + c
    n_steps = pl.num_programs(0) * S_NCH
    slot = step % 2

    def start_chunk(st, sl):
        bb = st // S_NCH
        cc = st % S_NCH

        def body(p, carry):
            for cp in _kv_page_copies(pt_ref, ck_hbm, cv_hbm, kbuf, vbuf, sem, bb, cc, p, sl):
                cp.start()
            return carry
        lax.fori_loop(0, S_PPC, body, 0, unroll=DMA_UNROLL)

    @pl.when(step == 0)
    def _():
        start_chunk(0, 0)

    @pl.when(step + 1 < n_steps)
    def _():
        start_chunk(step + 1, 1 - slot)

    def wait_body(p, carry):
        for cp in _kv_page_copies(pt_ref, ck_hbm, cv_hbm, kbuf, vbuf, sem, b, c, p, slot):
            cp.wait()
        return carry
    lax.fori_loop(0, S_PPC, wait_body, 0, unroll=DMA_UNROLL)

    @pl.when(c == 0)
    def _():
        _build_qbd(q_ref.at[0], qbd_ref, DEC_SEQ)
        m_ref[...] = jnp.full(m_ref.shape, NEG, F32)
        l_ref[...] = jnp.zeros_like(l_ref)
        acc_ref[...] = jnp.zeros_like(acc_ref)

    def update(kc, vc, bias, feature_major):
        qbd = qbd_ref[...].astype(BF16)
        if feature_major:
            a = jnp.dot(qbd, kc, preferred_element_type=F32)
        else:
            a = lax.dot_general(qbd, kc, NT_DIMS, preferred_element_type=F32)
        a3 = a.reshape(N_HEADS, DEC_SEQ, a.shape[1]) + bias[None]
        _softmax_step(a3, m_ref, l_ref, acc_ref, vc, DEC_SEQ, feature_major)

    update(kbuf[slot].astype(BF16), vbuf[slot].astype(BF16), bp_ref[0], True)

    @pl.when(c == S_NCH - 1)
    def _():
        knp_ref[...] = jnp.zeros_like(knp_ref)
        vnp_ref[...] = jnp.zeros_like(vnp_ref)
        knp_ref[:DEC_SEQ, :] = kn_ref[0]
        vnp_ref[:DEC_SEQ, :] = vn_ref[0]
        update(knp_ref[...].astype(BF16), vnp_ref[...].astype(BF16), bn_ref[...], False)
        o = acc_ref[...] / l_ref[...]
        for j, slab in enumerate(_gather_heads(o, DEC_SEQ)):
            o_ref[0, :, j * LANES:(j + 1) * LANES] = slab


def _sample_attn(page_table, q_s, bias_past, bias_new, k_s, v_s, cache_k, cache_v):
    blk = lambda b, c, pt: (b, 0, 0)
    grid_spec = pltpu.PrefetchScalarGridSpec(
        num_scalar_prefetch=1,
        grid=(DEC_BATCH, S_NCH),
        in_specs=[
            pl.BlockSpec((1, DEC_SEQ, D_ATTN), blk),
            pl.BlockSpec((1, DEC_SEQ, S_CH), lambda b, c, pt: (c, b, 0)),
            pl.BlockSpec((DEC_SEQ, LANES), lambda b, c, pt: (b, 0)),
            pl.BlockSpec((1, DEC_SEQ, D_KV), blk),
            pl.BlockSpec((1, DEC_SEQ, D_KV), blk),
            pl.BlockSpec(memory_space=pl.ANY),
            pl.BlockSpec(memory_space=pl.ANY),
        ],
        out_specs=pl.BlockSpec((1, DEC_SEQ, D_ATTN), blk),
        scratch_shapes=[
            pltpu.VMEM((2, D_KV, S_CH), F32),
            pltpu.VMEM((2, D_KV, S_CH), F32),
            pltpu.SemaphoreType.DMA((2, 2)),
            pltpu.VMEM((S_ROWS, D_KV), F32),
            pltpu.VMEM((S_ROWS, 1), F32),
            pltpu.VMEM((S_ROWS, 1), F32),
            pltpu.VMEM((S_ROWS, D_KV), F32),
            pltpu.VMEM((LANES, D_KV), F32),
            pltpu.VMEM((LANES, D_KV), F32),
        ],
    )
    return pl.pallas_call(
        _sample_attn_kernel,
        grid_spec=grid_spec,
        out_shape=jax.ShapeDtypeStruct((DEC_BATCH, DEC_SEQ, D_ATTN), F32),
        compiler_params=_cparams(("arbitrary", "arbitrary")),
        name="sample_attn",
    )(page_table, q_s, bias_past, bias_new, k_s, v_s, cache_k, cache_v)


def _rope_tables():
    pos = np.zeros((R,), np.float32)
    pos[:T] = np.arange(T)
    pos[ROW_S:ROW_S + NS] = np.tile(PAST_LEN + np.arange(DEC_SEQ), DEC_BATCH)
    half = HEAD_DIM // 2
    inv_freq = ROPE_THETA ** (-(jnp.arange(half, dtype=F32) * 2.0 / HEAD_DIM))
    ang = jnp.asarray(pos)[:, None] * inv_freq
    cos = jnp.tile(jnp.cos(ang), (1, LANES // half))
    sin = jnp.sin(ang)
    sin = jnp.tile(jnp.concatenate([-sin, sin], axis=1), (1, LANES // HEAD_DIM))
    return cos, sin


def kernel(x_prompt, x_sample, cache_k, cache_v, cache_kidx, state_pool, page_table, meta_tokens,
           norm_mix_pre, w_in, idx_k_norm, w_pool, pool_scale, w_attn_o, w_out, norm_mix_post,
           norm_ffn_pre, w_up, w_down, norm_ffn_post):
    x_all = jnp.concatenate([
        meta_tokens, x_prompt[0], jnp.zeros((ROW_S - T, D_MODEL), F32),
        x_sample.reshape(NS, D_MODEL), jnp.zeros((R - ROW_S - NS, D_MODEL), F32)], axis=0)

    wt = jnp.transpose(w_in[0])
    o_u, o_q, o_k, o_v, o_qi, o_ki, o_wi, o_gp = np.cumsum((0, D_POOL, D_ATTN, D_KV, D_KV,
                                                            N_IDX_HEADS * IDX_DIM, IDX_DIM, N_IDX_HEADS))
    n_pad = W_ROW_C - W_B_COLS - (o_v - o_q) - (o_gp - o_qi)
    w_all = jnp.concatenate([
        wt[o_v:o_qi], wt[o_u:o_q], wt[o_q:o_v], wt[o_qi:o_gp],
        jnp.zeros((n_pad, D_MODEL), F32), wt[o_gp:]], axis=0).astype(BF16)
    assert w_all.shape == (W_ROWS, D_MODEL)

    g_pre = norm_mix_pre[0][None]
    cos, sin = _rope_tables()
    gk = jnp.tile(idx_k_norm[0], LANES // IDX_DIM)[None]

    q, k, kb, qi, ki, kib, wi = _proj_a(x_all, g_pre, w_all, cos, sin, gk)
    v, vb, u = _proj_b(x_all, g_pre, w_all)
    gates = _proj_c(x_all, g_pre, w_all)

    def sample_rows(a):
        return a[ROW_S:ROW_S + NS].reshape(DEC_BATCH, DEC_SEQ, a.shape[1])

    pooled = _pool_prompt(u)
    u_s = sample_rows(u)
    state = state_pool[0]
    state16 = jnp.pad(state, ((0, 0), (HALO - POOL_BUF, 0), (0, 0)))
    pooled_s = _pool_sample(u_s, state16)
    pooled = lax.dynamic_update_slice(pooled, pooled_s.reshape(NS, D_POOL), (ROW_S, 0))

    attn, lmin = _attn_prompt(qi, wi, q, kib, kb, vb, bound_max=True)
    attn = lax.cond(jnp.min(lmin) >= MIN_ROW_SUM, lambda: attn,
                    lambda: _attn_prompt(qi, wi, q, kib, kb, vb, bound_max=False)[0])
    k_s, v_s = sample_rows(k), sample_rows(v)
    n_pool = cache_k.shape[1]
    kidx_fm = jnp.transpose(cache_kidx[0], (0, 2, 1))
    k_fm = jnp.transpose(cache_k[0], (0, 2, 3, 1)).reshape(n_pool, D_KV, PAGE_SIZE)
    v_fm = jnp.transpose(cache_v[0], (0, 2, 3, 1)).reshape(n_pool, D_KV, PAGE_SIZE)
    keys_past, keys_new = _sample_scores(page_table, sample_rows(qi), sample_rows(wi), sample_rows(ki),
                                         kidx_fm)
    bias_past, bias_new = _sample_threshold(keys_past, keys_new)
    attn_s = _sample_attn(page_table, sample_rows(q), bias_past, bias_new, k_s, v_s, k_fm, v_fm)
    attn = lax.dynamic_update_slice(attn, attn_s.reshape(NS, D_ATTN).astype(BF16), (ROW_S, 0))

    x1 = _merge(x_all, pooled, attn, gates, w_pool[0].astype(BF16), pool_scale[0][None],
                w_attn_o[0].astype(BF16), w_out[0].astype(BF16), norm_mix_post[0][None])
    y = _ffn(x1, norm_ffn_pre[0][None], w_up[0].astype(BF16), w_down[0].astype(BF16),
             norm_ffn_post[0][None])

    y_prompt = y[N_META:T][None]
    y_sample = y[ROW_S:ROW_S + NS].reshape(DEC_BATCH, DEC_SEQ, D_MODEL)
    kv_shape = (1, 1, T, N_KV_HEADS, HEAD_DIM)
    kv_s_shape = (1, DEC_BATCH, DEC_SEQ, N_KV_HEADS, HEAD_DIM)
    return (
        y_prompt, y_sample,
        k[:T].reshape(kv_shape), v[:T].reshape(kv_shape), ki[:T, :IDX_DIM][None, None],
        u[T - POOL_BUF:T][None, None],
        k_s.reshape(kv_s_shape), v_s.reshape(kv_s_shape), sample_rows(ki)[None, :, :, :IDX_DIM],
        jnp.concatenate([state[:, DEC_SEQ:], u_s], axis=1)[None],
    )
```
